```python
import math
import jax
import jax.numpy as jnp
from jax import lax
import numpy as np


D_MODEL = 2048
BATCH = 16
SEQ = 2048
DEPTH = 1

GRID_W = 64
CTX_LEN = 256
HY_WIDTH = D_MODEL // 2
RW_WIDTH = D_MODEL - HY_WIDTH
HY_GROUP = 64
RW_HEAD = 64
RW_HEADS = RW_WIDTH // RW_HEAD
PROJ_WIDTH = 3 * HY_WIDTH + 3 * RW_WIDTH
FILTER_BANDS = 16
FILTER_EMB = 2 * FILTER_BANDS + 1
FILTER_WIDTH = 64
FILTER_TARGET = 1e-2
FAST_DECAY_PCT = 0.3
SLOW_DECAY_PCT = 1.5
DECAY_LORA = 64
ICLR_LORA = 64
GATE_LORA = 128
N_EXPERTS = 32
TOP_K = 4
D_FF = D_MODEL
SWIGLU_LIMIT = 7.0
SWIGLU_ALPHA = 1.702
EXPERT_BLOCK = 128
NORM_EPS = 1e-6
LNX_EPS = 64e-5

kernel_name = 'hyena_rwkv7_moe_prefix_dit_block'


def rmsnorm(x, g):
    xf = x.astype(jnp.float32)
    y = xf * lax.rsqrt(jnp.mean(xf * xf, axis=-1, keepdims=True) + NORM_EPS)
    return (y * g).astype(x.dtype)


def modulate(h, shift, scale):
    return h * (1 + scale) + shift


def short_conv(z, w, b):
    zp = jnp.pad(z, ((0, 0), (1, 1), (0, 0)))
    return zp[:, :-2] * w[0] + zp[:, 1:-1] * w[1] + zp[:, 2:] * w[2] + b


def group_rmsnorm(y, g, group):
    b_, l_, ch = y.shape
    yf = y.astype(jnp.float32).reshape(b_, l_, ch // group, group)
    yf = yf * lax.rsqrt(jnp.mean(yf * yf, axis=-1, keepdims=True) + NORM_EPS)
    return (yf.reshape(b_, l_, ch) * g).astype(y.dtype)


def both_dirs(a):
    return jnp.stack([a, a[:, ::-1]])


def orient(a):
    return jnp.stack([a[0], a[1][:, ::-1]])


def hyena_filter(length, w1, b1, w2, b2, w3, b3, freq, w4):
    t = jnp.linspace(0.0, 1.0, length, dtype=jnp.float32)[:, None]
    ang = (2.0 * math.pi / length) * jnp.arange(length, dtype=jnp.float32)[:, None]
    bands = jnp.linspace(1e-4, FILTER_BANDS - 1, FILTER_BANDS, dtype=jnp.float32)[None, :]
    feats = jnp.concatenate([t, jnp.cos(bands * ang), -jnp.sin(bands * ang)], axis=-1)
    h = jnp.sin(freq * (feats @ w1 + b1))
    h = jnp.sin(freq * (h @ w2 + b2))
    h = jnp.sin(freq * (h @ w3 + b3))
    h = (h @ w4).astype(jnp.float32)
    deltas = jnp.abs(jnp.linspace(math.log(FILTER_TARGET) / SLOW_DECAY_PCT,
                                  math.log(FILTER_TARGET) / FAST_DECAY_PCT, HY_WIDTH, dtype=jnp.float32))
    h = h * jnp.exp(-t * jnp.tile(deltas, 2))
    return h[:, :HY_WIDTH], h[:, HY_WIDTH:]


def long_conv(u, h_fwd, h_bwd):
    length, ch = h_fwd.shape
    taps = jnp.concatenate([h_fwd, jnp.zeros((1, ch), h_fwd.dtype), h_bwd[:0:-1]], axis=0)
    u_f = jnp.fft.rfft(u.astype(jnp.float32), n=2 * length, axis=1)
    t_f = jnp.fft.rfft(taps, n=2 * length, axis=0)
    return jnp.fft.irfft(u_f * t_f[None], n=2 * length, axis=1)[:, :length].astype(u.dtype)


def hyena_mixer(z, w1, b1, w2, b2, w3, b3, freq, w4, bias, norm_g):
    x0, x1, v = jnp.split(z, 3, axis=-1)
    h_fwd, h_bwd = hyena_filter(z.shape[1], w1, b1, w2, b2, w3, b3, freq, w4)
    u = v * x1
    y = x0 * (long_conv(u, h_fwd, h_bwd) + bias * u)
    return group_rmsnorm(y, norm_g, HY_GROUP)


def rwkv7_prep(z, h, w0, w1, w2, a0, a1, a2, k_k, k_a, r_k):
    b_, l_, _ = z.shape
    r, k, v = jnp.split(z, 3, axis=-1)
    heads = lambda t: t.reshape(*t.shape[:-1], RW_HEADS, RW_HEAD)
    kk = heads((k * k_k).astype(jnp.float32))
    kk = (kk / jnp.maximum(jnp.linalg.norm(kk, axis=-1, keepdims=True), 1e-12)).reshape(b_, l_, RW_WIDTH)
    w_log = -jax.nn.softplus(-(w0[:, None, None, :] + jnp.einsum(
        'nblr,nrc->nblc', jnp.tanh(jnp.einsum('bld,ndr->nblr', h, w1)), w2))) - 0.5
    decay = jnp.exp(-jnp.exp(w_log.astype(jnp.float32)))
    a = jax.nn.sigmoid(a0[:, None, None, :] + jnp.einsum(
        'nblr,nrc->nblc', jnp.einsum('bld,ndr->nblr', h, a1), a2))
    k_dir = k * (1 + (a - 1) * k_a)
    bonus = jnp.sum(heads(r * k_dir) * r_k, axis=-1, keepdims=True) * heads(v)
    bonus = jnp.sum(bonus, axis=0).reshape(b_, l_, RW_WIDTH)
    scan_in = (both_dirs(r), orient(decay), orient(k_dir), both_dirs(v), both_dirs(-kk), orient(kk * a))
    return scan_in, bonus


def rwkv7_scan(r, decay, k, v, aa, bb, s0):
    out_shape = r.shape

    def tm(t):
        return jnp.moveaxis(t.astype(jnp.float32).reshape(*t.shape[:3], RW_HEADS, RW_HEAD), 2, 0)

    def step(s, inp):
        r_t, w_t, k_t, v_t, a_t, b_t = inp
        sa = jnp.einsum('nbhvk,nbhk->nbhv', s, a_t)
        s = s * w_t[..., None, :] + sa[..., :, None] * b_t[..., None, :] + v_t[..., :, None] * k_t[..., None, :]
        return s, jnp.einsum('nbhvk,nbhk->nbhv', s, r_t)

    s_fin, ys = lax.scan(step, s0, (tm(r), tm(decay), tm(k), tm(v), tm(aa), tm(bb)))
    return jnp.moveaxis(ys, 0, 2).reshape(out_shape), s_fin


def rwkv7_out(ys, bonus, h, g1, g2, lnx_g, lnx_b):
    ys = orient(ys)
    b_, l_, _ = h.shape
    y = (ys[0] + ys[1]).reshape(b_, l_, RW_HEADS, RW_HEAD)
    mu = jnp.mean(y, axis=-1, keepdims=True)
    var = jnp.mean(jnp.square(y - mu), axis=-1, keepdims=True)
    y = ((y - mu) * lax.rsqrt(var + LNX_EPS)).reshape(b_, l_, RW_WIDTH) * lnx_g + lnx_b
    gate = jax.nn.sigmoid(h @ g1) @ g2
    return ((y + bonus) * gate).astype(h.dtype)


def token_mixer(hx, hc, in_w, conv_w, conv_b, hy, rw_pre, rw_post, out_w, need_ctx):
    n_hy = 3 * HY_WIDTH
    zx = short_conv(hx @ in_w, conv_w, conv_b)
    zc = short_conv(hc @ in_w, conv_w, conv_b)
    s0 = jnp.zeros((2, hc.shape[0], RW_HEADS, RW_HEAD, RW_HEAD), jnp.float32)
    ins_c, bonus_c = rwkv7_prep(zc[..., n_hy:], hc, *rw_pre)
    ys_c, s_ctx = rwkv7_scan(*ins_c, s0)
    ins_x, bonus_x = rwkv7_prep(zx[..., n_hy:], hx, *rw_pre)
    ys_x, _ = rwkv7_scan(*ins_x, s_ctx)
    y_x = jnp.concatenate([hyena_mixer(zx[..., :n_hy], *hy),
                           rwkv7_out(ys_x, bonus_x, hx, *rw_post)], axis=-1) @ out_w
    if not need_ctx:
        return y_x, None
    y_c = jnp.concatenate([hyena_mixer(zc[..., :n_hy], *hy),
                           rwkv7_out(ys_c, bonus_c, hc, *rw_post)], axis=-1) @ out_w
    return y_x, y_c


def moe_ffn(h, router_w, router_b, w_gate, b_gate, w_up, b_up, w_down, b_down):
    b_, l_, d_ = h.shape
    t = h.reshape(b_ * l_, d_)
    n_tok = t.shape[0]
    logits = (t @ router_w + router_b).astype(jnp.float32)
    top_logits, top_idx = lax.top_k(logits, TOP_K)
    gates = jax.nn.softmax(top_logits, axis=-1).astype(h.dtype)
    n_asg = n_tok * TOP_K
    e_flat = top_idx.reshape(n_asg)
    order = jnp.argsort(e_flat)
    e_sorted = e_flat[order]
    tok_sorted = (order // TOP_K).astype(jnp.int32)
    gate_sorted = gates.reshape(n_asg)[order]
    counts = jnp.bincount(e_flat, length=N_EXPERTS)
    padded = (counts + EXPERT_BLOCK - 1) // EXPERT_BLOCK * EXPERT_BLOCK
    pad_end = jnp.cumsum(padded)
    pad_start = pad_end - padded
    raw_start = jnp.cumsum(counts) - counts
    dest = pad_start[e_sorted] + jnp.arange(n_asg) - raw_start[e_sorted]
    n_blocks = -(-n_asg // EXPERT_BLOCK) + N_EXPERTS
    n_rows = n_blocks * EXPERT_BLOCK
    row_tok = jnp.zeros((n_rows,), jnp.int32).at[dest].set(tok_sorted)
    row_gate = jnp.zeros((n_rows,), h.dtype).at[dest].set(gate_sorted)
    block_expert = jnp.minimum(
        jnp.searchsorted(pad_end, jnp.arange(n_blocks) * EXPERT_BLOCK, side='right'), N_EXPERTS - 1)

    def expert_block(args):
        rows, e = args
        xb = t[rows]
        g = jnp.minimum(xb @ w_gate[e] + b_gate[e], SWIGLU_LIMIT)
        u = jnp.clip(xb @ w_up[e] + b_up[e], -SWIGLU_LIMIT, SWIGLU_LIMIT)
        return ((u + 1) * (g * jax.nn.sigmoid(SWIGLU_ALPHA * g))) @ w_down[e] + b_down[e]

    y = lax.map(expert_block, (row_tok.reshape(n_blocks, EXPERT_BLOCK), block_expert))
    y = y.reshape(n_rows, d_) * row_gate[:, None]
    return jax.ops.segment_sum(y, row_tok, num_segments=n_tok).reshape(b_, l_, d_)


def setup_inputs(seed: int = 0) -> dict:
    key = jax.random.key(seed)
    ks = iter(jax.random.split(key, 48))

    def nrm(shape, scale):
        return scale * jax.random.normal(next(ks), shape, jnp.float32)

    nl, d, e, f = DEPTH, D_MODEL, N_EXPERTS, D_FF
    conv_base = jnp.array([0.25, 0.5, 0.25], jnp.float32)[None, :, None]
    return {
        'x': nrm((BATCH, SEQ, d), 1.0),
        'c': nrm((BATCH, d), 1.0),
        'ctx': nrm((BATCH, CTX_LEN, d), 1.0),
        'c_ctx': nrm((d,), 1.0),
        'ada_w': nrm((nl, d, 6 * d), 0.5 * d ** -0.5),
        'ada_b': nrm((nl, 6 * d), 0.02),
        'norm1_g': 1.0 + nrm((nl, d), 0.05),
        'norm2_g': 1.0 + nrm((nl, d), 0.05),
        'in_w': nrm((nl, d, PROJ_WIDTH), d ** -0.5),
        'conv_w': conv_base + nrm((nl, 3, PROJ_WIDTH), 0.2),
        'conv_b': nrm((nl, PROJ_WIDTH), 0.02),
        'hy_w1': nrm((nl, FILTER_EMB, FILTER_WIDTH), FILTER_EMB ** -0.5),
        'hy_b1': nrm((nl, FILTER_WIDTH), 0.1),
        'hy_w2': nrm((nl, FILTER_WIDTH, FILTER_WIDTH), FILTER_WIDTH ** -0.5),
        'hy_b2': nrm((nl, FILTER_WIDTH), 0.1),
        'hy_w3': nrm((nl, FILTER_WIDTH, FILTER_WIDTH), FILTER_WIDTH ** -0.5),
        'hy_b3': nrm((nl, FILTER_WIDTH), 0.1),
        'hy_freq': 1.0 + nrm((nl, FILTER_WIDTH), 0.1),
        'hy_w4': nrm((nl, FILTER_WIDTH, 2 * HY_WIDTH), FILTER_WIDTH ** -0.5),
        'hy_bias': nrm((nl, HY_WIDTH), 0.5),
        'hy_norm_g': 1.0 + nrm((nl, HY_WIDTH), 0.05),
        'rw_w0': -1.0 + nrm((nl, 2, RW_WIDTH), 0.5),
        'rw_w1': nrm((nl, 2, d, DECAY_LORA), d ** -0.5),
        'rw_w2': nrm((nl, 2, DECAY_LORA, RW_WIDTH), 0.5 * DECAY_LORA ** -0.5),
        'rw_a0': nrm((nl, 2, RW_WIDTH), 0.5),
        'rw_a1': nrm((nl, 2, d, ICLR_LORA), d ** -0.5),
        'rw_a2': nrm((nl, 2, ICLR_LORA, RW_WIDTH), 0.5 * ICLR_LORA ** -0.5),
        'rw_kk': 0.85 + nrm((nl, RW_WIDTH), 0.05),
        'rw_ka': 1.0 + nrm((nl, RW_WIDTH), 0.05),
        'rw_rk': nrm((nl, RW_HEADS, RW_HEAD), 0.1),
        'rw_g1': nrm((nl, d, GATE_LORA), d ** -0.5),
        'rw_g2': nrm((nl, GATE_LORA, RW_WIDTH), GATE_LORA ** -0.5),
        'rw_lnx_g': 1.0 + nrm((nl, RW_WIDTH), 0.05),
        'rw_lnx_b': nrm((nl, RW_WIDTH), 0.02),
        'out_w': nrm((nl, HY_WIDTH + RW_WIDTH, d), (HY_WIDTH + RW_WIDTH) ** -0.5),
        'router_w': nrm((nl, d, e), d ** -0.5),
        'router_b': nrm((nl, e), 0.01),
        'ex_w_gate': nrm((nl, e, d, f), d ** -0.5),
        'ex_b_gate': nrm((nl, e, f), 0.01),
        'ex_w_up': nrm((nl, e, d, f), d ** -0.5),
        'ex_b_up': nrm((nl, e, f), 0.01),
        'ex_w_down': nrm((nl, e, f, d), f ** -0.5),
        'ex_b_down': nrm((nl, e, d), 0.01),
        'final_g': 1.0 + nrm((d,), 0.05),
    }


def reference(x, c, ctx, c_ctx, ada_w, ada_b, norm1_g, norm2_g, in_w, conv_w, conv_b,
              hy_w1, hy_b1, hy_w2, hy_b2, hy_w3, hy_b3, hy_freq, hy_w4, hy_bias, hy_norm_g,
              rw_w0, rw_w1, rw_w2, rw_a0, rw_a1, rw_a2, rw_kk, rw_ka, rw_rk,
              rw_g1, rw_g2, rw_lnx_g, rw_lnx_b, out_w,
              router_w, router_b, ex_w_gate, ex_b_gate, ex_w_up, ex_b_up, ex_w_down, ex_b_down,
              final_g):
    cond_x = jax.nn.silu(c)[:, None, :]
    cond_c = jax.nn.silu(c_ctx)[None, None, :]
    for i in range(DEPTH):
        need_ctx = i < DEPTH - 1
        mod_x = jnp.split(cond_x @ ada_w[i] + ada_b[i], 6, axis=-1)
        mod_c = jnp.split(cond_c @ ada_w[i] + ada_b[i], 6, axis=-1)
        hx = modulate(rmsnorm(x, norm1_g[i]), mod_x[0], mod_x[1])
        hc = modulate(rmsnorm(ctx, norm1_g[i]), mod_c[0], mod_c[1])
        hy = (hy_w1[i], hy_b1[i], hy_w2[i], hy_b2[i], hy_w3[i], hy_b3[i], hy_freq[i], hy_w4[i],
              hy_bias[i], hy_norm_g[i])
        rw_pre = (rw_w0[i], rw_w1[i], rw_w2[i], rw_a0[i], rw_a1[i], rw_a2[i], rw_kk[i], rw_ka[i], rw_rk[i])
        rw_post = (rw_g1[i], rw_g2[i], rw_lnx_g[i], rw_lnx_b[i])
        mix_x, mix_c = token_mixer(hx, hc, in_w[i], conv_w[i], conv_b[i], hy, rw_pre, rw_post,
                                   out_w[i], need_ctx)
        x = x + mod_x[2] * mix_x
        moe_w = (router_w[i], router_b[i], ex_w_gate[i], ex_b_gate[i], ex_w_up[i], ex_b_up[i],
                 ex_w_down[i], ex_b_down[i])
        x = x + mod_x[5] * moe_ffn(modulate(rmsnorm(x, norm2_g[i]), mod_x[3], mod_x[4]), *moe_w)
        if need_ctx:
            ctx = ctx + mod_c[2] * mix_c
            ctx = ctx + mod_c[5] * moe_ffn(modulate(rmsnorm(ctx, norm2_g[i]), mod_c[3], mod_c[4]), *moe_w)
    return rmsnorm(x, final_g)
```

```python
import functools
import math

import jax
import jax.numpy as jnp
from jax import lax
from jax.experimental import pallas as pl
from jax.experimental.pallas import tpu as pltpu

F32 = jnp.float32
BF16 = jnp.bfloat16

HEAD = 64
CHUNK = 64
HEADS_PER_STEP = 4
FILTER_BANDS = 16
FILTER_TARGET = 1e-2
FAST_DECAY_PCT = 0.3
SLOW_DECAY_PCT = 1.5
TOP_K = 4
SWIGLU_LIMIT = 7.0
SWIGLU_ALPHA = 1.702
NORM_EPS = 1e-6
LNX_EPS = 64e-5
MOE_TILE = 512
VMEM_LIMIT = 56 * 1024 * 1024


def _cparams(sem):
    return pltpu.CompilerParams(dimension_semantics=sem, vmem_limit_bytes=VMEM_LIMIT)


def _mm_kernel(a_ref, b_ref, o_ref):
    o_ref[...] = jnp.dot(a_ref[...].astype(BF16), b_ref[...].astype(BF16),
                         preferred_element_type=F32).astype(o_ref.dtype)


def _tile(n, want):
    if n <= want:
        return n
    t = want
    while n % t:
        t //= 2
    assert t >= 8, (n, want)
    return t


def _matmul(a, b, out_dtype, tm, tn):
    m, k = a.shape
    n = b.shape[1]
    tm, tn = _tile(m, tm), _tile(n, tn)
    return pl.pallas_call(
        _mm_kernel,
        grid=(n // tn, m // tm),
        in_specs=[pl.BlockSpec((tm, k), lambda j, i: (i, 0)),
                  pl.BlockSpec((k, tn), lambda j, i: (0, j))],
        out_specs=pl.BlockSpec((tm, tn), lambda j, i: (i, j)),
        out_shape=jax.ShapeDtypeStruct((m, n), out_dtype),
        compiler_params=_cparams(("parallel", "parallel")),
    )(a, b)


def _bmm_shared_lhs(a, b, out_dtype, tm, tn):
    m, k = a.shape
    nb, _, n = b.shape
    tm, tn = _tile(m, tm), _tile(n, tn)
    return pl.pallas_call(
        _mm_kernel,
        grid=(m // tm, nb, n // tn),
        in_specs=[pl.BlockSpec((tm, k), lambda i, bb, j: (i, 0)),
                  pl.BlockSpec((None, k, tn), lambda i, bb, j: (bb, 0, j))],
        out_specs=pl.BlockSpec((None, tm, tn), lambda i, bb, j: (bb, i, j)),
        out_shape=jax.ShapeDtypeStruct((nb, m, n), out_dtype),
        compiler_params=_cparams(("parallel", "parallel", "parallel")),
    )(a, b)


def _scan_kernel(r_ref, v_ref, an_ref, lw_ref, k_ref, bb_ref, y_ref, ht_ref):
    c = CHUNK
    w = HEADS_PER_STEP * HEAD
    direction = pl.program_id(0)

    @pl.when(pl.program_id(3) == 0)
    def _():
        ht_ref[...] = jnp.zeros_like(ht_ref)

    sign = 1 - 2 * direction
    r = r_ref[...]
    v = v_ref[...]
    an = an_ref[...]
    lw = lw_ref[...]
    kd = k_ref[...]
    bb = bb_ref[...]

    d_sq = (lax.broadcasted_iota(jnp.int32, (c, c), 1) - lax.broadcasted_iota(jnp.int32, (c, c), 0)) * sign
    incl_sq = jnp.where(d_sq <= 0, 1.0, 0.0).astype(BF16)
    d_c = (lax.broadcasted_iota(jnp.int32, (c, w), 1) % c - lax.broadcasted_iota(jnp.int32, (c, w), 0)) * sign
    strict_c = d_c < 0
    incl_c = d_c <= 0
    eye_c = jnp.where(d_c == 0, 1.0, 0.0)
    bd_mask = (lax.broadcasted_iota(jnp.int32, (w, w), 0) // c) == (lax.broadcasted_iota(jnp.int32, (w, w), 1) // HEAD)

    def bd(x):
        xb = x.astype(BF16)
        return jnp.where(bd_mask, jnp.concatenate([xb] * HEADS_PER_STEP, axis=0), jnp.zeros((), BF16))

    def mm(a, b):
        return jnp.dot(a.astype(BF16), b.astype(BF16), preferred_element_type=F32)

    def mm_nt(a, b):
        return lax.dot_general(a.astype(BF16), b.astype(BF16), (((1,), (1,)), ((), ())),
                               preferred_element_type=F32)

    lw_hi = lw.astype(BF16)
    lw_lo = (lw - lw_hi.astype(F32)).astype(BF16)
    cum = (jnp.dot(incl_sq, lw_hi, preferred_element_type=F32)
           + jnp.dot(incl_sq, lw_lo, preferred_element_type=F32))
    tot = jnp.sum(lw, axis=0, keepdims=True)
    g_inv = jnp.exp(-cum)
    g_rem = jnp.exp(tot - cum)
    a_t = an * jnp.exp(cum - lw)
    r_t = r * jnp.exp(cum)
    b_t = bb * g_inv
    k_t = kd * g_inv
    b_s = bb * g_rem
    k_s = kd * g_rem

    ht = ht_ref[...]
    lhs = jnp.concatenate([a_t, r_t], axis=0)
    rhs = jnp.concatenate([bd(b_t), bd(k_t)], axis=0)
    scores = mm_nt(lhs, rhs)
    a_ab = jnp.where(strict_c, scores[:c, :w], 0.0)
    a_ak = jnp.where(strict_c, scores[:c, w:], 0.0)
    a_rb = jnp.where(incl_c, scores[c:, :w], 0.0)
    a_rk = jnp.where(incl_c, scores[c:, w:], 0.0)
    h0 = mm_nt(lhs, ht)
    bd_v = bd(v)
    x = h0[:c] + mm(a_ak, bd_v)

    t_inv = eye_c + a_ab
    p = mm(a_ab, bd(a_ab))
    n_sq = int(math.log2(c)) - 1
    for it in range(n_sq):
        bd_p = bd(p)
        if it + 1 < n_sq:
            pt = mm(jnp.concatenate([p, t_inv], axis=0), bd_p)
            p = pt[:c]
            t_inv = t_inv + pt[c:]
        else:
            t_inv = t_inv + mm(t_inv, bd_p)
    u = mm(t_inv, bd(x))

    y = h0[c:] + mm(jnp.concatenate([a_rb, a_rk], axis=1), jnp.concatenate([bd(u), bd_v], axis=0))
    y_ref[...] = y

    uv = jnp.concatenate([u, v], axis=0).astype(BF16)
    bk = jnp.concatenate([b_s, k_s], axis=0).astype(BF16)
    upd = lax.dot_general(uv, bk, (((0,), (0,)), ((), ())), preferred_element_type=F32)
    ht_ref[...] = ht * jnp.exp(tot) + jnp.where(bd_mask, upd, 0.0)


def _rwkv_scan(r, v, an, lw, kd, bb, n_ctx):
    nb, lt, ch = r.shape
    c = CHUNK
    w = HEADS_PER_STEP * HEAD
    assert CHUNK == HEAD and ch % w == 0 and n_ctx % c == 0 and lt % c == 0
    nc_ctx = n_ctx // c
    nc = lt // c

    def chunk_of(n, i):
        bwd = jnp.where(i < nc_ctx, nc_ctx - 1 - i, nc_ctx + nc - 1 - i)
        return jnp.where(n == 0, i, bwd)

    shared = pl.BlockSpec((None, c, w), lambda n, b, g, i: (b, chunk_of(n, i), g))
    per_dir = pl.BlockSpec((None, None, c, w), lambda n, b, g, i: (n, b, chunk_of(n, i), g))
    return pl.pallas_call(
        _scan_kernel,
        grid=(2, nb, ch // w, nc),
        in_specs=[shared, shared, shared, per_dir, per_dir, per_dir],
        out_specs=per_dir,
        out_shape=jax.ShapeDtypeStruct((2, nb, lt, ch), F32),
        scratch_shapes=[pltpu.VMEM((w, w), F32)],
        compiler_params=_cparams(("parallel", "parallel", "parallel", "arbitrary")),
    )(r, v, an, lw, kd, bb)


def _expert_changed(te_ref, i):
    prev = te_ref[jnp.maximum(i - 1, 0)]
    return jnp.logical_or(i == 0, te_ref[i] != prev)


def _moe_up_kernel(te_ref, nu_ref, x_ref, wg_ref, wu_ref, bg_ref, bu_ref, h_ref, wg_bf, wu_bf):
    i = pl.program_id(1)

    @pl.when(_expert_changed(te_ref, i))
    def _():
        wg_bf[...] = wg_ref[...].astype(BF16)
        wu_bf[...] = wu_ref[...].astype(BF16)

    @pl.when(i < nu_ref[0])
    def _():
        x = x_ref[...]
        g = jnp.dot(x, wg_bf[...], preferred_element_type=F32) + bg_ref[...]
        u = jnp.dot(x, wu_bf[...], preferred_element_type=F32) + bu_ref[...]
        g = jnp.minimum(g, SWIGLU_LIMIT)
        u = jnp.clip(u, -SWIGLU_LIMIT, SWIGLU_LIMIT)
        h_ref[...] = ((u + 1.0) * (g * jax.nn.sigmoid(SWIGLU_ALPHA * g))).astype(h_ref.dtype)

    @pl.when(i >= nu_ref[0])
    def _():
        h_ref[...] = jnp.zeros_like(h_ref)


def _moe_down_kernel(te_ref, nu_ref, h_ref, wd_ref, bd_ref, y_ref, wd_bf):
    i = pl.program_id(1)

    @pl.when(_expert_changed(te_ref, i))
    def _():
        wd_bf[...] = wd_ref[...].astype(BF16)

    @pl.when(i < nu_ref[0])
    def _():
        y = jnp.dot(h_ref[...], wd_bf[...], preferred_element_type=F32) + bd_ref[...]
        y_ref[...] = y.astype(y_ref.dtype)

    @pl.when(i >= nu_ref[0])
    def _():
        y_ref[...] = jnp.zeros_like(y_ref)


def _moe_up(tile_expert, n_used, xs, wg, wu, bg, bu, tn):
    n_rows, d = xs.shape
    e, _, f = wg.shape
    tm = MOE_TILE
    tn = _tile(f, tn)
    w_spec = pl.BlockSpec((None, d, tn), lambda j, i, te, nu: (te[i], 0, j))
    b_spec = pl.BlockSpec((None, 1, tn), lambda j, i, te, nu: (te[i], 0, j))
    return pl.pallas_call(
        _moe_up_kernel,
        grid_spec=pltpu.PrefetchScalarGridSpec(
            num_scalar_prefetch=2,
            grid=(f // tn, n_rows // tm),
            in_specs=[pl.BlockSpec((tm, d), lambda j, i, te, nu: (i, 0)), w_spec, w_spec, b_spec, b_spec],
            out_specs=pl.BlockSpec((tm, tn), lambda j, i, te, nu: (i, j)),
            scratch_shapes=[pltpu.VMEM((d, tn), BF16), pltpu.VMEM((d, tn), BF16)]),
        out_shape=jax.ShapeDtypeStruct((n_rows, f), BF16),
        compiler_params=_cparams(("arbitrary", "arbitrary")),
    )(tile_expert, n_used, xs, wg, wu, bg.reshape(e, 1, f), bu.reshape(e, 1, f))


def _moe_down(tile_expert, n_used, hs, wd, bdn, tn):
    n_rows, f = hs.shape
    e, _, d = wd.shape
    tm = MOE_TILE
    tn = _tile(d, tn)
    return pl.pallas_call(
        _moe_down_kernel,
        grid_spec=pltpu.PrefetchScalarGridSpec(
            num_scalar_prefetch=2,
            grid=(d // tn, n_rows // tm),
            in_specs=[pl.BlockSpec((tm, f), lambda j, i, te, nu: (i, 0)),
                      pl.BlockSpec((None, f, tn), lambda j, i, te, nu: (te[i], 0, j)),
                      pl.BlockSpec((None, 1, tn), lambda j, i, te, nu: (te[i], 0, j))],
            out_specs=pl.BlockSpec((tm, tn), lambda j, i, te, nu: (i, j)),
            scratch_shapes=[pltpu.VMEM((f, tn), BF16)]),
        out_shape=jax.ShapeDtypeStruct((n_rows, d), BF16),
        compiler_params=_cparams(("arbitrary", "arbitrary")),
    )(tile_expert, n_used, hs, wd, bdn.reshape(e, 1, d))


def _rmsnorm(x, g):
    return x * lax.rsqrt(jnp.mean(x * x, axis=-1, keepdims=True) + NORM_EPS) * g


def _short_conv(z, w, b):
    zp = jnp.pad(z, ((0, 0), (1, 1), (0, 0)))
    return zp[:, :-2] * w[0] + zp[:, 1:-1] * w[1] + zp[:, 2:] * w[2] + b


def _hyena_filter(length, w1, b1, w2, b2, w3, b3, freq, w4):
    hy = w4.shape[1] // 2
    t = jnp.linspace(0.0, 1.0, length, dtype=F32)[:, None]
    ang = (2.0 * math.pi / length) * jnp.arange(length, dtype=F32)[:, None]
    bands = jnp.linspace(1e-4, FILTER_BANDS - 1, FILTER_BANDS, dtype=F32)[None, :]
    feats = jnp.concatenate([t, jnp.cos(bands * ang), -jnp.sin(bands * ang)], axis=-1)
    hp = lax.Precision.HIGHEST
    h = jnp.sin(freq * (jnp.dot(feats, w1, precision=hp) + b1))
    h = jnp.sin(freq * (jnp.dot(h, w2, precision=hp) + b2))
    h = jnp.sin(freq * (jnp.dot(h, w3, precision=hp) + b3))
    h = jnp.dot(h, w4, precision=hp)
    deltas = jnp.abs(jnp.linspace(math.log(FILTER_TARGET) / SLOW_DECAY_PCT,
                                  math.log(FILTER_TARGET) / FAST_DECAY_PCT, hy, dtype=F32))
    h = h * jnp.exp(-t * jnp.tile(deltas, 2))
    return h[:, :hy], h[:, hy:]


def _dft_matrices(length):
    n = 2 * length
    f = lax.broadcasted_iota(jnp.int32, (length, length), 0)
    t = lax.broadcasted_iota(jnp.int32, (length, length), 1)
    ang = ((f * t) % n).astype(F32) * (2.0 * math.pi / n)
    cos, sin = jnp.cos(ang), jnp.sin(ang)
    nyq = jnp.where(t % 2 == 0, 1.0, -1.0)
    fwd = jnp.concatenate([cos, jnp.where(f == 0, nyq, -sin)], axis=0)
    cos_t, sin_t = cos.T, sin.T
    f_t = f.T
    inv = jnp.concatenate([jnp.where(f_t == 0, 1.0, 2.0 * cos_t),
                           jnp.where(f_t == 0, nyq.T, -2.0 * sin_t)], axis=1)
    return fwd.astype(BF16), inv.astype(BF16)


def _long_conv(u, h_fwd, h_bwd):
    nb, length, ch = u.shape
    fwd, inv = _dft_matrices(length)
    h_bwd0 = h_bwd.at[0].set(0.0)
    sig = jnp.concatenate([u.astype(BF16), h_fwd[None].astype(BF16), h_bwd0[None].astype(BF16)], axis=0)
    spec = _bmm_shared_lhs(fwd, sig, F32, 1024, 1024)
    re, im = spec[:, :length], spec[:, length:]
    hf_re, hf_im, hb_re, hb_im = re[nb], im[nb], re[nb + 1], im[nb + 1]
    scale = 1.0 / (2 * length)
    t_re = (hf_re + hb_re) * scale
    t_im = (hf_im - hb_im) * scale
    t_im0 = (hf_im + hb_im) * scale
    row0 = (lax.broadcasted_iota(jnp.int32, (length, 1), 0) == 0)
    u_re, u_im = re[:nb], im[:nb]
    y_re = u_re * t_re - jnp.where(row0, 0.0, u_im * t_im)
    y_im = jnp.where(row0, u_im * t_im0, u_re * t_im + u_im * t_re)
    y_spec = jnp.concatenate([y_re, y_im], axis=1).astype(BF16)
    return _bmm_shared_lhs(inv, y_spec, F32, 1024, 1024)


def _group_rmsnorm(y, g, group):
    shp = y.shape
    yg = y.reshape(*shp[:-1], shp[-1] // group, group)
    yg = yg * lax.rsqrt(jnp.mean(yg * yg, axis=-1, keepdims=True) + NORM_EPS)
    return yg.reshape(shp) * g


def _moe(h2, router_w, router_b, wg, bg, wu, bu, wd, bdn):
    n_tok, d = h2.shape
    e = router_w.shape[1]
    tm = MOE_TILE
    logits = jnp.dot(h2, router_w, precision=lax.Precision.HIGHEST) + router_b
    top_logits, top_idx = lax.top_k(logits, TOP_K)
    gates = jax.nn.softmax(top_logits, axis=-1)
    n_asg = n_tok * TOP_K
    e_flat = top_idx.reshape(n_asg)
    onehot = (e_flat[:, None] == jnp.arange(e, dtype=jnp.int32)[None, :]).astype(jnp.int32)
    csum = jnp.cumsum(onehot, axis=0)
    rank = jnp.sum(onehot * (csum - 1), axis=1)
    counts = csum[-1]
    padded = (counts + tm - 1) // tm * tm
    pad_end = jnp.cumsum(padded)
    pad_start = pad_end - padded
    dest = pad_start[e_flat] + rank
    n_tiles = -(-(n_asg + e * (tm - 1)) // tm)
    n_rows = n_tiles * tm
    tok = (jnp.arange(n_asg, dtype=jnp.int32) // TOP_K)
    row_tok = jnp.zeros((n_rows,), jnp.int32).at[dest].set(tok)
    tile_expert = jnp.minimum(
        jnp.searchsorted(pad_end, jnp.arange(n_tiles, dtype=jnp.int32) * tm, side='right'), e - 1).astype(jnp.int32)
    n_used = (pad_end[-1] // tm).astype(jnp.int32).reshape(1)

    xs = h2.astype(BF16)[row_tok]
    hs = _moe_up(tile_expert, n_used, xs, wg, wu, bg, bu, 512)
    ys = _moe_down(tile_expert, n_used, hs, wd, bdn, 1024)
    picked = ys[dest.reshape(n_tok, TOP_K)].astype(F32)
    return jnp.sum(picked * gates[:, :, None], axis=1)


def kernel(x, c, ctx, c_ctx, ada_w, ada_b, norm1_g, norm2_g, in_w, conv_w, conv_b, hy_w1, hy_b1, hy_w2, hy_b2, hy_w3, hy_b3, hy_freq, hy_w4, hy_bias, hy_norm_g, rw_w0, rw_w1, rw_w2, rw_a0, rw_a1, rw_a2, rw_kk, rw_ka, rw_rk, rw_g1, rw_g2, rw_lnx_g, rw_lnx_b, out_w, router_w, router_b, ex_w_gate, ex_b_gate, ex_w_up, ex_b_up, ex_w_down, ex_b_down, final_g):
    depth = ada_w.shape[0]
    assert depth == 1, "single-layer block: context outputs never reach a latent token"
    nb, seq, d = x.shape
    n_ctx = ctx.shape[1]
    lt = n_ctx + seq
    hy = hy_bias.shape[1]
    rw = rw_kk.shape[1]
    n_heads = rw // HEAD
    n_hy = 3 * hy
    lora_d = rw_w1.shape[-1]
    lora_a = rw_a1.shape[-1]
    lora_g = rw_g1.shape[-1]

    cond = jnp.concatenate([jax.nn.silu(c), jax.nn.silu(c_ctx)[None]], axis=0)
    mod = _matmul(cond, ada_w[0], F32, cond.shape[0], 1024) + ada_b[0]
    mod_x = [m[:, None, :] for m in jnp.split(mod[:nb], 6, axis=-1)]
    mod_c = [m[:, None, :] for m in jnp.split(mod[nb:], 6, axis=-1)]

    hx = _rmsnorm(x, norm1_g[0]) * (1 + mod_x[1]) + mod_x[0]
    hc = _rmsnorm(ctx, norm1_g[0]) * (1 + mod_c[1]) + mod_c[0]
    h_all = jnp.concatenate([hc, hx], axis=1)
    h_flat = h_all.astype(BF16).reshape(nb * lt, d)

    z = _matmul(h_flat, in_w[0].astype(BF16), BF16, 512, 1024).reshape(nb, lt, -1)
    lora_w = jnp.concatenate([rw_w1[0, 0], rw_w1[0, 1], rw_a1[0, 0], rw_a1[0, 1], rw_g1[0]], axis=1)
    lora = _matmul(h_flat, lora_w.astype(BF16), F32, 512, lora_w.shape[1]).reshape(nb, lt, -1)

    z = z.astype(F32)
    zc = _short_conv(z[:, :n_ctx], conv_w[0], conv_b[0])
    zx = _short_conv(z[:, n_ctx:], conv_w[0], conv_b[0])

    x0, x1, vh = jnp.split(zx[..., :n_hy], 3, axis=-1)
    h_fwd, h_bwd = _hyena_filter(seq, hy_w1[0], hy_b1[0], hy_w2[0], hy_b2[0], hy_w3[0], hy_b3[0],
                                 hy_freq[0], hy_w4[0])
    u = vh * x1
    y_hy = x0 * (_long_conv(u, h_fwd, h_bwd) + hy_bias[0] * u)
    y_hy = _group_rmsnorm(y_hy, hy_norm_g[0], HEAD)

    z_rw = jnp.concatenate([zc[..., n_hy:], zx[..., n_hy:]], axis=1)
    r, k, v = jnp.split(z_rw, 3, axis=-1)
    kk = (k * rw_kk[0]).reshape(nb, lt, n_heads, HEAD)
    kk = (kk / jnp.maximum(jnp.sqrt(jnp.sum(kk * kk, axis=-1, keepdims=True)), 1e-12)).reshape(nb, lt, rw)
    o = 0
    lw_list, a_list = [], []
    tw = jnp.tanh(lora[..., :2 * lora_d])
    for n in range(2):
        wl = _matmul(tw[..., n * lora_d:(n + 1) * lora_d].reshape(nb * lt, lora_d), rw_w2[0, n], F32, 512, 1024)
        w_log = -jax.nn.softplus(-(rw_w0[0, n] + wl.reshape(nb, lt, rw))) - 0.5
        lw_list.append(-jnp.exp(w_log))
    o = 2 * lora_d
    for n in range(2):
        al = _matmul(lora[..., o + n * lora_a:o + (n + 1) * lora_a].reshape(nb * lt, lora_a), rw_a2[0, n],
                     F32, 512, 1024)
        a_list.append(jax.nn.sigmoid(rw_a0[0, n] + al.reshape(nb, lt, rw)))
    o += 2 * lora_a
    lw = jnp.stack(lw_list)
    a_sig = jnp.stack(a_list)
    k_dir = k * (1 + (a_sig - 1) * rw_ka[0])
    bb = kk * a_sig
    ys = _rwkv_scan(r, v, -kk, lw, k_dir, bb, n_ctx)[:, :, n_ctx:]

    rx = r[:, n_ctx:]
    vx = v[:, n_ctx:]
    rk = (rx * k_dir[:, :, n_ctx:]).reshape(2, nb, seq, n_heads, HEAD) * rw_rk[0]
    bonus = jnp.sum(jnp.sum(rk, axis=-1, keepdims=True) * vx.reshape(nb, seq, n_heads, HEAD), axis=0)
    bonus = bonus.reshape(nb, seq, rw)
    y = (ys[0] + ys[1]).reshape(nb, seq, n_heads, HEAD)
    mu = jnp.mean(y, axis=-1, keepdims=True)
    var = jnp.mean(jnp.square(y - mu), axis=-1, keepdims=True)
    y = ((y - mu) * lax.rsqrt(var + LNX_EPS)).reshape(nb, seq, rw) * rw_lnx_g[0] + rw_lnx_b[0]
    gl = jax.nn.sigmoid(lora[:, n_ctx:, o:o + lora_g]).reshape(nb * seq, lora_g)
    gate = _matmul(gl, rw_g2[0], F32, 512, 1024).reshape(nb, seq, rw)
    y_rw = (y + bonus) * gate

    mix_in = jnp.concatenate([y_hy, y_rw], axis=-1).astype(BF16).reshape(nb * seq, hy + rw)
    mix = _matmul(mix_in, out_w[0].astype(BF16), F32, 512, 1024).reshape(nb, seq, d)
    x1r = x + mod_x[2] * mix

    h2 = (_rmsnorm(x1r, norm2_g[0]) * (1 + mod_x[4]) + mod_x[3]).reshape(nb * seq, d)
    moe = _moe(h2, router_w[0], router_b[0], ex_w_gate[0], ex_b_gate[0], ex_w_up[0], ex_b_up[0],
               ex_w_down[0], ex_b_down[0]).reshape(nb, seq, d)
    xo = x1r + mod_x[5] * moe
    return _rmsnorm(xo, final_g)
```

```python
import functools
import math

import jax
import jax.numpy as jnp
from jax import lax
from jax.experimental import pallas as pl
from jax.experimental.pallas import tpu as pltpu

F32 = jnp.float32
BF16 = jnp.bfloat16

HEAD = 64
CHUNK = 64
PACK = 4
PACK_W = PACK * HEAD
HALO = 16
FILTER_BANDS = 16
FILTER_TARGET = 1e-2
FAST_DECAY_PCT = 0.3
SLOW_DECAY_PCT = 1.5
TOP_K = 4
SWIGLU_LIMIT = 7.0
SWIGLU_ALPHA = 1.702
NORM_EPS = 1e-6
LNX_EPS = 64e-5
MOE_TILE = 512
LANE = 128
VMEM_LIMIT = 56 * 1024 * 1024


def _cparams(sem):
    return pltpu.CompilerParams(dimension_semantics=sem, vmem_limit_bytes=VMEM_LIMIT)


def _tile(n, want):
    if n <= want:
        return n
    t = want
    while n % t:
        t //= 2
    assert t >= 8, (n, want)
    return t


def _dot(a, b):
    return jnp.dot(a.astype(BF16), b.astype(BF16), preferred_element_type=F32)


def _group_ones(width):
    r = lax.broadcasted_iota(jnp.int32, (width, width), 0) // HEAD
    c = lax.broadcasted_iota(jnp.int32, (width, width), 1) // HEAD
    return jnp.where(r == c, 1.0, 0.0).astype(BF16)


def _group_sum(x, ones):
    width = ones.shape[0]
    parts = [_dot(x[:, o:o + width], ones) for o in range(0, x.shape[1], width)]
    return parts[0] if len(parts) == 1 else jnp.concatenate(parts, axis=1)


def _mm_kernel(a_ref, b_ref, o_ref):
    o_ref[...] = _dot(a_ref[...], b_ref[...]).astype(o_ref.dtype)


def _matmul(a, b, out_dtype, tm, tn):
    m, k = a.shape
    n = b.shape[1]
    tm, tn = _tile(m, tm), _tile(n, tn)
    return pl.pallas_call(
        _mm_kernel,
        grid=(n // tn, m // tm),
        in_specs=[pl.BlockSpec((tm, k), lambda j, i: (i, 0)),
                  pl.BlockSpec((k, tn), lambda j, i: (0, j))],
        out_specs=pl.BlockSpec((tm, tn), lambda j, i: (i, j)),
        out_shape=jax.ShapeDtypeStruct((m, n), out_dtype),
        compiler_params=_cparams(("parallel", "parallel")),
    )(a, b)


def _bmm_shared_lhs(a, b, out_dtype, tm, tn):
    m, k = a.shape
    nb, _, n = b.shape
    tm, tn = _tile(m, tm), _tile(n, tn)
    return pl.pallas_call(
        _mm_kernel,
        grid=(m // tm, nb, n // tn),
        in_specs=[pl.BlockSpec((tm, k), lambda i, bb, j: (i, 0)),
                  pl.BlockSpec((None, k, tn), lambda i, bb, j: (bb, 0, j))],
        out_specs=pl.BlockSpec((None, tm, tn), lambda i, bb, j: (bb, i, j)),
        out_shape=jax.ShapeDtypeStruct((nb, m, n), out_dtype),
        compiler_params=_cparams(("parallel", "parallel", "parallel")),
    )(a, b)


def _inproj_kernel(seq, xm_ref, xp_ref, xn_ref, shift_ref, scale_ref, g_ref, w_ref, cw_ref, cb_ref,
                   z_ref, h_ref):
    tm = xm_ref.shape[0]
    i = pl.program_id(0)

    @pl.when(pl.program_id(1) == 0)
    def _():
        def norm(xv):
            y = xv * lax.rsqrt(jnp.mean(xv * xv, axis=-1, keepdims=True) + NORM_EPS) * g_ref[...]
            return (y * (1.0 + scale_ref[...]) + shift_ref[...]).astype(BF16)
        h_ref[0:HALO] = norm(xp_ref[...])
        h_ref[HALO:HALO + tm] = norm(xm_ref[...])
        h_ref[HALO + tm:] = norm(xn_ref[...])

    zz = jnp.dot(h_ref[...], w_ref[...], preferred_element_type=F32)
    rows = tm + 2 * HALO
    z_prev = pltpu.roll(zz, 1, 0)[HALO:HALO + tm]
    z_next = pltpu.roll(zz, rows - 1, 0)[HALO:HALO + tm]
    z_mid = zz[HALO:HALO + tm]
    pos = (i * tm + lax.broadcasted_iota(jnp.int32, (tm, 1), 0)) % seq
    z_prev = jnp.where(pos == 0, 0.0, z_prev)
    z_next = jnp.where(pos == seq - 1, 0.0, z_next)
    cw = cw_ref[...]
    z_ref[...] = (z_prev * cw[0:1] + z_mid * cw[1:2] + z_next * cw[2:3] + cb_ref[...]).astype(z_ref.dtype)


def _inproj(x, shift, scale, g, w, cw, cb, tm, tn):
    nb, seq, d = x.shape
    n = w.shape[1]
    tm, tn = _tile(seq, tm), _tile(n, tn)
    m = nb * seq
    n_halo = m // HALO
    per_batch = shift.shape[0] > 1
    mod_spec = pl.BlockSpec((None, 1, d), (lambda i, j: ((i * tm) // seq, 0, 0)) if per_batch
                            else (lambda i, j: (0, 0, 0)))
    x2 = x.reshape(m, d)
    out = pl.pallas_call(
        functools.partial(_inproj_kernel, seq),
        grid=(m // tm, n // tn),
        in_specs=[pl.BlockSpec((tm, d), lambda i, j: (i, 0)),
                  pl.BlockSpec((HALO, d), lambda i, j: (jnp.maximum(i * (tm // HALO) - 1, 0), 0)),
                  pl.BlockSpec((HALO, d), lambda i, j: (jnp.minimum((i + 1) * (tm // HALO), n_halo - 1), 0)),
                  mod_spec, mod_spec,
                  pl.BlockSpec((1, d), lambda i, j: (0, 0)),
                  pl.BlockSpec((d, tn), lambda i, j: (0, j)),
                  pl.BlockSpec((3, tn), lambda i, j: (0, j)),
                  pl.BlockSpec((1, tn), lambda i, j: (0, j))],
        out_specs=pl.BlockSpec((tm, tn), lambda i, j: (i, j)),
        out_shape=jax.ShapeDtypeStruct((m, n), BF16),
        scratch_shapes=[pltpu.VMEM((tm + 2 * HALO, d), BF16)],
        compiler_params=_cparams(("parallel", "arbitrary")),
    )(x2, x2, x2, shift, scale, g, w, cw, cb)
    return out.reshape(nb, seq, n)


def _dft_matrices(length):
    n = 2 * length
    f = lax.broadcasted_iota(jnp.int32, (length, length), 0)
    t = lax.broadcasted_iota(jnp.int32, (length, length), 1)
    ang = ((f * t) % n).astype(F32) * (2.0 * math.pi / n)
    cos, sin = jnp.cos(ang), jnp.sin(ang)
    nyq = jnp.where(t % 2 == 0, 1.0, -1.0)
    fwd = jnp.concatenate([cos, jnp.where(f == 0, nyq, -sin)], axis=0)
    f_t = f.T
    inv = jnp.concatenate([jnp.where(f_t == 0, 1.0, 2.0 * cos.T),
                           jnp.where(f_t == 0, nyq.T, -2.0 * sin.T)], axis=1)
    return fwd.astype(BF16), inv.astype(BF16)


def _dft_fwd_kernel(f_ref, x1_ref, v_ref, o_ref, u_ref):
    @pl.when(pl.program_id(2) == 0)
    def _():
        u_ref[...] = (x1_ref[...].astype(F32) * v_ref[...].astype(F32)).astype(BF16)

    o_ref[...] = jnp.dot(f_ref[...], u_ref[...], preferred_element_type=F32).astype(o_ref.dtype)


def _dft_fwd(fwd, z, hy, tm, tn):
    nb, seq, _ = z.shape
    tm, tn = _tile(2 * seq, tm), _tile(hy, tn)
    nj = hy // tn
    return pl.pallas_call(
        _dft_fwd_kernel,
        grid=(nb, nj, 2 * seq // tm),
        in_specs=[pl.BlockSpec((tm, seq), lambda b, j, i: (i, 0)),
                  pl.BlockSpec((None, seq, tn), lambda b, j, i: (b, 0, nj + j)),
                  pl.BlockSpec((None, seq, tn), lambda b, j, i: (b, 0, 2 * nj + j))],
        out_specs=pl.BlockSpec((None, tm, tn), lambda b, j, i: (b, i, j)),
        out_shape=jax.ShapeDtypeStruct((nb, 2 * seq, hy), BF16),
        scratch_shapes=[pltpu.VMEM((seq, tn), BF16)],
        compiler_params=_cparams(("parallel", "parallel", "arbitrary")),
    )(fwd, z, z)


def _dft_inv_kernel(g_ref, s_ref, t_ref, x0_ref, x1_ref, v_ref, bias_ref, ng_ref, o_ref, y_ref):
    half = s_ref.shape[0] // 2

    @pl.when(pl.program_id(2) == 0)
    def _():
        re = s_ref[:half].astype(F32)
        im = s_ref[half:].astype(F32)
        t_re = t_ref[:half].astype(F32)
        t_im = t_ref[half:].astype(F32)
        row0 = lax.broadcasted_iota(jnp.int32, (half, 1), 0) == 0
        y_ref[:half] = (re * t_re - jnp.where(row0, 0.0, im * t_im)).astype(BF16)
        y_ref[half:] = (im * jnp.where(row0, t_im, t_re) + jnp.where(row0, 0.0, re * t_im)).astype(BF16)

    conv = jnp.dot(g_ref[...], y_ref[...], preferred_element_type=F32)
    u = x1_ref[...].astype(F32) * v_ref[...].astype(F32)
    y = x0_ref[...].astype(F32) * (conv + bias_ref[...] * u)
    ms = _group_sum(y * y, _group_ones(min(y.shape[1], PACK_W))) * (1.0 / HEAD)
    o_ref[...] = (y * lax.rsqrt(ms + NORM_EPS) * ng_ref[...]).astype(o_ref.dtype)


def _dft_inv(inv, spec, taps, z, bias, norm_g, tm, tn):
    nb, seq, _ = z.shape
    hy = spec.shape[2]
    tm, tn = _tile(seq, tm), _tile(hy, tn)
    nj = hy // tn
    row = lambda k: pl.BlockSpec((None, tm, tn), lambda b, j, i: (b, i, k * nj + j))
    vec = pl.BlockSpec((1, tn), lambda b, j, i: (0, j))
    return pl.pallas_call(
        _dft_inv_kernel,
        grid=(nb, nj, seq // tm),
        in_specs=[pl.BlockSpec((tm, 2 * seq), lambda b, j, i: (i, 0)),
                  pl.BlockSpec((None, 2 * seq, tn), lambda b, j, i: (b, 0, j)),
                  pl.BlockSpec((2 * seq, tn), lambda b, j, i: (0, j)),
                  row(0), row(1), row(2), vec, vec],
        out_specs=pl.BlockSpec((None, tm, tn), lambda b, j, i: (b, i, j)),
        out_shape=jax.ShapeDtypeStruct((nb, seq, hy), BF16),
        scratch_shapes=[pltpu.VMEM((2 * seq, tn), BF16)],
        compiler_params=_cparams(("parallel", "parallel", "arbitrary")),
    )(inv, spec, taps, z, z, z, bias, norm_g)


def _hyena_filter(length, w1, b1, w2, b2, w3, b3, freq, w4):
    hy = w4.shape[1] // 2
    t = jnp.linspace(0.0, 1.0, length, dtype=F32)[:, None]
    ang = (2.0 * math.pi / length) * jnp.arange(length, dtype=F32)[:, None]
    bands = jnp.linspace(1e-4, FILTER_BANDS - 1, FILTER_BANDS, dtype=F32)[None, :]
    feats = jnp.concatenate([t, jnp.cos(bands * ang), -jnp.sin(bands * ang)], axis=-1)
    hp = lax.Precision.HIGHEST
    h = jnp.sin(freq * (jnp.dot(feats, w1, precision=hp) + b1))
    h = jnp.sin(freq * (jnp.dot(h, w2, precision=hp) + b2))
    h = jnp.sin(freq * (jnp.dot(h, w3, precision=hp) + b3))
    h = jnp.dot(h, w4, precision=hp)
    deltas = jnp.abs(jnp.linspace(math.log(FILTER_TARGET) / SLOW_DECAY_PCT,
                                  math.log(FILTER_TARGET) / FAST_DECAY_PCT, hy, dtype=F32))
    h = h * jnp.exp(-t * jnp.tile(deltas, 2))
    return h[:, :hy], h[:, hy:]


def _taps_spectrum(fwd, h_fwd, h_bwd):
    length = h_fwd.shape[0]
    sig = jnp.stack([h_fwd, h_bwd.at[0].set(0.0)]).astype(BF16)
    spec = _bmm_shared_lhs(fwd, sig, F32, 1024, 1024)
    re, im = spec[:, :length], spec[:, length:]
    scale = 1.0 / (2 * length)
    row0 = lax.broadcasted_iota(jnp.int32, (length, 1), 0) == 0
    t_re = (re[0] + re[1]) * scale
    t_im = jnp.where(row0, im[0] + im[1], im[0] - im[1]) * scale
    return jnp.concatenate([t_re, t_im], axis=0).astype(BF16)


def _scan_kernel(r_ref, k_ref, v_ref, lw_ref, la_ref, w2_ref, a2_ref, w0_ref, a0_ref, kks_ref, ka_ref, rk_ref,
                 s0_ref, y_ref, bonus_ref, sfin_ref, ht_ref):
    c = CHUNK
    w = PACK_W
    direction = pl.program_id(0)
    step = pl.program_id(2)
    n_groups = r_ref.shape[1] // w
    lora = w2_ref.shape[0]

    @pl.when(step == 0)
    def _():
        ht_ref[...] = s0_ref[...]

    sign = 1 - 2 * direction
    d_sq = (lax.broadcasted_iota(jnp.int32, (c, c), 1) - lax.broadcasted_iota(jnp.int32, (c, c), 0)) * sign
    incl_sq = jnp.where(d_sq <= 0, 1.0, 0.0).astype(BF16)
    d_c = (lax.broadcasted_iota(jnp.int32, (c, w), 1) % c - lax.broadcasted_iota(jnp.int32, (c, w), 0)) * sign
    strict_c = d_c < 0
    incl_c = d_c <= 0
    eye_c = jnp.where(d_c == 0, 1.0, 0.0)
    bd_mask = (lax.broadcasted_iota(jnp.int32, (w, w), 0) // c) == (lax.broadcasted_iota(jnp.int32, (w, w), 1) // HEAD)
    ones_bd = jnp.where(bd_mask, 1.0, 0.0).astype(BF16)

    def bd(x):
        xb = x.astype(BF16)
        return jnp.where(bd_mask, jnp.concatenate([xb] * PACK, axis=0), jnp.zeros((), BF16))

    def mm_nt(a, b):
        return lax.dot_general(a.astype(BF16), b.astype(BF16), (((1,), (1,)), ((), ())),
                               preferred_element_type=F32)

    r_all = r_ref[...].astype(F32)
    k_all = k_ref[...].astype(F32)
    v_all = v_ref[...].astype(F32)
    lw_pre = lw_ref[...].astype(F32)
    la_pre = la_ref[...].astype(F32)
    fwd = direction == 0
    lw_pre = jnp.where(fwd, lw_pre[:, :lora], lw_pre[:, lora:])
    la_pre = jnp.where(fwd, la_pre[:, :lora], la_pre[:, lora:])
    x_w = w0_ref[...] + _dot(jnp.tanh(lw_pre), w2_ref[...])
    lw_all = -math.exp(-0.5) * jax.nn.sigmoid(x_w)
    a_sig = jax.nn.sigmoid(a0_ref[...] + _dot(la_pre, a2_ref[...]))
    kk = k_all * kks_ref[...]
    kk = kk * lax.rsqrt(jnp.maximum(_group_sum(kk * kk, ones_bd), 1e-24))
    kd_all = k_all * (1.0 + (a_sig - 1.0) * ka_ref[...])
    bb_all = kk * a_sig
    bonus_ref[...] = (_group_sum(r_all * kd_all * rk_ref[...], ones_bd) * v_all).astype(bonus_ref.dtype)

    for g in range(n_groups):
        sl = slice(g * w, (g + 1) * w)
        r, v, an, lw, kd, bb = r_all[:, sl], v_all[:, sl], -kk[:, sl], lw_all[:, sl], kd_all[:, sl], bb_all[:, sl]

        lw_hi = lw.astype(BF16)
        lw_lo = (lw - lw_hi.astype(F32)).astype(BF16)
        cum = (jnp.dot(incl_sq, lw_hi, preferred_element_type=F32)
               + jnp.dot(incl_sq, lw_lo, preferred_element_type=F32))
        tot = jnp.sum(lw, axis=0, keepdims=True)
        g_inv = jnp.exp(-cum)
        g_rem = jnp.exp(tot - cum)
        a_t = an * jnp.exp(cum - lw)
        r_t = r * jnp.exp(cum)
        b_t = bb * g_inv
        k_t = kd * g_inv
        b_s = bb * g_rem
        k_s = kd * g_rem

        ht = ht_ref[g]
        lhs = jnp.concatenate([a_t, r_t], axis=0)
        rhs = jnp.concatenate([bd(b_t), bd(k_t)], axis=0)
        scores = mm_nt(lhs, rhs)
        a_ab = jnp.where(strict_c, scores[:c, :w], 0.0)
        a_ak = jnp.where(strict_c, scores[:c, w:], 0.0)
        a_rb = jnp.where(incl_c, scores[c:, :w], 0.0)
        a_rk = jnp.where(incl_c, scores[c:, w:], 0.0)
        h0 = mm_nt(lhs, ht)
        bd_v = bd(v)
        x = h0[:c] + _dot(a_ak, bd_v)

        t_inv = eye_c + a_ab
        p = _dot(a_ab, bd(a_ab))
        n_sq = int(math.log2(c)) - 1
        for it in range(n_sq):
            bd_p = bd(p)
            if it + 1 < n_sq:
                pt = _dot(jnp.concatenate([p, t_inv], axis=0), bd_p)
                p = pt[:c]
                t_inv = t_inv + pt[c:]
            else:
                t_inv = t_inv + _dot(t_inv, bd_p)
        u = _dot(t_inv, bd(x))

        y = h0[c:] + _dot(jnp.concatenate([a_rb, a_rk], axis=1), jnp.concatenate([bd(u), bd_v], axis=0))
        y_ref[:, sl] = y.astype(y_ref.dtype)

        uv = jnp.concatenate([u, v], axis=0).astype(BF16)
        bk = jnp.concatenate([b_s, k_s], axis=0).astype(BF16)
        upd = lax.dot_general(uv, bk, (((0,), (0,)), ((), ())), preferred_element_type=F32)
        ht_ref[g] = ht * jnp.exp(tot) + jnp.where(bd_mask, upd, 0.0)

    @pl.when(step == pl.num_programs(2) - 1)
    def _():
        sfin_ref[...] = ht_ref[...]


def _rwkv_scan(z, col0, lora_col0, w2, a2, w0, a0, kks, ka, rk, s0):
    nb, seq, _ = z.shape
    lora, ch = w2.shape[1:]
    c = CHUNK
    assert CHUNK == HEAD and ch % PACK_W == 0 and seq % c == 0 and 2 * lora == LANE
    assert col0 % ch == 0 and lora_col0 % LANE == 0
    nc = seq // c
    ng = ch // PACK_W
    chunk_of = lambda n, i: jnp.where(n == 0, i, nc - 1 - i)
    zcol = lambda k: pl.BlockSpec((None, c, ch), lambda n, b, i: (b, chunk_of(n, i), col0 // ch + k))
    zlora = lambda k: pl.BlockSpec((None, c, LANE), lambda n, b, i: (b, chunk_of(n, i), lora_col0 // LANE + k))
    per_dir_w = pl.BlockSpec((None, lora, ch), lambda n, b, i: (n, 0, 0))
    per_dir_v = pl.BlockSpec((None, 1, ch), lambda n, b, i: (n, 0, 0))
    shared_v = pl.BlockSpec((1, ch), lambda n, b, i: (0, 0))
    state = pl.BlockSpec((None, None, ng, PACK_W, PACK_W), lambda n, b, i: (n, b, 0, 0, 0))
    out = pl.BlockSpec((None, None, c, ch), lambda n, b, i: (n, b, chunk_of(n, i), 0))
    return pl.pallas_call(
        _scan_kernel,
        grid=(2, nb, nc),
        in_specs=[zcol(0), zcol(1), zcol(2), zlora(0), zlora(1), per_dir_w, per_dir_w, per_dir_v, per_dir_v,
                  shared_v, shared_v, shared_v, state],
        out_specs=[out, out, state],
        out_shape=[jax.ShapeDtypeStruct((2, nb, seq, ch), BF16), jax.ShapeDtypeStruct((2, nb, seq, ch), BF16),
                   jax.ShapeDtypeStruct((2, nb, ng, PACK_W, PACK_W), F32)],
        scratch_shapes=[pltpu.VMEM((ng, PACK_W, PACK_W), F32)],
        compiler_params=_cparams(("parallel", "parallel", "arbitrary")),
    )(z, z, z, z, z, w2, a2, w0.reshape(2, 1, ch), a0.reshape(2, 1, ch), kks, ka, rk, s0)


def _outproj_kernel(yh_ref, ys0_ref, ys1_ref, b0_ref, b1_ref, lg_ref, g2_ref, lng_ref, lnb_ref, ow_ref,
                    x_ref, gate_ref, n2g_ref, shift_ref, scale_ref, rwh_ref, rwl_ref, rb_ref,
                    x1_ref, h2_ref, lg_out_ref, mix_ref):
    hy = yh_ref.shape[1]
    ones = _group_ones(PACK_W)
    y = ys0_ref[...].astype(F32) + ys1_ref[...].astype(F32)
    mu = _group_sum(y, ones) * (1.0 / HEAD)
    yc = y - mu
    var = _group_sum(yc * yc, ones) * (1.0 / HEAD)
    y = yc * lax.rsqrt(var + LNX_EPS) * lng_ref[...] + lnb_ref[...]
    gate = _dot(jax.nn.sigmoid(lg_ref[...].astype(F32)), g2_ref[...])
    y = (y + b0_ref[...].astype(F32) + b1_ref[...].astype(F32)) * gate
    mix_ref[:, :hy] = yh_ref[...]
    mix_ref[:, hy:] = y.astype(BF16)
    mix = jnp.dot(mix_ref[...], ow_ref[...], preferred_element_type=F32)
    x1 = x_ref[...] + gate_ref[...] * mix
    x1_ref[...] = x1
    h2 = x1 * lax.rsqrt(jnp.mean(x1 * x1, axis=-1, keepdims=True) + NORM_EPS) * n2g_ref[...]
    h2 = h2 * (1.0 + scale_ref[...]) + shift_ref[...]
    h2_ref[...] = h2.astype(BF16)
    h_hi = h2.astype(BF16)
    h_lo = (h2 - h_hi.astype(F32)).astype(BF16)
    logits = (jnp.dot(h_hi, rwh_ref[...], preferred_element_type=F32)
              + jnp.dot(h_hi, rwl_ref[...], preferred_element_type=F32)
              + jnp.dot(h_lo, rwh_ref[...], preferred_element_type=F32))
    lg_out_ref[...] = logits + rb_ref[...]


def _outproj(y_hy, ys, bonus, z, lora_g_col, g2, lnx_g, lnx_b, out_w, x, gate, n2g, shift, scale,
             rw_hi, rw_lo, rb, tm):
    nb, seq, d = x.shape
    m = nb * seq
    hy = y_hy.shape[2]
    rw = ys.shape[3]
    tm = _tile(seq, tm)
    lg = g2.shape[0]
    ne = rw_hi.shape[1]
    assert lora_g_col % lg == 0
    ys2 = ys.reshape(2, m, rw)
    bn2 = bonus.reshape(2, m, rw)
    nz = z.shape[2]
    full = lambda r, c: pl.BlockSpec((r, c), lambda i: (0, 0))
    dir_spec = lambda n: pl.BlockSpec((None, tm, rw), lambda i: (n, i, 0))
    mod_spec = pl.BlockSpec((None, 1, d), lambda i: ((i * tm) // seq, 0, 0))
    return pl.pallas_call(
        _outproj_kernel,
        grid=(m // tm,),
        in_specs=[pl.BlockSpec((tm, hy), lambda i: (i, 0)), dir_spec(0), dir_spec(1), dir_spec(0), dir_spec(1),
                  pl.BlockSpec((tm, lg), lambda i: (i, lora_g_col // lg)),
                  full(lg, rw), full(1, rw), full(1, rw), full(hy + rw, d),
                  pl.BlockSpec((tm, d), lambda i: (i, 0)), mod_spec, full(1, d), mod_spec, mod_spec,
                  full(d, ne), full(d, ne), full(1, ne)],
        out_specs=[pl.BlockSpec((tm, d), lambda i: (i, 0)), pl.BlockSpec((tm, d), lambda i: (i, 0)),
                   pl.BlockSpec((tm, ne), lambda i: (i, 0))],
        out_shape=[jax.ShapeDtypeStruct((m, d), F32), jax.ShapeDtypeStruct((m, d), BF16),
                   jax.ShapeDtypeStruct((m, ne), F32)],
        scratch_shapes=[pltpu.VMEM((tm, hy + rw), BF16)],
        compiler_params=_cparams(("parallel",)),
    )(y_hy.reshape(m, hy), ys2, ys2, bn2, bn2, z.reshape(m, nz), g2, lnx_g, lnx_b, out_w,
      x.reshape(m, d), gate, n2g, shift, scale, rw_hi, rw_lo, rb)


def _expert_changed(te_ref, i):
    prev = te_ref[jnp.maximum(i - 1, 0)]
    return jnp.logical_or(i == 0, te_ref[i] != prev)


def _moe_up_kernel(te_ref, nu_ref, x_ref, wg_ref, wu_ref, bg_ref, bu_ref, h_ref, wg_bf, wu_bf):
    i = pl.program_id(1)

    @pl.when(_expert_changed(te_ref, i))
    def _():
        wg_bf[...] = wg_ref[...].astype(BF16)
        wu_bf[...] = wu_ref[...].astype(BF16)

    @pl.when(i < nu_ref[0])
    def _():
        x = x_ref[...]
        g = jnp.dot(x, wg_bf[...], preferred_element_type=F32) + bg_ref[...]
        u = jnp.dot(x, wu_bf[...], preferred_element_type=F32) + bu_ref[...]
        g = jnp.minimum(g, SWIGLU_LIMIT)
        u = jnp.clip(u, -SWIGLU_LIMIT, SWIGLU_LIMIT)
        h_ref[...] = ((u + 1.0) * (g * jax.nn.sigmoid(SWIGLU_ALPHA * g))).astype(h_ref.dtype)

    @pl.when(i >= nu_ref[0])
    def _():
        h_ref[...] = jnp.zeros_like(h_ref)


def _moe_down_kernel(te_ref, nu_ref, h_ref, wd_ref, bd_ref, y_ref, wd_bf):
    i = pl.program_id(1)

    @pl.when(_expert_changed(te_ref, i))
    def _():
        wd_bf[...] = wd_ref[...].astype(BF16)

    @pl.when(i < nu_ref[0])
    def _():
        y = jnp.dot(h_ref[...], wd_bf[...], preferred_element_type=F32) + bd_ref[...]
        y_ref[...] = y.astype(y_ref.dtype)

    @pl.when(i >= nu_ref[0])
    def _():
        y_ref[...] = jnp.zeros_like(y_ref)


def _moe_up(tile_expert, n_used, xs, wg, wu, bg, bu, tn):
    n_rows, d = xs.shape
    e, _, f = wg.shape
    tm = MOE_TILE
    tn = _tile(f, tn)
    w_spec = pl.BlockSpec((None, d, tn), lambda j, i, te, nu: (te[i], 0, j))
    b_spec = pl.BlockSpec((None, 1, tn), lambda j, i, te, nu: (te[i], 0, j))
    return pl.pallas_call(
        _moe_up_kernel,
        grid_spec=pltpu.PrefetchScalarGridSpec(
            num_scalar_prefetch=2,
            grid=(f // tn, n_rows // tm),
            in_specs=[pl.BlockSpec((tm, d), lambda j, i, te, nu: (i, 0)), w_spec, w_spec, b_spec, b_spec],
            out_specs=pl.BlockSpec((tm, tn), lambda j, i, te, nu: (i, j)),
            scratch_shapes=[pltpu.VMEM((d, tn), BF16), pltpu.VMEM((d, tn), BF16)]),
        out_shape=jax.ShapeDtypeStruct((n_rows, f), BF16),
        compiler_params=_cparams(("arbitrary", "arbitrary")),
    )(tile_expert, n_used, xs, wg, wu, bg.reshape(e, 1, f), bu.reshape(e, 1, f))


def _moe_down(tile_expert, n_used, hs, wd, bdn, tn):
    n_rows, f = hs.shape
    e, _, d = wd.shape
    tm = MOE_TILE
    tn = _tile(d, tn)
    return pl.pallas_call(
        _moe_down_kernel,
        grid_spec=pltpu.PrefetchScalarGridSpec(
            num_scalar_prefetch=2,
            grid=(d // tn, n_rows // tm),
            in_specs=[pl.BlockSpec((tm, f), lambda j, i, te, nu: (i, 0)),
                      pl.BlockSpec((None, f, tn), lambda j, i, te, nu: (te[i], 0, j)),
                      pl.BlockSpec((None, 1, tn), lambda j, i, te, nu: (te[i], 0, j))],
            out_specs=pl.BlockSpec((tm, tn), lambda j, i, te, nu: (i, j)),
            scratch_shapes=[pltpu.VMEM((f, tn), BF16)]),
        out_shape=jax.ShapeDtypeStruct((n_rows, d), BF16),
        compiler_params=_cparams(("arbitrary", "arbitrary")),
    )(tile_expert, n_used, hs, wd, bdn.reshape(e, 1, d))


def _moe(h2, logits, wg, bg, wu, bu, wd, bdn):
    n_tok, d = h2.shape
    e = logits.shape[1]
    tm = MOE_TILE
    top_logits, top_idx = lax.top_k(logits, TOP_K)
    gates = jax.nn.softmax(top_logits, axis=-1)
    n_asg = n_tok * TOP_K
    e_flat = top_idx.reshape(n_asg)
    onehot = (e_flat[:, None] == jnp.arange(e, dtype=jnp.int32)[None, :]).astype(jnp.int32)
    csum = jnp.cumsum(onehot, axis=0)
    rank = jnp.sum(onehot * (csum - 1), axis=1)
    counts = csum[-1]
    padded = (counts + tm - 1) // tm * tm
    pad_end = jnp.cumsum(padded)
    pad_start = pad_end - padded
    dest = pad_start[e_flat] + rank
    n_tiles = -(-(n_asg + e * (tm - 1)) // tm)
    n_rows = n_tiles * tm
    tok = (jnp.arange(n_asg, dtype=jnp.int32) // TOP_K)
    row_tok = jnp.zeros((n_rows,), jnp.int32).at[dest].set(tok)
    tile_expert = jnp.minimum(
        jnp.searchsorted(pad_end, jnp.arange(n_tiles, dtype=jnp.int32) * tm, side='right'), e - 1).astype(jnp.int32)
    n_used = (pad_end[-1] // tm).astype(jnp.int32).reshape(1)

    xs = h2[row_tok]
    hs = _moe_up(tile_expert, n_used, xs, wg, wu, bg, bu, 512)
    ys = _moe_down(tile_expert, n_used, hs, wd, bdn, 1024)
    picked = ys[dest.reshape(n_tok, TOP_K)].astype(F32)
    return jnp.sum(picked * gates[:, :, None], axis=1)


def _rmsnorm(x, g):
    return x * lax.rsqrt(jnp.mean(x * x, axis=-1, keepdims=True) + NORM_EPS) * g


def kernel(x, c, ctx, c_ctx, ada_w, ada_b, norm1_g, norm2_g, in_w, conv_w, conv_b, hy_w1, hy_b1, hy_w2, hy_b2, hy_w3, hy_b3, hy_freq, hy_w4, hy_bias, hy_norm_g, rw_w0, rw_w1, rw_w2, rw_a0, rw_a1, rw_a2, rw_kk, rw_ka, rw_rk, rw_g1, rw_g2, rw_lnx_g, rw_lnx_b, out_w, router_w, router_b, ex_w_gate, ex_b_gate, ex_w_up, ex_b_up, ex_w_down, ex_b_down, final_g):
    assert ada_w.shape[0] == 1, "single-layer block: context outputs never reach a latent token"
    nb, seq, d = x.shape
    hy = hy_bias.shape[1]
    rw = rw_kk.shape[1]
    n_hy = 3 * hy
    proj = in_w.shape[2]
    n_exp = router_w.shape[2]

    cond = jnp.concatenate([jax.nn.silu(c), jax.nn.silu(c_ctx)[None]], axis=0)
    mod = _matmul(cond, ada_w[0], F32, cond.shape[0], 1024) + ada_b[0]
    mod_x = [m[:, None, :] for m in jnp.split(mod[:nb], 6, axis=-1)]
    mod_c = [m[:, None, :] for m in jnp.split(mod[nb:], 6, axis=-1)]

    lora_w = jnp.concatenate([rw_w1[0, 0], rw_w1[0, 1], rw_a1[0, 0], rw_a1[0, 1], rw_g1[0]], axis=1)
    n_lora = lora_w.shape[1]
    nz = -(-(proj + n_lora) // 512) * 512
    pad = nz - proj - n_lora
    w_all = jnp.concatenate([in_w[0], lora_w, jnp.zeros((d, pad), F32)], axis=1).astype(BF16)
    pass_taps = jnp.concatenate([jnp.zeros((1, n_lora + pad), F32), jnp.ones((1, n_lora + pad), F32),
                                 jnp.zeros((1, n_lora + pad), F32)], axis=0)
    cw_all = jnp.concatenate([conv_w[0], pass_taps], axis=1)
    cb_all = jnp.concatenate([conv_b[0], jnp.zeros((n_lora + pad,), F32)])[None]
    g1 = norm1_g[0][None]
    zx = _inproj(x, mod_x[0], mod_x[1], g1, w_all, cw_all, cb_all, 1024, 512)
    zc = _inproj(ctx, mod_c[0], mod_c[1], g1, w_all, cw_all, cb_all, 256, 512)

    h_fwd, h_bwd = _hyena_filter(seq, hy_w1[0], hy_b1[0], hy_w2[0], hy_b2[0], hy_w3[0], hy_b3[0],
                                 hy_freq[0], hy_w4[0])
    fwd, inv = _dft_matrices(seq)
    taps = _taps_spectrum(fwd, h_fwd, h_bwd)
    spec = _dft_fwd(fwd, zx, hy, 1024, 512)
    y_hy = _dft_inv(inv, spec, taps, zx, hy_bias[0][None], hy_norm_g[0][None], 1024, 512)

    scan_args = (n_hy, proj, rw_w2[0].astype(BF16), rw_a2[0].astype(BF16), rw_w0[0], rw_a0[0],
                 rw_kk[0][None], rw_ka[0][None], rw_rk[0].reshape(1, rw))
    s0 = jnp.zeros((2, nb, rw // PACK_W, PACK_W, PACK_W), F32)
    _, _, s_ctx = _rwkv_scan(zc, *scan_args, s0)
    ys, bonus, _ = _rwkv_scan(zx, *scan_args, s_ctx)

    ne = -(-n_exp // LANE) * LANE
    rw_pad = jnp.pad(router_w[0], ((0, 0), (0, ne - n_exp)))
    rw_hi = rw_pad.astype(BF16)
    rw_lo = (rw_pad - rw_hi.astype(F32)).astype(BF16)
    rb = jnp.pad(router_b[0], (0, ne - n_exp))[None]
    x1, h2, logits = _outproj(y_hy, ys, bonus, zx, proj + n_lora - rw_g1.shape[-1], rw_g2[0].astype(BF16),
                              rw_lnx_g[0][None], rw_lnx_b[0][None], out_w[0].astype(BF16), x,
                              mod_x[2], norm2_g[0][None], mod_x[3], mod_x[4], rw_hi, rw_lo, rb, 256)

    moe = _moe(h2, logits[:, :n_exp], ex_w_gate[0], ex_b_gate[0], ex_w_up[0], ex_b_up[0],
               ex_w_down[0], ex_b_down[0]).reshape(nb, seq, d)
    xo = x1.reshape(nb, seq, d) + mod_x[5] * moe
    return _rmsnorm(xo, final_g)
```

```python
import functools
import math

import jax
import jax.numpy as jnp
from jax import lax
from jax.experimental import pallas as pl
from jax.experimental.pallas import tpu as pltpu

F32 = jnp.float32
BF16 = jnp.bfloat16

HEAD = 64
CHUNK = 64
PACK = 4
PACK_W = PACK * HEAD
HALO = 16
FILTER_BANDS = 16
FILTER_TARGET = 1e-2
FAST_DECAY_PCT = 0.3
SLOW_DECAY_PCT = 1.5
TOP_K = 4
SWIGLU_LIMIT = 7.0
SWIGLU_ALPHA = 1.702
NORM_EPS = 1e-6
LNX_EPS = 64e-5
MOE_TILE = 512
LANE = 128
VMEM_LIMIT = 56 * 1024 * 1024


def _cparams(sem):
    return pltpu.CompilerParams(dimension_semantics=sem, vmem_limit_bytes=VMEM_LIMIT)


def _tile(n, want):
    if n <= want:
        return n
    t = want
    while n % t:
        t //= 2
    assert t >= 8, (n, want)
    return t


def _dot(a, b):
    return jnp.dot(a.astype(BF16), b.astype(BF16), preferred_element_type=F32)


def _group_ones(width):
    r = lax.broadcasted_iota(jnp.int32, (width, width), 0) // HEAD
    c = lax.broadcasted_iota(jnp.int32, (width, width), 1) // HEAD
    return jnp.where(r == c, 1.0, 0.0).astype(BF16)


def _group_sum(x, ones):
    width = ones.shape[0]
    parts = [_dot(x[:, o:o + width], ones) for o in range(0, x.shape[1], width)]
    return parts[0] if len(parts) == 1 else jnp.concatenate(parts, axis=1)


def _mm_kernel(a_ref, b_ref, o_ref):
    o_ref[...] = _dot(a_ref[...], b_ref[...]).astype(o_ref.dtype)


def _matmul(a, b, out_dtype, tm, tn):
    m, k = a.shape
    n = b.shape[1]
    tm, tn = _tile(m, tm), _tile(n, tn)
    return pl.pallas_call(
        _mm_kernel,
        grid=(n // tn, m // tm),
        in_specs=[pl.BlockSpec((tm, k), lambda j, i: (i, 0)),
                  pl.BlockSpec((k, tn), lambda j, i: (0, j))],
        out_specs=pl.BlockSpec((tm, tn), lambda j, i: (i, j)),
        out_shape=jax.ShapeDtypeStruct((m, n), out_dtype),
        compiler_params=_cparams(("parallel", "parallel")),
    )(a, b)


def _bmm_shared_lhs(a, b, out_dtype, tm, tn):
    m, k = a.shape
    nb, _, n = b.shape
    tm, tn = _tile(m, tm), _tile(n, tn)
    return pl.pallas_call(
        _mm_kernel,
        grid=(m // tm, nb, n // tn),
        in_specs=[pl.BlockSpec((tm, k), lambda i, bb, j: (i, 0)),
                  pl.BlockSpec((None, k, tn), lambda i, bb, j: (bb, 0, j))],
        out_specs=pl.BlockSpec((None, tm, tn), lambda i, bb, j: (bb, i, j)),
        out_shape=jax.ShapeDtypeStruct((nb, m, n), out_dtype),
        compiler_params=_cparams(("parallel", "parallel", "parallel")),
    )(a, b)


def _inproj_kernel(seq, xm_ref, xp_ref, xn_ref, shift_ref, scale_ref, g_ref, w_ref, cw_ref, cb_ref,
                   z_ref, h_ref, zz_ref):
    tm = xm_ref.shape[0]
    i = pl.program_id(0)

    @pl.when(pl.program_id(1) == 0)
    def _():
        def norm(xv):
            y = xv * lax.rsqrt(jnp.mean(xv * xv, axis=-1, keepdims=True) + NORM_EPS) * g_ref[...]
            return y * (1.0 + scale_ref[...]) + shift_ref[...]
        keep_prev = jnp.where((i * tm) % seq == 0, 0.0, 1.0)
        keep_next = jnp.where(((i + 1) * tm) % seq == 0, 0.0, 1.0)
        h_ref[0:HALO] = (norm(xp_ref[...]) * keep_prev).astype(BF16)
        h_ref[HALO:HALO + tm] = norm(xm_ref[...]).astype(BF16)
        h_ref[HALO + tm:] = (norm(xn_ref[...]) * keep_next).astype(BF16)

    zz_ref[...] = jnp.dot(h_ref[...], w_ref[...], preferred_element_type=F32)
    cw = cw_ref[...]
    z_ref[...] = (zz_ref[HALO - 1:HALO - 1 + tm] * cw[0:1] + zz_ref[HALO:HALO + tm] * cw[1:2]
                  + zz_ref[HALO + 1:HALO + 1 + tm] * cw[2:3] + cb_ref[...]).astype(z_ref.dtype)


def _inproj(x, shift, scale, g, w, cw, cb, tm, tn):
    nb, seq, d = x.shape
    n = w.shape[1]
    tm, tn = _tile(seq, tm), _tile(n, tn)
    m = nb * seq
    n_halo = m // HALO
    per_batch = shift.shape[0] > 1
    mod_spec = pl.BlockSpec((None, 1, d), (lambda i, j: ((i * tm) // seq, 0, 0)) if per_batch
                            else (lambda i, j: (0, 0, 0)))
    x2 = x.reshape(m, d)
    out = pl.pallas_call(
        functools.partial(_inproj_kernel, seq),
        grid=(m // tm, n // tn),
        in_specs=[pl.BlockSpec((tm, d), lambda i, j: (i, 0)),
                  pl.BlockSpec((HALO, d), lambda i, j: (jnp.maximum(i * (tm // HALO) - 1, 0), 0)),
                  pl.BlockSpec((HALO, d), lambda i, j: (jnp.minimum((i + 1) * (tm // HALO), n_halo - 1), 0)),
                  mod_spec, mod_spec,
                  pl.BlockSpec((1, d), lambda i, j: (0, 0)),
                  pl.BlockSpec((d, tn), lambda i, j: (0, j)),
                  pl.BlockSpec((3, tn), lambda i, j: (0, j)),
                  pl.BlockSpec((1, tn), lambda i, j: (0, j))],
        out_specs=pl.BlockSpec((tm, tn), lambda i, j: (i, j)),
        out_shape=jax.ShapeDtypeStruct((m, n), BF16),
        scratch_shapes=[pltpu.VMEM((tm + 2 * HALO, d), BF16), pltpu.VMEM((tm + 2 * HALO, tn), F32)],
        compiler_params=_cparams(("parallel", "arbitrary")),
    )(x2, x2, x2, shift, scale, g, w, cw, cb)
    return out.reshape(nb, seq, n)


def _dft_matrices(length):
    n = 2 * length
    f = lax.broadcasted_iota(jnp.int32, (length, length), 0)
    t = lax.broadcasted_iota(jnp.int32, (length, length), 1)
    ang = ((f * t) % n).astype(F32) * (2.0 * math.pi / n)
    cos, sin = jnp.cos(ang), jnp.sin(ang)
    nyq = jnp.where(t % 2 == 0, 1.0, -1.0)
    fwd = jnp.concatenate([cos, jnp.where(f == 0, nyq, -sin)], axis=0)
    f_t = f.T
    inv = jnp.concatenate([jnp.where(f_t == 0, 1.0, 2.0 * cos.T),
                           jnp.where(f_t == 0, nyq.T, -2.0 * sin.T)], axis=1)
    return fwd.astype(BF16), inv.astype(BF16)


def _dft_fwd_kernel(f_ref, x1_ref, v_ref, o_ref, u_ref):
    @pl.when(pl.program_id(2) == 0)
    def _():
        u_ref[...] = (x1_ref[...].astype(F32) * v_ref[...].astype(F32)).astype(BF16)

    o_ref[...] = jnp.dot(f_ref[...], u_ref[...], preferred_element_type=F32).astype(o_ref.dtype)


def _dft_fwd(fwd, z, hy, tm, tn):
    nb, seq, _ = z.shape
    tm, tn = _tile(2 * seq, tm), _tile(hy, tn)
    nj = hy // tn
    return pl.pallas_call(
        _dft_fwd_kernel,
        grid=(nb, nj, 2 * seq // tm),
        in_specs=[pl.BlockSpec((tm, seq), lambda b, j, i: (i, 0)),
                  pl.BlockSpec((None, seq, tn), lambda b, j, i: (b, 0, nj + j)),
                  pl.BlockSpec((None, seq, tn), lambda b, j, i: (b, 0, 2 * nj + j))],
        out_specs=pl.BlockSpec((None, tm, tn), lambda b, j, i: (b, i, j)),
        out_shape=jax.ShapeDtypeStruct((nb, 2 * seq, hy), BF16),
        scratch_shapes=[pltpu.VMEM((seq, tn), BF16)],
        compiler_params=_cparams(("parallel", "parallel", "arbitrary")),
    )(fwd, z, z)


def _dft_inv_kernel(g_ref, s_ref, t_ref, x0_ref, x1_ref, v_ref, bias_ref, ng_ref, o_ref, y_ref):
    half = s_ref.shape[0] // 2

    @pl.when(pl.program_id(2) == 0)
    def _():
        re = s_ref[:half].astype(F32)
        im = s_ref[half:].astype(F32)
        t_re = t_ref[:half].astype(F32)
        t_im = t_ref[half:].astype(F32)
        row0 = lax.broadcasted_iota(jnp.int32, (half, 1), 0) == 0
        y_ref[:half] = (re * t_re - jnp.where(row0, 0.0, im * t_im)).astype(BF16)
        y_ref[half:] = (im * jnp.where(row0, t_im, t_re) + jnp.where(row0, 0.0, re * t_im)).astype(BF16)

    conv = jnp.dot(g_ref[...], y_ref[...], preferred_element_type=F32)
    u = x1_ref[...].astype(F32) * v_ref[...].astype(F32)
    y = x0_ref[...].astype(F32) * (conv + bias_ref[...] * u)
    ms = _group_sum(y * y, _group_ones(min(y.shape[1], PACK_W))) * (1.0 / HEAD)
    o_ref[...] = (y * lax.rsqrt(ms + NORM_EPS) * ng_ref[...]).astype(o_ref.dtype)


def _dft_inv(inv, spec, taps, z, bias, norm_g, tm, tn):
    nb, seq, _ = z.shape
    hy = spec.shape[2]
    tm, tn = _tile(seq, tm), _tile(hy, tn)
    nj = hy // tn
    row = lambda k: pl.BlockSpec((None, tm, tn), lambda b, j, i: (b, i, k * nj + j))
    vec = pl.BlockSpec((1, tn), lambda b, j, i: (0, j))
    return pl.pallas_call(
        _dft_inv_kernel,
        grid=(nb, nj, seq // tm),
        in_specs=[pl.BlockSpec((tm, 2 * seq), lambda b, j, i: (i, 0)),
                  pl.BlockSpec((None, 2 * seq, tn), lambda b, j, i: (b, 0, j)),
                  pl.BlockSpec((2 * seq, tn), lambda b, j, i: (0, j)),
                  row(0), row(1), row(2), vec, vec],
        out_specs=pl.BlockSpec((None, tm, tn), lambda b, j, i: (b, i, j)),
        out_shape=jax.ShapeDtypeStruct((nb, seq, hy), BF16),
        scratch_shapes=[pltpu.VMEM((2 * seq, tn), BF16)],
        compiler_params=_cparams(("parallel", "parallel", "arbitrary")),
    )(inv, spec, taps, z, z, z, bias, norm_g)


def _hyena_filter(length, w1, b1, w2, b2, w3, b3, freq, w4):
    hy = w4.shape[1] // 2
    t = jnp.linspace(0.0, 1.0, length, dtype=F32)[:, None]
    ang = (2.0 * math.pi / length) * jnp.arange(length, dtype=F32)[:, None]
    bands = jnp.linspace(1e-4, FILTER_BANDS - 1, FILTER_BANDS, dtype=F32)[None, :]
    feats = jnp.concatenate([t, jnp.cos(bands * ang), -jnp.sin(bands * ang)], axis=-1)
    hp = lax.Precision.HIGHEST
    h = jnp.sin(freq * (jnp.dot(feats, w1, precision=hp) + b1))
    h = jnp.sin(freq * (jnp.dot(h, w2, precision=hp) + b2))
    h = jnp.sin(freq * (jnp.dot(h, w3, precision=hp) + b3))
    h = jnp.dot(h, w4, precision=hp)
    deltas = jnp.abs(jnp.linspace(math.log(FILTER_TARGET) / SLOW_DECAY_PCT,
                                  math.log(FILTER_TARGET) / FAST_DECAY_PCT, hy, dtype=F32))
    h = h * jnp.exp(-t * jnp.tile(deltas, 2))
    return h[:, :hy], h[:, hy:]


def _taps_spectrum(fwd, h_fwd, h_bwd):
    length = h_fwd.shape[0]
    sig = jnp.stack([h_fwd, h_bwd.at[0].set(0.0)]).astype(BF16)
    spec = _bmm_shared_lhs(fwd, sig, F32, 1024, 1024)
    re, im = spec[:, :length], spec[:, length:]
    scale = 1.0 / (2 * length)
    row0 = lax.broadcasted_iota(jnp.int32, (length, 1), 0) == 0
    t_re = (re[0] + re[1]) * scale
    t_im = jnp.where(row0, im[0] + im[1], im[0] - im[1]) * scale
    return jnp.concatenate([t_re, t_im], axis=0).astype(BF16)


def _scan_kernel(r_ref, k_ref, v_ref, lw_ref, la_ref, w2_ref, a2_ref, w0_ref, a0_ref, kks_ref, ka_ref, rk_ref,
                 s0_ref, y_ref, bonus_ref, sfin_ref, ht_ref):
    c = CHUNK
    w = PACK_W
    direction = pl.program_id(0)
    step = pl.program_id(2)
    n_groups = r_ref.shape[1] // w
    lora = w2_ref.shape[0]

    @pl.when(step == 0)
    def _():
        ht_ref[...] = s0_ref[...]

    sign = 1 - 2 * direction
    d_sq = (lax.broadcasted_iota(jnp.int32, (c, c), 1) - lax.broadcasted_iota(jnp.int32, (c, c), 0)) * sign
    incl_sq = jnp.where(d_sq <= 0, 1.0, 0.0).astype(BF16)
    d_c = (lax.broadcasted_iota(jnp.int32, (c, w), 1) % c - lax.broadcasted_iota(jnp.int32, (c, w), 0)) * sign
    strict_c = d_c < 0
    incl_c = d_c <= 0
    eye_c = jnp.where(d_c == 0, 1.0, 0.0)
    bd_mask = (lax.broadcasted_iota(jnp.int32, (w, w), 0) // c) == (lax.broadcasted_iota(jnp.int32, (w, w), 1) // HEAD)
    ones_bd = jnp.where(bd_mask, 1.0, 0.0).astype(BF16)

    def bd(x):
        xb = x.astype(BF16)
        return jnp.where(bd_mask, jnp.concatenate([xb] * PACK, axis=0), jnp.zeros((), BF16))

    def mm_nt(a, b):
        return lax.dot_general(a.astype(BF16), b.astype(BF16), (((1,), (1,)), ((), ())),
                               preferred_element_type=F32)

    r_all = r_ref[...].astype(F32)
    k_all = k_ref[...].astype(F32)
    v_all = v_ref[...].astype(F32)
    lw_pre = lw_ref[...].astype(F32)
    la_pre = la_ref[...].astype(F32)
    fwd = direction == 0
    lw_pre = jnp.where(fwd, lw_pre[:, :lora], lw_pre[:, lora:])
    la_pre = jnp.where(fwd, la_pre[:, :lora], la_pre[:, lora:])
    x_w = w0_ref[...] + _dot(jnp.tanh(lw_pre), w2_ref[...])
    lw_all = -math.exp(-0.5) * jax.nn.sigmoid(x_w)
    a_sig = jax.nn.sigmoid(a0_ref[...] + _dot(la_pre, a2_ref[...]))
    kk = k_all * kks_ref[...]
    kk = kk * lax.rsqrt(jnp.maximum(_group_sum(kk * kk, ones_bd), 1e-24))
    kd_all = k_all * (1.0 + (a_sig - 1.0) * ka_ref[...])
    bb_all = kk * a_sig
    bonus_ref[...] = (_group_sum(r_all * kd_all * rk_ref[...], ones_bd) * v_all).astype(bonus_ref.dtype)

    lw_hi = lw_all.astype(BF16)
    lw_lo = (lw_all - lw_hi.astype(F32)).astype(BF16)
    cum = jnp.dot(jnp.concatenate([incl_sq, incl_sq], axis=1), jnp.concatenate([lw_hi, lw_lo], axis=0),
                  preferred_element_type=F32)
    tot = jnp.sum(lw_all, axis=0, keepdims=True)
    g_inv = jnp.exp(-cum)
    g_rem = jnp.exp(tot - cum)
    g_tot = jnp.exp(tot)
    a_t_all = -kk * jnp.exp(cum - lw_all)
    r_t_all = r_all * jnp.exp(cum)
    b_t_all = bb_all * g_inv
    k_t_all = kd_all * g_inv
    b_s_all = bb_all * g_rem
    k_s_all = kd_all * g_rem

    groups = range(n_groups)
    sls = [slice(g * w, (g + 1) * w) for g in groups]
    hts = [ht_ref[g] for g in groups]
    lhs = [jnp.concatenate([a_t_all[:, sl], r_t_all[:, sl]], axis=0) for sl in sls]
    scores = [mm_nt(lhs[g], jnp.concatenate([bd(b_t_all[:, sls[g]]), bd(k_t_all[:, sls[g]])], axis=0))
              for g in groups]
    h0 = [mm_nt(lhs[g], hts[g]) for g in groups]
    a_ab = [jnp.where(strict_c, s[:c, :w], 0.0) for s in scores]
    a_ak = [jnp.where(strict_c, s[:c, w:], 0.0) for s in scores]
    a_rb = [jnp.where(incl_c, s[c:, :w], 0.0) for s in scores]
    a_rk = [jnp.where(incl_c, s[c:, w:], 0.0) for s in scores]
    bd_v = [bd(v_all[:, sl]) for sl in sls]
    x = [h0[g][:c] + _dot(a_ak[g], bd_v[g]) for g in groups]

    t_inv = [eye_c + a for a in a_ab]
    p = [_dot(a, bd(a)) for a in a_ab]
    n_sq = int(math.log2(c)) - 1
    for it in range(n_sq):
        bd_p = [bd(q) for q in p]
        if it + 1 < n_sq:
            pt = [_dot(jnp.concatenate([p[g], t_inv[g]], axis=0), bd_p[g]) for g in groups]
            p = [q[:c] for q in pt]
            t_inv = [t_inv[g] + pt[g][c:] for g in groups]
        else:
            t_inv = [t_inv[g] + _dot(t_inv[g], bd_p[g]) for g in groups]
    u = [_dot(t_inv[g], bd(x[g])) for g in groups]

    y = [h0[g][c:] + _dot(jnp.concatenate([a_rb[g], a_rk[g]], axis=1),
                          jnp.concatenate([bd(u[g]), bd_v[g]], axis=0)) for g in groups]
    for g in groups:
        y_ref[:, sls[g]] = y[g].astype(y_ref.dtype)

    for g in groups:
        uv = jnp.concatenate([u[g], v_all[:, sls[g]]], axis=0).astype(BF16)
        bk = jnp.concatenate([b_s_all[:, sls[g]], k_s_all[:, sls[g]]], axis=0).astype(BF16)
        upd = lax.dot_general(uv, bk, (((0,), (0,)), ((), ())), preferred_element_type=F32)
        ht_ref[g] = hts[g] * g_tot[:, sls[g]] + jnp.where(bd_mask, upd, 0.0)

    @pl.when(step == pl.num_programs(2) - 1)
    def _():
        sfin_ref[...] = ht_ref[...]


def _rwkv_scan(z, col0, lora_col0, w2, a2, w0, a0, kks, ka, rk, s0):
    nb, seq, _ = z.shape
    lora, ch = w2.shape[1:]
    c = CHUNK
    assert CHUNK == HEAD and ch % PACK_W == 0 and seq % c == 0 and 2 * lora == LANE
    assert col0 % ch == 0 and lora_col0 % LANE == 0
    nc = seq // c
    ng = ch // PACK_W
    chunk_of = lambda n, i: jnp.where(n == 0, i, nc - 1 - i)
    zcol = lambda k: pl.BlockSpec((None, c, ch), lambda n, b, i: (b, chunk_of(n, i), col0 // ch + k))
    zlora = lambda k: pl.BlockSpec((None, c, LANE), lambda n, b, i: (b, chunk_of(n, i), lora_col0 // LANE + k))
    per_dir_w = pl.BlockSpec((None, lora, ch), lambda n, b, i: (n, 0, 0))
    per_dir_v = pl.BlockSpec((None, 1, ch), lambda n, b, i: (n, 0, 0))
    shared_v = pl.BlockSpec((1, ch), lambda n, b, i: (0, 0))
    state = pl.BlockSpec((None, None, ng, PACK_W, PACK_W), lambda n, b, i: (n, b, 0, 0, 0))
    out = pl.BlockSpec((None, None, c, ch), lambda n, b, i: (n, b, chunk_of(n, i), 0))
    return pl.pallas_call(
        _scan_kernel,
        grid=(2, nb, nc),
        in_specs=[zcol(0), zcol(1), zcol(2), zlora(0), zlora(1), per_dir_w, per_dir_w, per_dir_v, per_dir_v,
                  shared_v, shared_v, shared_v, state],
        out_specs=[out, out, state],
        out_shape=[jax.ShapeDtypeStruct((2, nb, seq, ch), BF16), jax.ShapeDtypeStruct((2, nb, seq, ch), BF16),
                   jax.ShapeDtypeStruct((2, nb, ng, PACK_W, PACK_W), F32)],
        scratch_shapes=[pltpu.VMEM((ng, PACK_W, PACK_W), F32)],
        compiler_params=_cparams(("parallel", "parallel", "arbitrary")),
    )(z, z, z, z, z, w2, a2, w0.reshape(2, 1, ch), a0.reshape(2, 1, ch), kks, ka, rk, s0)


def _outproj_kernel(yh_ref, ys0_ref, ys1_ref, b0_ref, b1_ref, lg_ref, g2_ref, lng_ref, lnb_ref, ow_ref,
                    x_ref, gate_ref, n2g_ref, shift_ref, scale_ref, rwh_ref, rwl_ref, rb_ref,
                    x1_ref, h2_ref, lg_out_ref, mix_ref):
    hy = yh_ref.shape[1]
    ones = _group_ones(PACK_W)
    y = ys0_ref[...].astype(F32) + ys1_ref[...].astype(F32)
    mu = _group_sum(y, ones) * (1.0 / HEAD)
    yc = y - mu
    var = _group_sum(yc * yc, ones) * (1.0 / HEAD)
    y = yc * lax.rsqrt(var + LNX_EPS) * lng_ref[...] + lnb_ref[...]
    gate = _dot(jax.nn.sigmoid(lg_ref[...].astype(F32)), g2_ref[...])
    y = (y + b0_ref[...].astype(F32) + b1_ref[...].astype(F32)) * gate
    mix_ref[:, :hy] = yh_ref[...]
    mix_ref[:, hy:] = y.astype(BF16)
    mix = jnp.dot(mix_ref[...], ow_ref[...], preferred_element_type=F32)
    x1 = x_ref[...] + gate_ref[...] * mix
    x1_ref[...] = x1
    h2 = x1 * lax.rsqrt(jnp.mean(x1 * x1, axis=-1, keepdims=True) + NORM_EPS) * n2g_ref[...]
    h2 = h2 * (1.0 + scale_ref[...]) + shift_ref[...]
    h2_ref[...] = h2.astype(BF16)
    h_hi = h2.astype(BF16)
    h_lo = (h2 - h_hi.astype(F32)).astype(BF16)
    logits = (jnp.dot(h_hi, rwh_ref[...], preferred_element_type=F32)
              + jnp.dot(h_hi, rwl_ref[...], preferred_element_type=F32)
              + jnp.dot(h_lo, rwh_ref[...], preferred_element_type=F32))
    lg_out_ref[...] = logits + rb_ref[...]


def _outproj(y_hy, ys, bonus, z, lora_g_col, g2, lnx_g, lnx_b, out_w, x, gate, n2g, shift, scale,
             rw_hi, rw_lo, rb, tm):
    nb, seq, d = x.shape
    m = nb * seq
    hy = y_hy.shape[2]
    rw = ys.shape[3]
    tm = _tile(seq, tm)
    lg = g2.shape[0]
    ne = rw_hi.shape[1]
    assert lora_g_col % lg == 0
    ys2 = ys.reshape(2, m, rw)
    bn2 = bonus.reshape(2, m, rw)
    nz = z.shape[2]
    full = lambda r, c: pl.BlockSpec((r, c), lambda i: (0, 0))
    dir_spec = lambda n: pl.BlockSpec((None, tm, rw), lambda i: (n, i, 0))
    mod_spec = pl.BlockSpec((None, 1, d), lambda i: ((i * tm) // seq, 0, 0))
    return pl.pallas_call(
        _outproj_kernel,
        grid=(m // tm,),
        in_specs=[pl.BlockSpec((tm, hy), lambda i: (i, 0)), dir_spec(0), dir_spec(1), dir_spec(0), dir_spec(1),
                  pl.BlockSpec((tm, lg), lambda i: (i, lora_g_col // lg)),
                  full(lg, rw), full(1, rw), full(1, rw), full(hy + rw, d),
                  pl.BlockSpec((tm, d), lambda i: (i, 0)), mod_spec, full(1, d), mod_spec, mod_spec,
                  full(d, ne), full(d, ne), full(1, ne)],
        out_specs=[pl.BlockSpec((tm, d), lambda i: (i, 0)), pl.BlockSpec((tm, d), lambda i: (i, 0)),
                   pl.BlockSpec((tm, ne), lambda i: (i, 0))],
        out_shape=[jax.ShapeDtypeStruct((m, d), F32), jax.ShapeDtypeStruct((m, d), BF16),
                   jax.ShapeDtypeStruct((m, ne), F32)],
        scratch_shapes=[pltpu.VMEM((tm, hy + rw), BF16)],
        compiler_params=_cparams(("parallel",)),
    )(y_hy.reshape(m, hy), ys2, ys2, bn2, bn2, z.reshape(m, nz), g2, lnx_g, lnx_b, out_w,
      x.reshape(m, d), gate, n2g, shift, scale, rw_hi, rw_lo, rb)


def _expert_changed(te_ref, i):
    prev = te_ref[jnp.maximum(i - 1, 0)]
    return jnp.logical_or(i == 0, te_ref[i] != prev)


def _moe_up_kernel(te_ref, nu_ref, x_ref, wg_ref, wu_ref, bg_ref, bu_ref, h_ref, wg_bf, wu_bf):
    i = pl.program_id(1)

    @pl.when(_expert_changed(te_ref, i))
    def _():
        wg_bf[...] = wg_ref[...].astype(BF16)
        wu_bf[...] = wu_ref[...].astype(BF16)

    @pl.when(i < nu_ref[0])
    def _():
        x = x_ref[...]
        g = jnp.dot(x, wg_bf[...], preferred_element_type=F32) + bg_ref[...]
        u = jnp.dot(x, wu_bf[...], preferred_element_type=F32) + bu_ref[...]
        g = jnp.minimum(g, SWIGLU_LIMIT)
        u = jnp.clip(u, -SWIGLU_LIMIT, SWIGLU_LIMIT)
        h_ref[...] = ((u + 1.0) * (g * jax.nn.sigmoid(SWIGLU_ALPHA * g))).astype(h_ref.dtype)

    @pl.when(i >= nu_ref[0])
    def _():
        h_ref[...] = jnp.zeros_like(h_ref)


def _moe_down_kernel(te_ref, nu_ref, h_ref, wd_ref, bd_ref, y_ref, wd_bf):
    i = pl.program_id(1)

    @pl.when(_expert_changed(te_ref, i))
    def _():
        wd_bf[...] = wd_ref[...].astype(BF16)

    @pl.when(i < nu_ref[0])
    def _():
        y = jnp.dot(h_ref[...], wd_bf[...], preferred_element_type=F32) + bd_ref[...]
        y_ref[...] = y.astype(y_ref.dtype)

    @pl.when(i >= nu_ref[0])
    def _():
        y_ref[...] = jnp.zeros_like(y_ref)


def _moe_up(tile_expert, n_used, xs, wg, wu, bg, bu, tn):
    n_rows, d = xs.shape
    e, _, f = wg.shape
    tm = MOE_TILE
    tn = _tile(f, tn)
    w_spec = pl.BlockSpec((None, d, tn), lambda j, i, te, nu: (te[i], 0, j))
    b_spec = pl.BlockSpec((None, 1, tn), lambda j, i, te, nu: (te[i], 0, j))
    return pl.pallas_call(
        _moe_up_kernel,
        grid_spec=pltpu.PrefetchScalarGridSpec(
            num_scalar_prefetch=2,
            grid=(f // tn, n_rows // tm),
            in_specs=[pl.BlockSpec((tm, d), lambda j, i, te, nu: (i, 0)), w_spec, w_spec, b_spec, b_spec],
            out_specs=pl.BlockSpec((tm, tn), lambda j, i, te, nu: (i, j)),
            scratch_shapes=[pltpu.VMEM((d, tn), BF16), pltpu.VMEM((d, tn), BF16)]),
        out_shape=jax.ShapeDtypeStruct((n_rows, f), BF16),
        compiler_params=_cparams(("arbitrary", "arbitrary")),
    )(tile_expert, n_used, xs, wg, wu, bg.reshape(e, 1, f), bu.reshape(e, 1, f))


def _moe_down(tile_expert, n_used, hs, wd, bdn, tn):
    n_rows, f = hs.shape
    e, _, d = wd.shape
    tm = MOE_TILE
    tn = _tile(d, tn)
    return pl.pallas_call(
        _moe_down_kernel,
        grid_spec=pltpu.PrefetchScalarGridSpec(
            num_scalar_prefetch=2,
            grid=(d // tn, n_rows // tm),
            in_specs=[pl.BlockSpec((tm, f), lambda j, i, te, nu: (i, 0)),
                      pl.BlockSpec((None, f, tn), lambda j, i, te, nu: (te[i], 0, j)),
                      pl.BlockSpec((None, 1, tn), lambda j, i, te, nu: (te[i], 0, j))],
            out_specs=pl.BlockSpec((tm, tn), lambda j, i, te, nu: (i, j)),
            scratch_shapes=[pltpu.VMEM((f, tn), BF16)]),
        out_shape=jax.ShapeDtypeStruct((n_rows, d), BF16),
        compiler_params=_cparams(("arbitrary", "arbitrary")),
    )(tile_expert, n_used, hs, wd, bdn.reshape(e, 1, d))


def _moe(h2, logits, wg, bg, wu, bu, wd, bdn):
    n_tok, d = h2.shape
    e = logits.shape[1]
    tm = MOE_TILE
    top_logits, top_idx = lax.top_k(logits, TOP_K)
    gates = jax.nn.softmax(top_logits, axis=-1)
    n_asg = n_tok * TOP_K
    e_flat = top_idx.reshape(n_asg).astype(jnp.int32)
    asg = jnp.arange(n_asg, dtype=jnp.int32)
    e_sorted, order = lax.sort((e_flat, asg), num_keys=1, is_stable=True)
    counts = jnp.sum((e_flat[:, None] == jnp.arange(e, dtype=jnp.int32)[None, :]).astype(jnp.int32), axis=0)
    raw_start = jnp.cumsum(counts) - counts
    padded = (counts + tm - 1) // tm * tm
    pad_end = jnp.cumsum(padded)
    pad_start = pad_end - padded
    dest_sorted = (pad_start - raw_start)[e_sorted] + asg
    _, dest = lax.sort((order, dest_sorted), num_keys=1)
    n_tiles = -(-(n_asg + e * (tm - 1)) // tm)
    n_rows = n_tiles * tm
    tile_expert = jnp.minimum(
        jnp.searchsorted(pad_end, jnp.arange(n_tiles, dtype=jnp.int32) * tm, side='right'), e - 1).astype(jnp.int32)
    n_used = (pad_end[-1] // tm).astype(jnp.int32).reshape(1)
    row = jnp.arange(n_rows, dtype=jnp.int32)
    row_e = jnp.repeat(tile_expert, tm)
    rank = row - pad_start[row_e]
    src = jnp.clip(raw_start[row_e] + rank, 0, n_asg - 1)
    row_tok = jnp.where(rank < counts[row_e], order[src] // TOP_K, 0)

    xs = h2[row_tok]
    hs = _moe_up(tile_expert, n_used, xs, wg, wu, bg, bu, 512)
    ys = _moe_down(tile_expert, n_used, hs, wd, bdn, 1024)
    picked = ys[dest.reshape(n_tok, TOP_K)].astype(F32)
    return jnp.sum(picked * gates[:, :, None], axis=1)


def _rmsnorm(x, g):
    return x * lax.rsqrt(jnp.mean(x * x, axis=-1, keepdims=True) + NORM_EPS) * g


def kernel(x, c, ctx, c_ctx, ada_w, ada_b, norm1_g, norm2_g, in_w, conv_w, conv_b, hy_w1, hy_b1, hy_w2, hy_b2, hy_w3, hy_b3, hy_freq, hy_w4, hy_bias, hy_norm_g, rw_w0, rw_w1, rw_w2, rw_a0, rw_a1, rw_a2, rw_kk, rw_ka, rw_rk, rw_g1, rw_g2, rw_lnx_g, rw_lnx_b, out_w, router_w, router_b, ex_w_gate, ex_b_gate, ex_w_up, ex_b_up, ex_w_down, ex_b_down, final_g):
    assert ada_w.shape[0] == 1, "single-layer block: context outputs never reach a latent token"
    nb, seq, d = x.shape
    hy = hy_bias.shape[1]
    rw = rw_kk.shape[1]
    n_hy = 3 * hy
    proj = in_w.shape[2]
    n_exp = router_w.shape[2]

    cond = jnp.concatenate([jax.nn.silu(c), jax.nn.silu(c_ctx)[None]], axis=0)
    mod = _matmul(cond, ada_w[0], F32, cond.shape[0], 1024) + ada_b[0]
    mod_x = [m[:, None, :] for m in jnp.split(mod[:nb], 6, axis=-1)]
    mod_c = [m[:, None, :] for m in jnp.split(mod[nb:], 6, axis=-1)]

    lora_w = jnp.concatenate([rw_w1[0, 0], rw_w1[0, 1], rw_a1[0, 0], rw_a1[0, 1], rw_g1[0]], axis=1)
    n_lora = lora_w.shape[1]
    nz = -(-(proj + n_lora) // 512) * 512
    pad = nz - proj - n_lora
    w_all = jnp.concatenate([in_w[0], lora_w, jnp.zeros((d, pad), F32)], axis=1).astype(BF16)
    pass_taps = jnp.concatenate([jnp.zeros((1, n_lora + pad), F32), jnp.ones((1, n_lora + pad), F32),
                                 jnp.zeros((1, n_lora + pad), F32)], axis=0)
    cw_all = jnp.concatenate([conv_w[0], pass_taps], axis=1)
    cb_all = jnp.concatenate([conv_b[0], jnp.zeros((n_lora + pad,), F32)])[None]
    g1 = norm1_g[0][None]
    zx = _inproj(x, mod_x[0], mod_x[1], g1, w_all, cw_all, cb_all, 1024, 512)
    zc = _inproj(ctx, mod_c[0], mod_c[1], g1, w_all, cw_all, cb_all, 256, 512)

    h_fwd, h_bwd = _hyena_filter(seq, hy_w1[0], hy_b1[0], hy_w2[0], hy_b2[0], hy_w3[0], hy_b3[0],
                                 hy_freq[0], hy_w4[0])
    fwd, inv = _dft_matrices(seq)
    taps = _taps_spectrum(fwd, h_fwd, h_bwd)
    spec = _dft_fwd(fwd, zx, hy, 1024, 512)
    y_hy = _dft_inv(inv, spec, taps, zx, hy_bias[0][None], hy_norm_g[0][None], 1024, 512)

    scan_args = (n_hy, proj, rw_w2[0].astype(BF16), rw_a2[0].astype(BF16), rw_w0[0], rw_a0[0],
                 rw_kk[0][None], rw_ka[0][None], rw_rk[0].reshape(1, rw))
    s0 = jnp.zeros((2, nb, rw // PACK_W, PACK_W, PACK_W), F32)
    _, _, s_ctx = _rwkv_scan(zc, *scan_args, s0)
    ys, bonus, _ = _rwkv_scan(zx, *scan_args, s_ctx)

    ne = -(-n_exp // LANE) * LANE
    rw_pad = jnp.pad(router_w[0], ((0, 0), (0, ne - n_exp)))
    rw_hi = rw_pad.astype(BF16)
    rw_lo = (rw_pad - rw_hi.astype(F32)).astype(BF16)
    rb = jnp.pad(router_b[0], (0, ne - n_exp))[None]
    x1, h2, logits = _outproj(y_hy, ys, bonus, zx, proj + n_lora - rw_g1.shape[-1], rw_g2[0].astype(BF16),
                              rw_lnx_g[0][None], rw_lnx_b[0][None], out_w[0].astype(BF16), x,
                              mod_x[2], norm2_g[0][None], mod_x[3], mod_x[4], rw_hi, rw_lo, rb, 256)

    moe = _moe(h2, logits[:, :n_exp], ex_w_gate[0], ex_b_gate[0], ex_w_up[0], ex_b_up[0],
               ex_w_down[0], ex_b_down[0]).reshape(nb, seq, d)
    xo = x1.reshape(nb, seq, d) + mod_x[5] * moe
    return _rmsnorm(xo, final_g)
```

```python
import functools
import math

import jax
import jax.numpy as jnp
from jax import lax
from jax.experimental import pallas as pl
from jax.experimental.pallas import tpu as pltpu

F32 = jnp.float32
BF16 = jnp.bfloat16

HEAD = 64
CHUNK = 64
PACK = 4
PACK_W = PACK * HEAD
SCAN_BATCH = 2
HALO = 16
INPROJ_ROWS = 256
FILTER_BANDS = 16
FILTER_TARGET = 1e-2
FAST_DECAY_PCT = 0.3
SLOW_DECAY_PCT = 1.5
TOP_K = 4
SWIGLU_LIMIT = 7.0
SWIGLU_ALPHA = 1.702
NORM_EPS = 1e-6
LNX_EPS = 64e-5
MOE_TILE = 512
LANE = 128
VMEM_LIMIT = 56 * 1024 * 1024


def _cparams(sem):
    return pltpu.CompilerParams(dimension_semantics=sem, vmem_limit_bytes=VMEM_LIMIT)


def _tile(n, want):
    if n <= want:
        return n
    t = want
    while n % t:
        t //= 2
    assert t >= 8, (n, want)
    return t


def _dot(a, b):
    return jnp.dot(a.astype(BF16), b.astype(BF16), preferred_element_type=F32)


def _group_ones(width):
    r = lax.broadcasted_iota(jnp.int32, (width, width), 0) // HEAD
    c = lax.broadcasted_iota(jnp.int32, (width, width), 1) // HEAD
    return jnp.where(r == c, 1.0, 0.0).astype(BF16)


def _group_sum(x, ones):
    width = ones.shape[0]
    parts = [_dot(x[:, o:o + width], ones) for o in range(0, x.shape[1], width)]
    return parts[0] if len(parts) == 1 else jnp.concatenate(parts, axis=1)


def _mm_kernel(a_ref, b_ref, o_ref):
    o_ref[...] = _dot(a_ref[...], b_ref[...]).astype(o_ref.dtype)


def _matmul(a, b, out_dtype, tm, tn):
    m, k = a.shape
    n = b.shape[1]
    tm, tn = _tile(m, tm), _tile(n, tn)
    return pl.pallas_call(
        _mm_kernel,
        grid=(n // tn, m // tm),
        in_specs=[pl.BlockSpec((tm, k), lambda j, i: (i, 0)),
                  pl.BlockSpec((k, tn), lambda j, i: (0, j))],
        out_specs=pl.BlockSpec((tm, tn), lambda j, i: (i, j)),
        out_shape=jax.ShapeDtypeStruct((m, n), out_dtype),
        compiler_params=_cparams(("parallel", "parallel")),
    )(a, b)


def _bmm_shared_lhs(a, b, out_dtype, tm, tn):
    m, k = a.shape
    nb, _, n = b.shape
    tm, tn = _tile(m, tm), _tile(n, tn)
    return pl.pallas_call(
        _mm_kernel,
        grid=(m // tm, nb, n // tn),
        in_specs=[pl.BlockSpec((tm, k), lambda i, bb, j: (i, 0)),
                  pl.BlockSpec((None, k, tn), lambda i, bb, j: (bb, 0, j))],
        out_specs=pl.BlockSpec((None, tm, tn), lambda i, bb, j: (bb, i, j)),
        out_shape=jax.ShapeDtypeStruct((nb, m, n), out_dtype),
        compiler_params=_cparams(("parallel", "parallel", "parallel")),
    )(a, b)


def _inproj_kernel(seq, xm_ref, xp_ref, xn_ref, shift_ref, scale_ref, g_ref, w_ref, cw_ref, cb_ref,
                   z_ref, h_ref, zz_ref):
    tm = xm_ref.shape[0]
    i = pl.program_id(0)

    @pl.when(pl.program_id(1) == 0)
    def _():
        def norm(xv):
            y = xv * lax.rsqrt(jnp.mean(xv * xv, axis=-1, keepdims=True) + NORM_EPS) * g_ref[...]
            return y * (1.0 + scale_ref[...]) + shift_ref[...]
        keep_prev = jnp.where((i * tm) % seq == 0, 0.0, 1.0)
        keep_next = jnp.where(((i + 1) * tm) % seq == 0, 0.0, 1.0)
        h_ref[0:HALO] = (norm(xp_ref[...]) * keep_prev).astype(BF16)
        h_ref[HALO:HALO + tm] = norm(xm_ref[...]).astype(BF16)
        h_ref[HALO + tm:] = (norm(xn_ref[...]) * keep_next).astype(BF16)

    cw = cw_ref[...]
    n_chunks = max(1, tm // INPROJ_ROWS)
    q = tm // n_chunks
    d0 = o0 = 0
    for ci in range(n_chunks):
        last = ci + 1 == n_chunks
        d1 = tm + 2 * HALO if last else (ci + 1) * q + HALO
        o1 = tm if last else d1 - 2 * HALO
        zz_ref[d0:d1] = jnp.dot(h_ref[d0:d1], w_ref[...], preferred_element_type=F32)
        z_ref[o0:o1] = (zz_ref[HALO - 1 + o0:HALO - 1 + o1] * cw[0:1] + zz_ref[HALO + o0:HALO + o1] * cw[1:2]
                        + zz_ref[HALO + 1 + o0:HALO + 1 + o1] * cw[2:3] + cb_ref[...]).astype(z_ref.dtype)
        d0, o0 = d1, o1


def _inproj(x, shift, scale, g, w, cw, cb, tm, tn):
    nb, seq, d = x.shape
    n = w.shape[1]
    tm, tn = _tile(seq, tm), _tile(n, tn)
    m = nb * seq
    n_halo = m // HALO
    per_batch = shift.shape[0] > 1
    mod_spec = pl.BlockSpec((None, 1, d), (lambda i, j: ((i * tm) // seq, 0, 0)) if per_batch
                            else (lambda i, j: (0, 0, 0)))
    x2 = x.reshape(m, d)
    out = pl.pallas_call(
        functools.partial(_inproj_kernel, seq),
        grid=(m // tm, n // tn),
        in_specs=[pl.BlockSpec((tm, d), lambda i, j: (i, 0)),
                  pl.BlockSpec((HALO, d), lambda i, j: (jnp.maximum(i * (tm // HALO) - 1, 0), 0)),
                  pl.BlockSpec((HALO, d), lambda i, j: (jnp.minimum((i + 1) * (tm // HALO), n_halo - 1), 0)),
                  mod_spec, mod_spec,
                  pl.BlockSpec((1, d), lambda i, j: (0, 0)),
                  pl.BlockSpec((d, tn), lambda i, j: (0, j)),
                  pl.BlockSpec((3, tn), lambda i, j: (0, j)),
                  pl.BlockSpec((1, tn), lambda i, j: (0, j))],
        out_specs=pl.BlockSpec((tm, tn), lambda i, j: (i, j)),
        out_shape=jax.ShapeDtypeStruct((m, n), BF16),
        scratch_shapes=[pltpu.VMEM((tm + 2 * HALO, d), BF16), pltpu.VMEM((tm + 2 * HALO, tn), F32)],
        compiler_params=_cparams(("parallel", "arbitrary")),
    )(x2, x2, x2, shift, scale, g, w, cw, cb)
    return out.reshape(nb, seq, n)


def _dft_matrices(length):
    n = 2 * length
    f = lax.broadcasted_iota(jnp.int32, (length, length), 0)
    t = lax.broadcasted_iota(jnp.int32, (length, length), 1)
    ang = ((f * t) % n).astype(F32) * (2.0 * math.pi / n)
    cos, sin = jnp.cos(ang), jnp.sin(ang)
    nyq = jnp.where(t % 2 == 0, 1.0, -1.0)
    fwd = jnp.concatenate([cos, jnp.where(f == 0, nyq, -sin)], axis=0)
    f_t = f.T
    inv = jnp.concatenate([jnp.where(f_t == 0, 1.0, 2.0 * cos.T),
                           jnp.where(f_t == 0, nyq.T, -2.0 * sin.T)], axis=1)
    return fwd.astype(BF16), inv.astype(BF16)


def _dft_fwd_kernel(f_ref, x1_ref, v_ref, o_ref, u_ref):
    @pl.when(pl.program_id(2) == 0)
    def _():
        u_ref[...] = (x1_ref[...].astype(F32) * v_ref[...].astype(F32)).astype(BF16)

    o_ref[...] = jnp.dot(f_ref[...], u_ref[...], preferred_element_type=F32).astype(o_ref.dtype)


def _dft_fwd(fwd, z, hy, tm, tn):
    nb, seq, _ = z.shape
    tm, tn = _tile(2 * seq, tm), _tile(hy, tn)
    nj = hy // tn
    return pl.pallas_call(
        _dft_fwd_kernel,
        grid=(nb, nj, 2 * seq // tm),
        in_specs=[pl.BlockSpec((tm, seq), lambda b, j, i: (i, 0)),
                  pl.BlockSpec((None, seq, tn), lambda b, j, i: (b, 0, nj + j)),
                  pl.BlockSpec((None, seq, tn), lambda b, j, i: (b, 0, 2 * nj + j))],
        out_specs=pl.BlockSpec((None, tm, tn), lambda b, j, i: (b, i, j)),
        out_shape=jax.ShapeDtypeStruct((nb, 2 * seq, hy), BF16),
        scratch_shapes=[pltpu.VMEM((seq, tn), BF16)],
        compiler_params=_cparams(("parallel", "parallel", "arbitrary")),
    )(fwd, z, z)


def _dft_inv_kernel(g_ref, s_ref, t_ref, x0_ref, x1_ref, v_ref, bias_ref, ng_ref, o_ref, y_ref):
    half = s_ref.shape[0] // 2

    @pl.when(pl.program_id(2) == 0)
    def _():
        re = s_ref[:half].astype(F32)
        im = s_ref[half:].astype(F32)
        t_re = t_ref[:half].astype(F32)
        t_im = t_ref[half:].astype(F32)
        row0 = lax.broadcasted_iota(jnp.int32, (half, 1), 0) == 0
        y_ref[:half] = (re * t_re - jnp.where(row0, 0.0, im * t_im)).astype(BF16)
        y_ref[half:] = (im * jnp.where(row0, t_im, t_re) + jnp.where(row0, 0.0, re * t_im)).astype(BF16)

    conv = jnp.dot(g_ref[...], y_ref[...], preferred_element_type=F32)
    u = x1_ref[...].astype(F32) * v_ref[...].astype(F32)
    y = x0_ref[...].astype(F32) * (conv + bias_ref[...] * u)
    ms = _group_sum(y * y, _group_ones(min(y.shape[1], PACK_W))) * (1.0 / HEAD)
    o_ref[...] = (y * lax.rsqrt(ms + NORM_EPS) * ng_ref[...]).astype(o_ref.dtype)


def _dft_inv(inv, spec, taps, z, bias, norm_g, tm, tn):
    nb, seq, _ = z.shape
    hy = spec.shape[2]
    tm, tn = _tile(seq, tm), _tile(hy, tn)
    nj = hy // tn
    row = lambda k: pl.BlockSpec((None, tm, tn), lambda b, j, i: (b, i, k * nj + j))
    vec = pl.BlockSpec((1, tn), lambda b, j, i: (0, j))
    return pl.pallas_call(
        _dft_inv_kernel,
        grid=(nb, nj, seq // tm),
        in_specs=[pl.BlockSpec((tm, 2 * seq), lambda b, j, i: (i, 0)),
                  pl.BlockSpec((None, 2 * seq, tn), lambda b, j, i: (b, 0, j)),
                  pl.BlockSpec((2 * seq, tn), lambda b, j, i: (0, j)),
                  row(0), row(1), row(2), vec, vec],
        out_specs=pl.BlockSpec((None, tm, tn), lambda b, j, i: (b, i, j)),
        out_shape=jax.ShapeDtypeStruct((nb, seq, hy), BF16),
        scratch_shapes=[pltpu.VMEM((2 * seq, tn), BF16)],
        compiler_params=_cparams(("parallel", "parallel", "arbitrary")),
    )(inv, spec, taps, z, z, z, bias, norm_g)


def _hyena_filter(length, w1, b1, w2, b2, w3, b3, freq, w4):
    hy = w4.shape[1] // 2
    t = jnp.linspace(0.0, 1.0, length, dtype=F32)[:, None]
    ang = (2.0 * math.pi / length) * jnp.arange(length, dtype=F32)[:, None]
    bands = jnp.linspace(1e-4, FILTER_BANDS - 1, FILTER_BANDS, dtype=F32)[None, :]
    feats = jnp.concatenate([t, jnp.cos(bands * ang), -jnp.sin(bands * ang)], axis=-1)
    hp = lax.Precision.HIGHEST
    h = jnp.sin(freq * (jnp.dot(feats, w1, precision=hp) + b1))
    h = jnp.sin(freq * (jnp.dot(h, w2, precision=hp) + b2))
    h = jnp.sin(freq * (jnp.dot(h, w3, precision=hp) + b3))
    h = jnp.dot(h, w4, precision=hp)
    deltas = jnp.abs(jnp.linspace(math.log(FILTER_TARGET) / SLOW_DECAY_PCT,
                                  math.log(FILTER_TARGET) / FAST_DECAY_PCT, hy, dtype=F32))
    h = h * jnp.exp(-t * jnp.tile(deltas, 2))
    return h[:, :hy], h[:, hy:]


def _taps_spectrum(fwd, h_fwd, h_bwd):
    length = h_fwd.shape[0]
    sig = jnp.stack([h_fwd, h_bwd.at[0].set(0.0)]).astype(BF16)
    spec = _bmm_shared_lhs(fwd, sig, F32, 1024, 1024)
    re, im = spec[:, :length], spec[:, length:]
    scale = 1.0 / (2 * length)
    row0 = lax.broadcasted_iota(jnp.int32, (length, 1), 0) == 0
    t_re = (re[0] + re[1]) * scale
    t_im = jnp.where(row0, im[0] + im[1], im[0] - im[1]) * scale
    return jnp.concatenate([t_re, t_im], axis=0).astype(BF16)


def _scan_kernel(r_ref, k_ref, v_ref, lw_ref, la_ref, w2_ref, a2_ref, w0_ref, a0_ref, kks_ref, ka_ref, rk_ref,
                 s0_ref, y_ref, bonus_ref, sfin_ref, ht_ref):
    c = CHUNK
    w = PACK_W
    direction = pl.program_id(0)
    step = pl.program_id(2)
    n_batch = r_ref.shape[0]
    n_groups = r_ref.shape[2] // w
    lora = w2_ref.shape[0]

    @pl.when(step == 0)
    def _():
        ht_ref[...] = s0_ref[...]

    sign = 1 - 2 * direction
    d_sq = (lax.broadcasted_iota(jnp.int32, (c, c), 1) - lax.broadcasted_iota(jnp.int32, (c, c), 0)) * sign
    incl_sq = jnp.where(d_sq <= 0, 1.0, 0.0).astype(BF16)
    d_c = (lax.broadcasted_iota(jnp.int32, (c, w), 1) % c - lax.broadcasted_iota(jnp.int32, (c, w), 0)) * sign
    strict_c = d_c < 0
    incl_c = d_c <= 0
    eye_c = jnp.where(d_c == 0, 1.0, 0.0)
    bd_mask = (lax.broadcasted_iota(jnp.int32, (w, w), 0) // c) == (lax.broadcasted_iota(jnp.int32, (w, w), 1) // HEAD)
    ones_bd = jnp.where(bd_mask, 1.0, 0.0).astype(BF16)

    def bd(x):
        xb = x.astype(BF16)
        return jnp.where(bd_mask, jnp.concatenate([xb] * PACK, axis=0), jnp.zeros((), BF16))

    def mm_nt(a, b):
        return lax.dot_general(a.astype(BF16), b.astype(BF16), (((1,), (1,)), ((), ())),
                               preferred_element_type=F32)

    fwd = direction == 0
    incl_2 = jnp.concatenate([incl_sq, incl_sq], axis=1)
    prep = []
    for bi in range(n_batch):
        r_all = r_ref[bi].astype(F32)
        k_all = k_ref[bi].astype(F32)
        v_all = v_ref[bi].astype(F32)
        lw_pre = lw_ref[bi].astype(F32)
        la_pre = la_ref[bi].astype(F32)
        lw_pre = jnp.where(fwd, lw_pre[:, :lora], lw_pre[:, lora:])
        la_pre = jnp.where(fwd, la_pre[:, :lora], la_pre[:, lora:])
        x_w = w0_ref[...] + _dot(jnp.tanh(lw_pre), w2_ref[...])
        lw_all = -math.exp(-0.5) * jax.nn.sigmoid(x_w)
        a_sig = jax.nn.sigmoid(a0_ref[...] + _dot(la_pre, a2_ref[...]))
        kk = k_all * kks_ref[...]
        kk = kk * lax.rsqrt(jnp.maximum(_group_sum(kk * kk, ones_bd), 1e-24))
        kd_all = k_all * (1.0 + (a_sig - 1.0) * ka_ref[...])
        bb_all = kk * a_sig
        bonus_ref[bi] = (_group_sum(r_all * kd_all * rk_ref[...], ones_bd) * v_all).astype(bonus_ref.dtype)

        lw_hi = lw_all.astype(BF16)
        lw_lo = (lw_all - lw_hi.astype(F32)).astype(BF16)
        cum = jnp.dot(incl_2, jnp.concatenate([lw_hi, lw_lo], axis=0), preferred_element_type=F32)
        tot = jnp.sum(lw_all, axis=0, keepdims=True)
        g_inv = jnp.exp(-cum)
        g_rem = jnp.exp(tot - cum)
        prep.append(dict(v=v_all, g_tot=jnp.exp(tot), a_t=-kk * jnp.exp(cum - lw_all), r_t=r_all * jnp.exp(cum),
                         b_t=bb_all * g_inv, k_t=kd_all * g_inv, b_s=bb_all * g_rem, k_s=kd_all * g_rem))

    probs = [(bi, g) for bi in range(n_batch) for g in range(n_groups)]
    idx = range(len(probs))
    cols = [slice(g * w, (g + 1) * w) for _, g in probs]
    take = lambda name: [prep[bi][name][:, cols[q]] for q, (bi, _) in enumerate(probs)]
    v_g, a_t, r_t, b_t, k_t, b_s, k_s = (take(nm) for nm in ("v", "a_t", "r_t", "b_t", "k_t", "b_s", "k_s"))
    hts = [ht_ref[bi, g] for bi, g in probs]
    lhs = [jnp.concatenate([a_t[q], r_t[q]], axis=0) for q in idx]
    scores = [mm_nt(lhs[q], jnp.concatenate([bd(b_t[q]), bd(k_t[q])], axis=0)) for q in idx]
    h0 = [mm_nt(lhs[q], hts[q]) for q in idx]
    a_ab = [jnp.where(strict_c, s[:c, :w], 0.0) for s in scores]
    a_ak = [jnp.where(strict_c, s[:c, w:], 0.0) for s in scores]
    a_rb = [jnp.where(incl_c, s[c:, :w], 0.0) for s in scores]
    a_rk = [jnp.where(incl_c, s[c:, w:], 0.0) for s in scores]
    bd_v = [bd(vq) for vq in v_g]
    x = [h0[q][:c] + _dot(a_ak[q], bd_v[q]) for q in idx]

    t_inv = [eye_c + a for a in a_ab]
    p = [_dot(a, bd(a)) for a in a_ab]
    n_sq = int(math.log2(c)) - 1
    for it in range(n_sq):
        bd_p = [bd(pq) for pq in p]
        if it + 1 < n_sq:
            pt = [_dot(jnp.concatenate([p[q], t_inv[q]], axis=0), bd_p[q]) for q in idx]
            p = [m[:c] for m in pt]
            t_inv = [t_inv[q] + pt[q][c:] for q in idx]
        else:
            t_inv = [t_inv[q] + _dot(t_inv[q], bd_p[q]) for q in idx]
    u = [_dot(t_inv[q], bd(x[q])) for q in idx]

    y = [h0[q][c:] + _dot(jnp.concatenate([a_rb[q], a_rk[q]], axis=1),
                          jnp.concatenate([bd(u[q]), bd_v[q]], axis=0)) for q in idx]
    for q, (bi, _) in enumerate(probs):
        y_ref[bi, :, cols[q]] = y[q].astype(y_ref.dtype)

    for q, (bi, g) in enumerate(probs):
        uv = jnp.concatenate([u[q], v_g[q]], axis=0).astype(BF16)
        bk = jnp.concatenate([b_s[q], k_s[q]], axis=0).astype(BF16)
        upd = lax.dot_general(uv, bk, (((0,), (0,)), ((), ())), preferred_element_type=F32)
        ht_ref[bi, g] = hts[q] * prep[bi]["g_tot"][:, cols[q]] + jnp.where(bd_mask, upd, 0.0)

    @pl.when(step == pl.num_programs(2) - 1)
    def _():
        sfin_ref[...] = ht_ref[...]


def _rwkv_scan(z, col0, lora_col0, w2, a2, w0, a0, kks, ka, rk, s0):
    nb, seq, _ = z.shape
    lora, ch = w2.shape[1:]
    c = CHUNK
    assert CHUNK == HEAD and ch % PACK_W == 0 and seq % c == 0 and 2 * lora == LANE
    assert col0 % ch == 0 and lora_col0 % LANE == 0
    nc = seq // c
    ng = ch // PACK_W
    bs = SCAN_BATCH if nb % SCAN_BATCH == 0 else 1
    chunk_of = lambda n, i: jnp.where(n == 0, i, nc - 1 - i)
    zcol = lambda k: pl.BlockSpec((bs, c, ch), lambda n, b, i: (b, chunk_of(n, i), col0 // ch + k))
    zlora = lambda k: pl.BlockSpec((bs, c, LANE), lambda n, b, i: (b, chunk_of(n, i), lora_col0 // LANE + k))
    per_dir_w = pl.BlockSpec((None, lora, ch), lambda n, b, i: (n, 0, 0))
    per_dir_v = pl.BlockSpec((None, 1, ch), lambda n, b, i: (n, 0, 0))
    shared_v = pl.BlockSpec((1, ch), lambda n, b, i: (0, 0))
    state = pl.BlockSpec((None, bs, ng, PACK_W, PACK_W), lambda n, b, i: (n, b, 0, 0, 0))
    out = pl.BlockSpec((None, bs, c, ch), lambda n, b, i: (n, b, chunk_of(n, i), 0))
    return pl.pallas_call(
        _scan_kernel,
        grid=(2, nb // bs, nc),
        in_specs=[zcol(0), zcol(1), zcol(2), zlora(0), zlora(1), per_dir_w, per_dir_w, per_dir_v, per_dir_v,
                  shared_v, shared_v, shared_v, state],
        out_specs=[out, out, state],
        out_shape=[jax.ShapeDtypeStruct((2, nb, seq, ch), BF16), jax.ShapeDtypeStruct((2, nb, seq, ch), BF16),
                   jax.ShapeDtypeStruct((2, nb, ng, PACK_W, PACK_W), F32)],
        scratch_shapes=[pltpu.VMEM((bs, ng, PACK_W, PACK_W), F32)],
        compiler_params=_cparams(("parallel", "parallel", "arbitrary")),
    )(z, z, z, z, z, w2, a2, w0.reshape(2, 1, ch), a0.reshape(2, 1, ch), kks, ka, rk, s0)


def _outproj_kernel(yh_ref, ys0_ref, ys1_ref, b0_ref, b1_ref, lg_ref, g2_ref, lng_ref, lnb_ref, ow_ref,
                    x_ref, gate_ref, n2g_ref, shift_ref, scale_ref, rwh_ref, rwl_ref, rb_ref,
                    x1_ref, h2_ref, lg_out_ref, mix_ref):
    hy = yh_ref.shape[1]
    ones = _group_ones(PACK_W)
    y = ys0_ref[...].astype(F32) + ys1_ref[...].astype(F32)
    mu = _group_sum(y, ones) * (1.0 / HEAD)
    yc = y - mu
    var = _group_sum(yc * yc, ones) * (1.0 / HEAD)
    y = yc * lax.rsqrt(var + LNX_EPS) * lng_ref[...] + lnb_ref[...]
    gate = _dot(jax.nn.sigmoid(lg_ref[...].astype(F32)), g2_ref[...])
    y = (y + b0_ref[...].astype(F32) + b1_ref[...].astype(F32)) * gate
    mix_ref[:, :hy] = yh_ref[...]
    mix_ref[:, hy:] = y.astype(BF16)
    mix = jnp.dot(mix_ref[...], ow_ref[...], preferred_element_type=F32)
    x1 = x_ref[...] + gate_ref[...] * mix
    x1_ref[...] = x1
    h2 = x1 * lax.rsqrt(jnp.mean(x1 * x1, axis=-1, keepdims=True) + NORM_EPS) * n2g_ref[...]
    h2 = h2 * (1.0 + scale_ref[...]) + shift_ref[...]
    h2_ref[...] = h2.astype(BF16)
    h_hi = h2.astype(BF16)
    h_lo = (h2 - h_hi.astype(F32)).astype(BF16)
    logits = (jnp.dot(h_hi, rwh_ref[...], preferred_element_type=F32)
              + jnp.dot(h_hi, rwl_ref[...], preferred_element_type=F32)
              + jnp.dot(h_lo, rwh_ref[...], preferred_element_type=F32))
    lg_out_ref[...] = logits + rb_ref[...]


def _outproj(y_hy, ys, bonus, z, lora_g_col, g2, lnx_g, lnx_b, out_w, x, gate, n2g, shift, scale,
             rw_hi, rw_lo, rb, tm):
    nb, seq, d = x.shape
    m = nb * seq
    hy = y_hy.shape[2]
    rw = ys.shape[3]
    tm = _tile(seq, tm)
    lg = g2.shape[0]
    ne = rw_hi.shape[1]
    assert lora_g_col % lg == 0
    ys2 = ys.reshape(2, m, rw)
    bn2 = bonus.reshape(2, m, rw)
    nz = z.shape[2]
    full = lambda r, c: pl.BlockSpec((r, c), lambda i: (0, 0))
    dir_spec = lambda n: pl.BlockSpec((None, tm, rw), lambda i: (n, i, 0))
    mod_spec = pl.BlockSpec((None, 1, d), lambda i: ((i * tm) // seq, 0, 0))
    return pl.pallas_call(
        _outproj_kernel,
        grid=(m // tm,),
        in_specs=[pl.BlockSpec((tm, hy), lambda i: (i, 0)), dir_spec(0), dir_spec(1), dir_spec(0), dir_spec(1),
                  pl.BlockSpec((tm, lg), lambda i: (i, lora_g_col // lg)),
                  full(lg, rw), full(1, rw), full(1, rw), full(hy + rw, d),
                  pl.BlockSpec((tm, d), lambda i: (i, 0)), mod_spec, full(1, d), mod_spec, mod_spec,
                  full(d, ne), full(d, ne), full(1, ne)],
        out_specs=[pl.BlockSpec((tm, d), lambda i: (i, 0)), pl.BlockSpec((tm, d), lambda i: (i, 0)),
                   pl.BlockSpec((tm, ne), lambda i: (i, 0))],
        out_shape=[jax.ShapeDtypeStruct((m, d), F32), jax.ShapeDtypeStruct((m, d), BF16),
                   jax.ShapeDtypeStruct((m, ne), F32)],
        scratch_shapes=[pltpu.VMEM((tm, hy + rw), BF16)],
        compiler_params=_cparams(("parallel",)),
    )(y_hy.reshape(m, hy), ys2, ys2, bn2, bn2, z.reshape(m, nz), g2, lnx_g, lnx_b, out_w,
      x.reshape(m, d), gate, n2g, shift, scale, rw_hi, rw_lo, rb)


def _expert_changed(te_ref, i):
    prev = te_ref[jnp.maximum(i - 1, 0)]
    return jnp.logical_or(i == 0, te_ref[i] != prev)


def _moe_up_kernel(te_ref, nu_ref, x_ref, wg_ref, wu_ref, bg_ref, bu_ref, h_ref, wg_bf, wu_bf):
    i = pl.program_id(1)

    @pl.when(_expert_changed(te_ref, i))
    def _():
        wg_bf[...] = wg_ref[...].astype(BF16)
        wu_bf[...] = wu_ref[...].astype(BF16)

    @pl.when(i < nu_ref[0])
    def _():
        x = x_ref[...]
        g = jnp.dot(x, wg_bf[...], preferred_element_type=F32) + bg_ref[...]
        u = jnp.dot(x, wu_bf[...], preferred_element_type=F32) + bu_ref[...]
        g = jnp.minimum(g, SWIGLU_LIMIT)
        u = jnp.clip(u, -SWIGLU_LIMIT, SWIGLU_LIMIT)
        h_ref[...] = ((u + 1.0) * (g * jax.nn.sigmoid(SWIGLU_ALPHA * g))).astype(h_ref.dtype)

    @pl.when(i >= nu_ref[0])
    def _():
        h_ref[...] = jnp.zeros_like(h_ref)


def _moe_down_kernel(te_ref, nu_ref, h_ref, wd_ref, bd_ref, y_ref, wd_bf):
    i = pl.program_id(1)

    @pl.when(_expert_changed(te_ref, i))
    def _():
        wd_bf[...] = wd_ref[...].astype(BF16)

    @pl.when(i < nu_ref[0])
    def _():
        y = jnp.dot(h_ref[...], wd_bf[...], preferred_element_type=F32) + bd_ref[...]
        y_ref[...] = y.astype(y_ref.dtype)

    @pl.when(i >= nu_ref[0])
    def _():
        y_ref[...] = jnp.zeros_like(y_ref)


def _moe_up(tile_expert, n_used, xs, wg, wu, bg, bu, tn):
    n_rows, d = xs.shape
    e, _, f = wg.shape
    tm = MOE_TILE
    tn = _tile(f, tn)
    w_spec = pl.BlockSpec((None, d, tn), lambda j, i, te, nu: (te[i], 0, j))
    b_spec = pl.BlockSpec((None, 1, tn), lambda j, i, te, nu: (te[i], 0, j))
    return pl.pallas_call(
        _moe_up_kernel,
        grid_spec=pltpu.PrefetchScalarGridSpec(
            num_scalar_prefetch=2,
            grid=(f // tn, n_rows // tm),
            in_specs=[pl.BlockSpec((tm, d), lambda j, i, te, nu: (i, 0)), w_spec, w_spec, b_spec, b_spec],
            out_specs=pl.BlockSpec((tm, tn), lambda j, i, te, nu: (i, j)),
            scratch_shapes=[pltpu.VMEM((d, tn), BF16), pltpu.VMEM((d, tn), BF16)]),
        out_shape=jax.ShapeDtypeStruct((n_rows, f), BF16),
        compiler_params=_cparams(("arbitrary", "arbitrary")),
    )(tile_expert, n_used, xs, wg, wu, bg.reshape(e, 1, f), bu.reshape(e, 1, f))


def _moe_down(tile_expert, n_used, hs, wd, bdn, tn):
    n_rows, f = hs.shape
    e, _, d = wd.shape
    tm = MOE_TILE
    tn = _tile(d, tn)
    return pl.pallas_call(
        _moe_down_kernel,
        grid_spec=pltpu.PrefetchScalarGridSpec(
            num_scalar_prefetch=2,
            grid=(d // tn, n_rows // tm),
            in_specs=[pl.BlockSpec((tm, f), lambda j, i, te, nu: (i, 0)),
                      pl.BlockSpec((None, f, tn), lambda j, i, te, nu: (te[i], 0, j)),
                      pl.BlockSpec((None, 1, tn), lambda j, i, te, nu: (te[i], 0, j))],
            out_specs=pl.BlockSpec((tm, tn), lambda j, i, te, nu: (i, j)),
            scratch_shapes=[pltpu.VMEM((f, tn), BF16)]),
        out_shape=jax.ShapeDtypeStruct((n_rows, d), BF16),
        compiler_params=_cparams(("arbitrary", "arbitrary")),
    )(tile_expert, n_used, hs, wd, bdn.reshape(e, 1, d))


def _moe(h2, logits, wg, bg, wu, bu, wd, bdn):
    n_tok, d = h2.shape
    e = logits.shape[1]
    tm = MOE_TILE
    top_logits, top_idx = lax.top_k(logits, TOP_K)
    gates = jax.nn.softmax(top_logits, axis=-1)
    n_asg = n_tok * TOP_K
    e_flat = top_idx.reshape(n_asg).astype(jnp.int32)
    asg = jnp.arange(n_asg, dtype=jnp.int32)
    e_sorted, order = lax.sort((e_flat, asg), num_keys=1, is_stable=True)
    counts = jnp.sum((e_flat[:, None] == jnp.arange(e, dtype=jnp.int32)[None, :]).astype(jnp.int32), axis=0)
    raw_start = jnp.cumsum(counts) - counts
    padded = (counts + tm - 1) // tm * tm
    pad_end = jnp.cumsum(padded)
    pad_start = pad_end - padded
    dest_sorted = (pad_start - raw_start)[e_sorted] + asg
    _, dest = lax.sort((order, dest_sorted), num_keys=1)
    n_tiles = -(-(n_asg + e * (tm - 1)) // tm)
    n_rows = n_tiles * tm
    tile_expert = jnp.minimum(
        jnp.searchsorted(pad_end, jnp.arange(n_tiles, dtype=jnp.int32) * tm, side='right'), e - 1).astype(jnp.int32)
    n_used = (pad_end[-1] // tm).astype(jnp.int32).reshape(1)
    row = jnp.arange(n_rows, dtype=jnp.int32)
    row_e = jnp.repeat(tile_expert, tm)
    rank = row - pad_start[row_e]
    src = jnp.clip(raw_start[row_e] + rank, 0, n_asg - 1)
    row_tok = jnp.where(rank < counts[row_e], order[src] // TOP_K, 0)

    xs = h2[row_tok]
    hs = _moe_up(tile_expert, n_used, xs, wg, wu, bg, bu, 512)
    ys = _moe_down(tile_expert, n_used, hs, wd, bdn, 1024)
    picked = ys[dest.reshape(n_tok, TOP_K).T]
    return picked, gates


def _combine_kernel(y_ref, g_ref, x1_ref, gate_ref, fg_ref, o_ref):
    g = g_ref[...]
    acc = y_ref[0].astype(F32) * g[:, 0:1]
    for k in range(1, y_ref.shape[0]):
        acc = acc + y_ref[k].astype(F32) * g[:, k:k + 1]
    xo = x1_ref[...] + gate_ref[...] * acc
    o_ref[...] = xo * lax.rsqrt(jnp.mean(xo * xo, axis=-1, keepdims=True) + NORM_EPS) * fg_ref[...]


def _combine(picked, gates, x1, gate, final_g, seq, tm):
    nk, m, d = picked.shape
    tm = _tile(seq, tm)
    return pl.pallas_call(
        _combine_kernel,
        grid=(m // tm,),
        in_specs=[pl.BlockSpec((nk, tm, d), lambda i: (0, i, 0)),
                  pl.BlockSpec((tm, nk), lambda i: (i, 0)),
                  pl.BlockSpec((tm, d), lambda i: (i, 0)),
                  pl.BlockSpec((None, 1, d), lambda i: ((i * tm) // seq, 0, 0)),
                  pl.BlockSpec((1, d), lambda i: (0, 0))],
        out_specs=pl.BlockSpec((tm, d), lambda i: (i, 0)),
        out_shape=jax.ShapeDtypeStruct((m, d), F32),
        compiler_params=_cparams(("parallel",)),
    )(picked, gates, x1, gate, final_g)


def _rmsnorm(x, g):
    return x * lax.rsqrt(jnp.mean(x * x, axis=-1, keepdims=True) + NORM_EPS) * g


def kernel(x, c, ctx, c_ctx, ada_w, ada_b, norm1_g, norm2_g, in_w, conv_w, conv_b, hy_w1, hy_b1, hy_w2, hy_b2, hy_w3, hy_b3, hy_freq, hy_w4, hy_bias, hy_norm_g, rw_w0, rw_w1, rw_w2, rw_a0, rw_a1, rw_a2, rw_kk, rw_ka, rw_rk, rw_g1, rw_g2, rw_lnx_g, rw_lnx_b, out_w, router_w, router_b, ex_w_gate, ex_b_gate, ex_w_up, ex_b_up, ex_w_down, ex_b_down, final_g):
    assert ada_w.shape[0] == 1, "single-layer block: context outputs never reach a latent token"
    nb, seq, d = x.shape
    hy = hy_bias.shape[1]
    rw = rw_kk.shape[1]
    n_hy = 3 * hy
    proj = in_w.shape[2]
    n_exp = router_w.shape[2]

    cond = jnp.concatenate([jax.nn.silu(c), jax.nn.silu(c_ctx)[None]], axis=0)
    mod = _matmul(cond, ada_w[0], F32, cond.shape[0], 1024) + ada_b[0]
    mod_x = [m[:, None, :] for m in jnp.split(mod[:nb], 6, axis=-1)]
    mod_c = [m[:, None, :] for m in jnp.split(mod[nb:], 6, axis=-1)]

    lora_w = jnp.concatenate([rw_w1[0, 0], rw_w1[0, 1], rw_a1[0, 0], rw_a1[0, 1], rw_g1[0]], axis=1)
    n_lora = lora_w.shape[1]
    nz = -(-(proj + n_lora) // 512) * 512
    pad = nz - proj - n_lora
    w_all = jnp.concatenate([in_w[0], lora_w, jnp.zeros((d, pad), F32)], axis=1).astype(BF16)
    pass_taps = jnp.concatenate([jnp.zeros((1, n_lora + pad), F32), jnp.ones((1, n_lora + pad), F32),
                                 jnp.zeros((1, n_lora + pad), F32)], axis=0)
    cw_all = jnp.concatenate([conv_w[0], pass_taps], axis=1)
    cb_all = jnp.concatenate([conv_b[0], jnp.zeros((n_lora + pad,), F32)])[None]
    g1 = norm1_g[0][None]
    zx = _inproj(x, mod_x[0], mod_x[1], g1, w_all, cw_all, cb_all, 1024, 512)
    zc = _inproj(ctx, mod_c[0], mod_c[1], g1, w_all, cw_all, cb_all, 256, 512)

    h_fwd, h_bwd = _hyena_filter(seq, hy_w1[0], hy_b1[0], hy_w2[0], hy_b2[0], hy_w3[0], hy_b3[0],
                                 hy_freq[0], hy_w4[0])
    fwd, inv = _dft_matrices(seq)
    taps = _taps_spectrum(fwd, h_fwd, h_bwd)
    spec = _dft_fwd(fwd, zx, hy, 1024, 512)
    y_hy = _dft_inv(inv, spec, taps, zx, hy_bias[0][None], hy_norm_g[0][None], 1024, 512)

    scan_args = (n_hy, proj, rw_w2[0].astype(BF16), rw_a2[0].astype(BF16), rw_w0[0], rw_a0[0],
                 rw_kk[0][None], rw_ka[0][None], rw_rk[0].reshape(1, rw))
    s0 = jnp.zeros((2, nb, rw // PACK_W, PACK_W, PACK_W), F32)
    _, _, s_ctx = _rwkv_scan(zc, *scan_args, s0)
    ys, bonus, _ = _rwkv_scan(zx, *scan_args, s_ctx)

    ne = -(-n_exp // LANE) * LANE
    rw_pad = jnp.pad(router_w[0], ((0, 0), (0, ne - n_exp)))
    rw_hi = rw_pad.astype(BF16)
    rw_lo = (rw_pad - rw_hi.astype(F32)).astype(BF16)
    rb = jnp.pad(router_b[0], (0, ne - n_exp))[None]
    x1, h2, logits = _outproj(y_hy, ys, bonus, zx, proj + n_lora - rw_g1.shape[-1], rw_g2[0].astype(BF16),
                              rw_lnx_g[0][None], rw_lnx_b[0][None], out_w[0].astype(BF16), x,
                              mod_x[2], norm2_g[0][None], mod_x[3], mod_x[4], rw_hi, rw_lo, rb, 256)

    picked, gates = _moe(h2, logits[:, :n_exp], ex_w_gate[0], ex_b_gate[0], ex_w_up[0], ex_b_up[0],
                         ex_w_down[0], ex_b_down[0])
    return _combine(picked, gates, x1, mod_x[5], final_g[None], seq, 256).reshape(nb, seq, d)
```

```python
import functools
import math

import jax
import jax.numpy as jnp
from jax import lax
from jax.experimental import pallas as pl
from jax.experimental.pallas import tpu as pltpu

F32 = jnp.float32
BF16 = jnp.bfloat16

HEAD = 64
CHUNK = 64
PACK = 4
PACK_W = PACK * HEAD
SCAN_BATCH = 4
HALO = 16
INPROJ_ROWS = 256
FILTER_BANDS = 16
FILTER_TARGET = 1e-2
FAST_DECAY_PCT = 0.3
SLOW_DECAY_PCT = 1.5
TOP_K = 4
SWIGLU_LIMIT = 7.0
SWIGLU_ALPHA = 1.702
NORM_EPS = 1e-6
LNX_EPS = 64e-5
MOE_TILE = 512
MOE_PARTS = 4
LANE = 128
VMEM_LIMIT = 56 * 1024 * 1024


def _cparams(sem):
    return pltpu.CompilerParams(dimension_semantics=sem, vmem_limit_bytes=VMEM_LIMIT)


def _tile(n, want):
    if n <= want:
        return n
    t = want
    while n % t:
        t //= 2
    assert t >= 8, (n, want)
    return t


def _dot(a, b):
    return jnp.dot(a.astype(BF16), b.astype(BF16), preferred_element_type=F32)


def _group_ones(width):
    r = lax.broadcasted_iota(jnp.int32, (width, width), 0) // HEAD
    c = lax.broadcasted_iota(jnp.int32, (width, width), 1) // HEAD
    return jnp.where(r == c, 1.0, 0.0).astype(BF16)


def _group_sum(x, ones):
    width = ones.shape[0]
    parts = [_dot(x[:, o:o + width], ones) for o in range(0, x.shape[1], width)]
    return parts[0] if len(parts) == 1 else jnp.concatenate(parts, axis=1)


def _mm_kernel(a_ref, b_ref, o_ref):
    o_ref[...] = _dot(a_ref[...], b_ref[...]).astype(o_ref.dtype)


def _matmul(a, b, out_dtype, tm, tn):
    m, k = a.shape
    n = b.shape[1]
    tm, tn = _tile(m, tm), _tile(n, tn)
    return pl.pallas_call(
        _mm_kernel,
        grid=(n // tn, m // tm),
        in_specs=[pl.BlockSpec((tm, k), lambda j, i: (i, 0)),
                  pl.BlockSpec((k, tn), lambda j, i: (0, j))],
        out_specs=pl.BlockSpec((tm, tn), lambda j, i: (i, j)),
        out_shape=jax.ShapeDtypeStruct((m, n), out_dtype),
        compiler_params=_cparams(("parallel", "parallel")),
    )(a, b)


def _bmm_shared_lhs(a, b, out_dtype, tm, tn):
    m, k = a.shape
    nb, _, n = b.shape
    tm, tn = _tile(m, tm), _tile(n, tn)
    return pl.pallas_call(
        _mm_kernel,
        grid=(m // tm, nb, n // tn),
        in_specs=[pl.BlockSpec((tm, k), lambda i, bb, j: (i, 0)),
                  pl.BlockSpec((None, k, tn), lambda i, bb, j: (bb, 0, j))],
        out_specs=pl.BlockSpec((None, tm, tn), lambda i, bb, j: (bb, i, j)),
        out_shape=jax.ShapeDtypeStruct((nb, m, n), out_dtype),
        compiler_params=_cparams(("parallel", "parallel", "parallel")),
    )(a, b)


def _inproj_kernel(seq, xm_ref, xp_ref, xn_ref, shift_ref, scale_ref, g_ref, w_ref, cw_ref, cb_ref,
                   z_ref, h_ref, zz_ref):
    tm = xm_ref.shape[0]
    i = pl.program_id(0)

    @pl.when(pl.program_id(1) == 0)
    def _():
        def norm(xv):
            y = xv * lax.rsqrt(jnp.mean(xv * xv, axis=-1, keepdims=True) + NORM_EPS) * g_ref[...]
            return y * (1.0 + scale_ref[...]) + shift_ref[...]
        keep_prev = jnp.where((i * tm) % seq == 0, 0.0, 1.0)
        keep_next = jnp.where(((i + 1) * tm) % seq == 0, 0.0, 1.0)
        h_ref[0:HALO] = (norm(xp_ref[...]) * keep_prev).astype(BF16)
        h_ref[HALO:HALO + tm] = norm(xm_ref[...]).astype(BF16)
        h_ref[HALO + tm:] = (norm(xn_ref[...]) * keep_next).astype(BF16)

    cw = cw_ref[...]
    n_chunks = max(1, tm // INPROJ_ROWS)
    q = tm // n_chunks
    d0 = o0 = 0
    for ci in range(n_chunks):
        last = ci + 1 == n_chunks
        d1 = tm + 2 * HALO if last else (ci + 1) * q + HALO
        o1 = tm if last else d1 - 2 * HALO
        zz_ref[d0:d1] = jnp.dot(h_ref[d0:d1], w_ref[...], preferred_element_type=F32)
        z_ref[o0:o1] = (zz_ref[HALO - 1 + o0:HALO - 1 + o1] * cw[0:1] + zz_ref[HALO + o0:HALO + o1] * cw[1:2]
                        + zz_ref[HALO + 1 + o0:HALO + 1 + o1] * cw[2:3] + cb_ref[...]).astype(z_ref.dtype)
        d0, o0 = d1, o1


def _inproj(x, shift, scale, g, w, cw, cb, tm, tn):
    nb, seq, d = x.shape
    n = w.shape[1]
    tm, tn = _tile(seq, tm), _tile(n, tn)
    m = nb * seq
    n_halo = m // HALO
    per_batch = shift.shape[0] > 1
    mod_spec = pl.BlockSpec((None, 1, d), (lambda i, j: ((i * tm) // seq, 0, 0)) if per_batch
                            else (lambda i, j: (0, 0, 0)))
    x2 = x.reshape(m, d)
    out = pl.pallas_call(
        functools.partial(_inproj_kernel, seq),
        grid=(m // tm, n // tn),
        in_specs=[pl.BlockSpec((tm, d), lambda i, j: (i, 0)),
                  pl.BlockSpec((HALO, d), lambda i, j: (jnp.maximum(i * (tm // HALO) - 1, 0), 0)),
                  pl.BlockSpec((HALO, d), lambda i, j: (jnp.minimum((i + 1) * (tm // HALO), n_halo - 1), 0)),
                  mod_spec, mod_spec,
                  pl.BlockSpec((1, d), lambda i, j: (0, 0)),
                  pl.BlockSpec((d, tn), lambda i, j: (0, j)),
                  pl.BlockSpec((3, tn), lambda i, j: (0, j)),
                  pl.BlockSpec((1, tn), lambda i, j: (0, j))],
        out_specs=pl.BlockSpec((tm, tn), lambda i, j: (i, j)),
        out_shape=jax.ShapeDtypeStruct((m, n), BF16),
        scratch_shapes=[pltpu.VMEM((tm + 2 * HALO, d), BF16), pltpu.VMEM((tm + 2 * HALO, tn), F32)],
        compiler_params=_cparams(("parallel", "arbitrary")),
    )(x2, x2, x2, shift, scale, g, w, cw, cb)
    return out.reshape(nb, seq, n)


def _dft_matrices(length):
    n = 2 * length
    f = lax.broadcasted_iota(jnp.int32, (length, length), 0)
    t = lax.broadcasted_iota(jnp.int32, (length, length), 1)
    ang = ((f * t) % n).astype(F32) * (2.0 * math.pi / n)
    cos, sin = jnp.cos(ang), jnp.sin(ang)
    nyq = jnp.where(t % 2 == 0, 1.0, -1.0)
    fwd = jnp.concatenate([cos, jnp.where(f == 0, nyq, -sin)], axis=0)
    f_t = f.T
    inv = jnp.concatenate([jnp.where(f_t == 0, 1.0, 2.0 * cos.T),
                           jnp.where(f_t == 0, nyq.T, -2.0 * sin.T)], axis=1)
    return fwd.astype(BF16), inv.astype(BF16)


def _dft_fwd_kernel(f_ref, x1_ref, v_ref, o_ref, u_ref):
    @pl.when(pl.program_id(2) == 0)
    def _():
        u_ref[...] = (x1_ref[...].astype(F32) * v_ref[...].astype(F32)).astype(BF16)

    o_ref[...] = jnp.dot(f_ref[...], u_ref[...], preferred_element_type=F32).astype(o_ref.dtype)


def _dft_fwd(fwd, z, hy, tm, tn):
    nb, seq, _ = z.shape
    tm, tn = _tile(2 * seq, tm), _tile(hy, tn)
    nj = hy // tn
    return pl.pallas_call(
        _dft_fwd_kernel,
        grid=(nb, nj, 2 * seq // tm),
        in_specs=[pl.BlockSpec((tm, seq), lambda b, j, i: (i, 0)),
                  pl.BlockSpec((None, seq, tn), lambda b, j, i: (b, 0, nj + j)),
                  pl.BlockSpec((None, seq, tn), lambda b, j, i: (b, 0, 2 * nj + j))],
        out_specs=pl.BlockSpec((None, tm, tn), lambda b, j, i: (b, i, j)),
        out_shape=jax.ShapeDtypeStruct((nb, 2 * seq, hy), BF16),
        scratch_shapes=[pltpu.VMEM((seq, tn), BF16)],
        compiler_params=_cparams(("parallel", "parallel", "arbitrary")),
    )(fwd, z, z)


def _dft_inv_kernel(g_ref, s_ref, t_ref, x0_ref, x1_ref, v_ref, bias_ref, ng_ref, o_ref, y_ref):
    half = s_ref.shape[0] // 2

    @pl.when(pl.program_id(2) == 0)
    def _():
        re = s_ref[:half].astype(F32)
        im = s_ref[half:].astype(F32)
        t_re = t_ref[:half].astype(F32)
        t_im = t_ref[half:].astype(F32)
        row0 = lax.broadcasted_iota(jnp.int32, (half, 1), 0) == 0
        y_ref[:half] = (re * t_re - jnp.where(row0, 0.0, im * t_im)).astype(BF16)
        y_ref[half:] = (im * jnp.where(row0, t_im, t_re) + jnp.where(row0, 0.0, re * t_im)).astype(BF16)

    conv = jnp.dot(g_ref[...], y_ref[...], preferred_element_type=F32)
    u = x1_ref[...].astype(F32) * v_ref[...].astype(F32)
    y = x0_ref[...].astype(F32) * (conv + bias_ref[...] * u)
    ms = _group_sum(y * y, _group_ones(min(y.shape[1], PACK_W))) * (1.0 / HEAD)
    o_ref[...] = (y * lax.rsqrt(ms + NORM_EPS) * ng_ref[...]).astype(o_ref.dtype)


def _dft_inv(inv, spec, taps, z, bias, norm_g, tm, tn):
    nb, seq, _ = z.shape
    hy = spec.shape[2]
    tm, tn = _tile(seq, tm), _tile(hy, tn)
    nj = hy // tn
    row = lambda k: pl.BlockSpec((None, tm, tn), lambda b, j, i: (b, i, k * nj + j))
    vec = pl.BlockSpec((1, tn), lambda b, j, i: (0, j))
    return pl.pallas_call(
        _dft_inv_kernel,
        grid=(nb, nj, seq // tm),
        in_specs=[pl.BlockSpec((tm, 2 * seq), lambda b, j, i: (i, 0)),
                  pl.BlockSpec((None, 2 * seq, tn), lambda b, j, i: (b, 0, j)),
                  pl.BlockSpec((2 * seq, tn), lambda b, j, i: (0, j)),
                  row(0), row(1), row(2), vec, vec],
        out_specs=pl.BlockSpec((None, tm, tn), lambda b, j, i: (b, i, j)),
        out_shape=jax.ShapeDtypeStruct((nb, seq, hy), BF16),
        scratch_shapes=[pltpu.VMEM((2 * seq, tn), BF16)],
        compiler_params=_cparams(("parallel", "parallel", "arbitrary")),
    )(inv, spec, taps, z, z, z, bias, norm_g)


def _hyena_filter(length, w1, b1, w2, b2, w3, b3, freq, w4):
    hy = w4.shape[1] // 2
    t = jnp.linspace(0.0, 1.0, length, dtype=F32)[:, None]
    ang = (2.0 * math.pi / length) * jnp.arange(length, dtype=F32)[:, None]
    bands = jnp.linspace(1e-4, FILTER_BANDS - 1, FILTER_BANDS, dtype=F32)[None, :]
    feats = jnp.concatenate([t, jnp.cos(bands * ang), -jnp.sin(bands * ang)], axis=-1)
    hp = lax.Precision.HIGHEST
    h = jnp.sin(freq * (jnp.dot(feats, w1, precision=hp) + b1))
    h = jnp.sin(freq * (jnp.dot(h, w2, precision=hp) + b2))
    h = jnp.sin(freq * (jnp.dot(h, w3, precision=hp) + b3))
    h = jnp.dot(h, w4, precision=hp)
    deltas = jnp.abs(jnp.linspace(math.log(FILTER_TARGET) / SLOW_DECAY_PCT,
                                  math.log(FILTER_TARGET) / FAST_DECAY_PCT, hy, dtype=F32))
    h = h * jnp.exp(-t * jnp.tile(deltas, 2))
    return h[:, :hy], h[:, hy:]


def _taps_spectrum(fwd, h_fwd, h_bwd):
    length = h_fwd.shape[0]
    sig = jnp.stack([h_fwd, h_bwd.at[0].set(0.0)]).astype(BF16)
    spec = _bmm_shared_lhs(fwd, sig, F32, 1024, 1024)
    re, im = spec[:, :length], spec[:, length:]
    scale = 1.0 / (2 * length)
    row0 = lax.broadcasted_iota(jnp.int32, (length, 1), 0) == 0
    t_re = (re[0] + re[1]) * scale
    t_im = jnp.where(row0, im[0] + im[1], im[0] - im[1]) * scale
    return jnp.concatenate([t_re, t_im], axis=0).astype(BF16)


def _scan_kernel(r_ref, k_ref, v_ref, lw_ref, la_ref, w2_ref, a2_ref, w0_ref, a0_ref, kks_ref, ka_ref, rk_ref,
                 s0_ref, y_ref, bonus_ref, sfin_ref, ht_ref):
    c = CHUNK
    w = PACK_W
    direction = pl.program_id(0)
    step = pl.program_id(2)
    n_batch = r_ref.shape[0]
    n_groups = r_ref.shape[2] // w
    lora = w2_ref.shape[0]

    @pl.when(step == 0)
    def _():
        ht_ref[...] = s0_ref[...]

    sign = 1 - 2 * direction
    d_sq = (lax.broadcasted_iota(jnp.int32, (c, c), 1) - lax.broadcasted_iota(jnp.int32, (c, c), 0)) * sign
    incl_sq = jnp.where(d_sq <= 0, 1.0, 0.0).astype(BF16)
    d_c = (lax.broadcasted_iota(jnp.int32, (c, w), 1) % c - lax.broadcasted_iota(jnp.int32, (c, w), 0)) * sign
    strict_c = d_c < 0
    incl_c = d_c <= 0
    eye_c = jnp.where(d_c == 0, 1.0, 0.0)
    bd_mask = (lax.broadcasted_iota(jnp.int32, (w, w), 0) // c) == (lax.broadcasted_iota(jnp.int32, (w, w), 1) // HEAD)
    ones_bd = jnp.where(bd_mask, 1.0, 0.0).astype(BF16)

    def bd(x):
        xb = x.astype(BF16)
        return jnp.where(bd_mask, jnp.concatenate([xb] * PACK, axis=0), jnp.zeros((), BF16))

    def mm_nt(a, b):
        return lax.dot_general(a.astype(BF16), b.astype(BF16), (((1,), (1,)), ((), ())),
                               preferred_element_type=F32)

    fwd = direction == 0
    incl_2 = jnp.concatenate([incl_sq, incl_sq], axis=1)
    prep = []
    for bi in range(n_batch):
        r_all = r_ref[bi].astype(F32)
        k_all = k_ref[bi].astype(F32)
        v_all = v_ref[bi].astype(F32)
        lw_pre = lw_ref[bi].astype(F32)
        la_pre = la_ref[bi].astype(F32)
        lw_pre = jnp.where(fwd, lw_pre[:, :lora], lw_pre[:, lora:])
        la_pre = jnp.where(fwd, la_pre[:, :lora], la_pre[:, lora:])
        x_w = w0_ref[...] + _dot(jnp.tanh(lw_pre), w2_ref[...])
        lw_all = -math.exp(-0.5) * jax.nn.sigmoid(x_w)
        a_sig = jax.nn.sigmoid(a0_ref[...] + _dot(la_pre, a2_ref[...]))
        kk = k_all * kks_ref[...]
        kk = kk * lax.rsqrt(jnp.maximum(_group_sum(kk * kk, ones_bd), 1e-24))
        kd_all = k_all * (1.0 + (a_sig - 1.0) * ka_ref[...])
        bb_all = kk * a_sig
        bonus_ref[bi] = (_group_sum(r_all * kd_all * rk_ref[...], ones_bd) * v_all).astype(bonus_ref.dtype)

        lw_hi = lw_all.astype(BF16)
        lw_lo = (lw_all - lw_hi.astype(F32)).astype(BF16)
        cum = jnp.dot(incl_2, jnp.concatenate([lw_hi, lw_lo], axis=0), preferred_element_type=F32)
        tot = jnp.sum(lw_all, axis=0, keepdims=True)
        g_inv = jnp.exp(-cum)
        g_rem = jnp.exp(tot - cum)
        prep.append(dict(v=v_all, g_tot=jnp.exp(tot), a_t=-kk * jnp.exp(cum - lw_all), r_t=r_all * jnp.exp(cum),
                         b_t=bb_all * g_inv, k_t=kd_all * g_inv, b_s=bb_all * g_rem, k_s=kd_all * g_rem))

    probs = [(bi, g) for bi in range(n_batch) for g in range(n_groups)]
    idx = range(len(probs))
    cols = [slice(g * w, (g + 1) * w) for _, g in probs]
    take = lambda name: [prep[bi][name][:, cols[q]] for q, (bi, _) in enumerate(probs)]
    v_g, a_t, r_t, b_t, k_t, b_s, k_s = (take(nm) for nm in ("v", "a_t", "r_t", "b_t", "k_t", "b_s", "k_s"))
    hts = [ht_ref[bi, g] for bi, g in probs]
    lhs = [jnp.concatenate([a_t[q], r_t[q]], axis=0) for q in idx]
    scores = [mm_nt(lhs[q], jnp.concatenate([bd(b_t[q]), bd(k_t[q])], axis=0)) for q in idx]
    h0 = [mm_nt(lhs[q], hts[q]) for q in idx]
    a_ab = [jnp.where(strict_c, s[:c, :w], 0.0) for s in scores]
    a_ak = [jnp.where(strict_c, s[:c, w:], 0.0) for s in scores]
    a_rb = [jnp.where(incl_c, s[c:, :w], 0.0) for s in scores]
    a_rk = [jnp.where(incl_c, s[c:, w:], 0.0) for s in scores]
    bd_v = [bd(vq) for vq in v_g]
    x = [h0[q][:c] + _dot(a_ak[q], bd_v[q]) for q in idx]

    t_inv = [eye_c + a for a in a_ab]
    p = [_dot(a, bd(a)) for a in a_ab]
    n_sq = int(math.log2(c)) - 1
    for it in range(n_sq):
        bd_p = [bd(pq) for pq in p]
        if it + 1 < n_sq:
            pt = [_dot(jnp.concatenate([p[q], t_inv[q]], axis=0), bd_p[q]) for q in idx]
            p = [m[:c] for m in pt]
            t_inv = [t_inv[q] + pt[q][c:] for q in idx]
        else:
            t_inv = [t_inv[q] + _dot(t_inv[q], bd_p[q]) for q in idx]
    u = [_dot(t_inv[q], bd(x[q])) for q in idx]

    y = [h0[q][c:] + _dot(jnp.concatenate([a_rb[q], a_rk[q]], axis=1),
                          jnp.concatenate([bd(u[q]), bd_v[q]], axis=0)) for q in idx]
    for q, (bi, _) in enumerate(probs):
        y_ref[bi, :, cols[q]] = y[q].astype(y_ref.dtype)

    for q, (bi, g) in enumerate(probs):
        uv = jnp.concatenate([u[q], v_g[q]], axis=0).astype(BF16)
        bk = jnp.concatenate([b_s[q], k_s[q]], axis=0).astype(BF16)
        upd = lax.dot_general(uv, bk, (((0,), (0,)), ((), ())), preferred_element_type=F32)
        ht_ref[bi, g] = hts[q] * prep[bi]["g_tot"][:, cols[q]] + jnp.where(bd_mask, upd, 0.0)

    @pl.when(step == pl.num_programs(2) - 1)
    def _():
        sfin_ref[...] = ht_ref[...]


def _rwkv_scan(z, col0, lora_col0, w2, a2, w0, a0, kks, ka, rk, s0):
    nb, seq, _ = z.shape
    lora, ch = w2.shape[1:]
    c = CHUNK
    assert CHUNK == HEAD and ch % PACK_W == 0 and seq % c == 0 and 2 * lora == LANE
    assert col0 % ch == 0 and lora_col0 % LANE == 0
    nc = seq // c
    ng = ch // PACK_W
    bs = SCAN_BATCH if nb % SCAN_BATCH == 0 else 1
    chunk_of = lambda n, i: jnp.where(n == 0, i, nc - 1 - i)
    zcol = lambda k: pl.BlockSpec((bs, c, ch), lambda n, b, i: (b, chunk_of(n, i), col0 // ch + k))
    zlora = lambda k: pl.BlockSpec((bs, c, LANE), lambda n, b, i: (b, chunk_of(n, i), lora_col0 // LANE + k))
    per_dir_w = pl.BlockSpec((None, lora, ch), lambda n, b, i: (n, 0, 0))
    per_dir_v = pl.BlockSpec((None, 1, ch), lambda n, b, i: (n, 0, 0))
    shared_v = pl.BlockSpec((1, ch), lambda n, b, i: (0, 0))
    state = pl.BlockSpec((None, bs, ng, PACK_W, PACK_W), lambda n, b, i: (n, b, 0, 0, 0))
    out = pl.BlockSpec((None, bs, c, ch), lambda n, b, i: (n, b, chunk_of(n, i), 0))
    return pl.pallas_call(
        _scan_kernel,
        grid=(2, nb // bs, nc),
        in_specs=[zcol(0), zcol(1), zcol(2), zlora(0), zlora(1), per_dir_w, per_dir_w, per_dir_v, per_dir_v,
                  shared_v, shared_v, shared_v, state],
        out_specs=[out, out, state],
        out_shape=[jax.ShapeDtypeStruct((2, nb, seq, ch), BF16), jax.ShapeDtypeStruct((2, nb, seq, ch), BF16),
                   jax.ShapeDtypeStruct((2, nb, ng, PACK_W, PACK_W), F32)],
        scratch_shapes=[pltpu.VMEM((bs, ng, PACK_W, PACK_W), F32)],
        compiler_params=_cparams(("parallel", "parallel", "arbitrary")),
    )(z, z, z, z, z, w2, a2, w0.reshape(2, 1, ch), a0.reshape(2, 1, ch), kks, ka, rk, s0)


def _outproj_kernel(yh_ref, ys0_ref, ys1_ref, b0_ref, b1_ref, lg_ref, g2_ref, lng_ref, lnb_ref, ow_ref,
                    x_ref, gate_ref, n2g_ref, shift_ref, scale_ref, rwh_ref, rwl_ref, rb_ref,
                    x1_ref, h2_ref, lg_out_ref, mix_ref):
    hy = yh_ref.shape[1]
    ones = _group_ones(PACK_W)
    y = ys0_ref[...].astype(F32) + ys1_ref[...].astype(F32)
    mu = _group_sum(y, ones) * (1.0 / HEAD)
    yc = y - mu
    var = _group_sum(yc * yc, ones) * (1.0 / HEAD)
    y = yc * lax.rsqrt(var + LNX_EPS) * lng_ref[...] + lnb_ref[...]
    gate = _dot(jax.nn.sigmoid(lg_ref[...].astype(F32)), g2_ref[...])
    y = (y + b0_ref[...].astype(F32) + b1_ref[...].astype(F32)) * gate
    mix_ref[:, :hy] = yh_ref[...]
    mix_ref[:, hy:] = y.astype(BF16)
    mix = jnp.dot(mix_ref[...], ow_ref[...], preferred_element_type=F32)
    x1 = x_ref[...] + gate_ref[...] * mix
    x1_ref[...] = x1
    h2 = x1 * lax.rsqrt(jnp.mean(x1 * x1, axis=-1, keepdims=True) + NORM_EPS) * n2g_ref[...]
    h2 = h2 * (1.0 + scale_ref[...]) + shift_ref[...]
    h2_ref[...] = h2.astype(BF16)
    h_hi = h2.astype(BF16)
    h_lo = (h2 - h_hi.astype(F32)).astype(BF16)
    logits = (jnp.dot(h_hi, rwh_ref[...], preferred_element_type=F32)
              + jnp.dot(h_hi, rwl_ref[...], preferred_element_type=F32)
              + jnp.dot(h_lo, rwh_ref[...], preferred_element_type=F32))
    lg_out_ref[...] = logits + rb_ref[...]


def _outproj(y_hy, ys, bonus, z, lora_g_col, g2, lnx_g, lnx_b, out_w, x, gate, n2g, shift, scale,
             rw_hi, rw_lo, rb, tm):
    nb, seq, d = x.shape
    m = nb * seq
    hy = y_hy.shape[2]
    rw = ys.shape[3]
    tm = _tile(seq, tm)
    lg = g2.shape[0]
    ne = rw_hi.shape[1]
    assert lora_g_col % lg == 0
    ys2 = ys.reshape(2, m, rw)
    bn2 = bonus.reshape(2, m, rw)
    nz = z.shape[2]
    full = lambda r, c: pl.BlockSpec((r, c), lambda i: (0, 0))
    dir_spec = lambda n: pl.BlockSpec((None, tm, rw), lambda i: (n, i, 0))
    mod_spec = pl.BlockSpec((None, 1, d), lambda i: ((i * tm) // seq, 0, 0))
    return pl.pallas_call(
        _outproj_kernel,
        grid=(m // tm,),
        in_specs=[pl.BlockSpec((tm, hy), lambda i: (i, 0)), dir_spec(0), dir_spec(1), dir_spec(0), dir_spec(1),
                  pl.BlockSpec((tm, lg), lambda i: (i, lora_g_col // lg)),
                  full(lg, rw), full(1, rw), full(1, rw), full(hy + rw, d),
                  pl.BlockSpec((tm, d), lambda i: (i, 0)), mod_spec, full(1, d), mod_spec, mod_spec,
                  full(d, ne), full(d, ne), full(1, ne)],
        out_specs=[pl.BlockSpec((tm, d), lambda i: (i, 0)), pl.BlockSpec((tm, d), lambda i: (i, 0)),
                   pl.BlockSpec((tm, ne), lambda i: (i, 0))],
        out_shape=[jax.ShapeDtypeStruct((m, d), F32), jax.ShapeDtypeStruct((m, d), BF16),
                   jax.ShapeDtypeStruct((m, ne), F32)],
        scratch_shapes=[pltpu.VMEM((tm, hy + rw), BF16)],
        compiler_params=_cparams(("parallel",)),
    )(y_hy.reshape(m, hy), ys2, ys2, bn2, bn2, z.reshape(m, nz), g2, lnx_g, lnx_b, out_w,
      x.reshape(m, d), gate, n2g, shift, scale, rw_hi, rw_lo, rb)


def _expert_changed(te_ref, i):
    prev = te_ref[jnp.maximum(i - 1, 0)]
    return jnp.logical_or(i == 0, te_ref[i] != prev)


def _moe_up_kernel(te_ref, nu_ref, x_ref, wg_ref, wu_ref, bg_ref, bu_ref, h_ref, wg_bf, wu_bf):
    i = pl.program_id(1)

    @pl.when(_expert_changed(te_ref, i))
    def _():
        wg_bf[...] = wg_ref[...].astype(BF16)
        wu_bf[...] = wu_ref[...].astype(BF16)

    @pl.when(i < nu_ref[0])
    def _():
        x = x_ref[...]
        g = jnp.dot(x, wg_bf[...], preferred_element_type=F32) + bg_ref[...]
        u = jnp.dot(x, wu_bf[...], preferred_element_type=F32) + bu_ref[...]
        g = jnp.minimum(g, SWIGLU_LIMIT)
        u = jnp.clip(u, -SWIGLU_LIMIT, SWIGLU_LIMIT)
        h_ref[...] = ((u + 1.0) * (g * jax.nn.sigmoid(SWIGLU_ALPHA * g))).astype(h_ref.dtype)

    @pl.when(i >= nu_ref[0])
    def _():
        h_ref[...] = jnp.zeros_like(h_ref)


def _moe_down_kernel(te_ref, nu_ref, h_ref, wd_ref, bd_ref, *rest):
    y_ref, wd_bf = rest[-2:]
    i = pl.program_id(1)

    @pl.when(_expert_changed(te_ref, i))
    def _():
        wd_bf[...] = wd_ref[...].astype(BF16)

    @pl.when(i < nu_ref[0])
    def _():
        y = jnp.dot(h_ref[...], wd_bf[...], preferred_element_type=F32) + bd_ref[...]
        y_ref[...] = y.astype(y_ref.dtype)

    @pl.when(i >= nu_ref[0])
    def _():
        y_ref[...] = jnp.zeros_like(y_ref)


def _moe_up(tile_expert, n_used, xs, wg, wu, bg, bu, tn):
    n_rows, d = xs.shape
    e, _, f = wg.shape
    tm = MOE_TILE
    tn = _tile(f, tn)
    w_spec = pl.BlockSpec((None, d, tn), lambda j, i, te, nu: (te[i], 0, j))
    b_spec = pl.BlockSpec((None, 1, tn), lambda j, i, te, nu: (te[i], 0, j))
    return pl.pallas_call(
        _moe_up_kernel,
        grid_spec=pltpu.PrefetchScalarGridSpec(
            num_scalar_prefetch=2,
            grid=(f // tn, n_rows // tm),
            in_specs=[pl.BlockSpec((tm, d), lambda j, i, te, nu: (i, 0)), w_spec, w_spec, b_spec, b_spec],
            out_specs=pl.BlockSpec((tm, tn), lambda j, i, te, nu: (i, j)),
            scratch_shapes=[pltpu.VMEM((d, tn), BF16), pltpu.VMEM((d, tn), BF16)]),
        out_shape=jax.ShapeDtypeStruct((n_rows, f), BF16),
        compiler_params=_cparams(("arbitrary", "arbitrary")),
    )(tile_expert, n_used, xs, wg, wu, bg.reshape(e, 1, f), bu.reshape(e, 1, f))


def _moe_down(tile_expert, n_used, hs, wd, bdn, tn, total_rows, tile0, ys_prev):
    n_rows, f = hs.shape
    e, _, d = wd.shape
    tm = MOE_TILE
    tn = _tile(d, tn)
    in_specs = [pl.BlockSpec((tm, f), lambda j, i, te, nu: (i, 0)),
                pl.BlockSpec((None, f, tn), lambda j, i, te, nu: (te[i], 0, j)),
                pl.BlockSpec((None, 1, tn), lambda j, i, te, nu: (te[i], 0, j))]
    args = [tile_expert, n_used, hs, wd, bdn.reshape(e, 1, d)]
    aliases = {}
    if ys_prev is not None:
        in_specs.append(pl.BlockSpec(memory_space=pl.ANY))
        args.append(ys_prev)
        aliases = {len(args) - 1: 0}
    return pl.pallas_call(
        _moe_down_kernel,
        grid_spec=pltpu.PrefetchScalarGridSpec(
            num_scalar_prefetch=2,
            grid=(d // tn, n_rows // tm),
            in_specs=in_specs,
            out_specs=pl.BlockSpec((tm, tn), lambda j, i, te, nu: (tile0 + i, j)),
            scratch_shapes=[pltpu.VMEM((f, tn), BF16)]),
        out_shape=jax.ShapeDtypeStruct((total_rows, d), BF16),
        input_output_aliases=aliases,
        compiler_params=_cparams(("arbitrary", "arbitrary")),
    )(*args)


def _moe(h2, logits, wg, bg, wu, bu, wd, bdn):
    n_tok, d = h2.shape
    e = logits.shape[1]
    tm = MOE_TILE
    top_logits, top_idx = lax.top_k(logits, TOP_K)
    gates = jax.nn.softmax(top_logits, axis=-1)
    n_asg = n_tok * TOP_K
    e_flat = top_idx.reshape(n_asg).astype(jnp.int32)
    asg = jnp.arange(n_asg, dtype=jnp.int32)
    e_sorted, order = lax.sort((e_flat, asg), num_keys=1, is_stable=True)
    counts = jnp.sum((e_flat[:, None] == jnp.arange(e, dtype=jnp.int32)[None, :]).astype(jnp.int32), axis=0)
    raw_start = jnp.cumsum(counts) - counts
    padded = (counts + tm - 1) // tm * tm
    pad_end = jnp.cumsum(padded)
    pad_start = pad_end - padded
    dest_sorted = (pad_start - raw_start)[e_sorted] + asg
    _, dest = lax.sort((order, dest_sorted), num_keys=1)
    n_tiles = -(-(n_asg + e * (tm - 1)) // tm)
    n_rows = n_tiles * tm
    tile_start = jnp.arange(n_tiles, dtype=jnp.int32) * tm
    tile_expert = jnp.minimum(jnp.sum((pad_end[None, :] <= tile_start[:, None]).astype(jnp.int32), axis=1), e - 1)
    n_used = pad_end[-1] // tm
    row = jnp.arange(n_rows, dtype=jnp.int32)
    row_e = jnp.repeat(tile_expert, tm)
    rank = row - pad_start[row_e]
    src = jnp.clip(raw_start[row_e] + rank, 0, n_asg - 1)
    row_tok = jnp.where(rank < counts[row_e], order[src] // TOP_K, 0)

    n_parts = next(p for p in (MOE_PARTS, 2, 1) if n_tiles % p == 0)
    tp = n_tiles // n_parts
    ys = None
    for q in range(n_parts):
        te_q = tile_expert[q * tp:(q + 1) * tp]
        nu_q = jnp.clip(n_used - q * tp, 0, tp).astype(jnp.int32).reshape(1)
        xs = h2[row_tok[q * tp * tm:(q + 1) * tp * tm]]
        hs = _moe_up(te_q, nu_q, xs, wg, wu, bg, bu, 512)
        ys = _moe_down(te_q, nu_q, hs, wd, bdn, 1024, n_rows, q * tp, ys)
    picked = ys[dest.reshape(n_tok, TOP_K).T]
    return picked, gates


def _combine_kernel(y_ref, g_ref, x1_ref, gate_ref, fg_ref, o_ref):
    g = g_ref[...]
    acc = y_ref[0].astype(F32) * g[:, 0:1]
    for k in range(1, y_ref.shape[0]):
        acc = acc + y_ref[k].astype(F32) * g[:, k:k + 1]
    xo = x1_ref[...] + gate_ref[...] * acc
    o_ref[...] = xo * lax.rsqrt(jnp.mean(xo * xo, axis=-1, keepdims=True) + NORM_EPS) * fg_ref[...]


def _combine(picked, gates, x1, gate, final_g, seq, tm):
    nk, m, d = picked.shape
    tm = _tile(seq, tm)
    return pl.pallas_call(
        _combine_kernel,
        grid=(m // tm,),
        in_specs=[pl.BlockSpec((nk, tm, d), lambda i: (0, i, 0)),
                  pl.BlockSpec((tm, nk), lambda i: (i, 0)),
                  pl.BlockSpec((tm, d), lambda i: (i, 0)),
                  pl.BlockSpec((None, 1, d), lambda i: ((i * tm) // seq, 0, 0)),
                  pl.BlockSpec((1, d), lambda i: (0, 0))],
        out_specs=pl.BlockSpec((tm, d), lambda i: (i, 0)),
        out_shape=jax.ShapeDtypeStruct((m, d), F32),
        compiler_params=_cparams(("parallel",)),
    )(picked, gates, x1, gate, final_g)


def _rmsnorm(x, g):
    return x * lax.rsqrt(jnp.mean(x * x, axis=-1, keepdims=True) + NORM_EPS) * g


def kernel(x, c, ctx, c_ctx, ada_w, ada_b, norm1_g, norm2_g, in_w, conv_w, conv_b, hy_w1, hy_b1, hy_w2, hy_b2, hy_w3, hy_b3, hy_freq, hy_w4, hy_bias, hy_norm_g, rw_w0, rw_w1, rw_w2, rw_a0, rw_a1, rw_a2, rw_kk, rw_ka, rw_rk, rw_g1, rw_g2, rw_lnx_g, rw_lnx_b, out_w, router_w, router_b, ex_w_gate, ex_b_gate, ex_w_up, ex_b_up, ex_w_down, ex_b_down, final_g):
    assert ada_w.shape[0] == 1, "single-layer block: context outputs never reach a latent token"
    nb, seq, d = x.shape
    hy = hy_bias.shape[1]
    rw = rw_kk.shape[1]
    n_hy = 3 * hy
    proj = in_w.shape[2]
    n_exp = router_w.shape[2]

    cond = jnp.concatenate([jax.nn.silu(c), jax.nn.silu(c_ctx)[None]], axis=0)
    mod = _matmul(cond, ada_w[0], F32, cond.shape[0], 1024) + ada_b[0]
    mod_x = [m[:, None, :] for m in jnp.split(mod[:nb], 6, axis=-1)]
    mod_c = [m[:, None, :] for m in jnp.split(mod[nb:], 6, axis=-1)]

    lora_w = jnp.concatenate([rw_w1[0, 0], rw_w1[0, 1], rw_a1[0, 0], rw_a1[0, 1], rw_g1[0]], axis=1)
    n_lora = lora_w.shape[1]
    nz = -(-(proj + n_lora) // 512) * 512
    pad = nz - proj - n_lora
    w_all = jnp.concatenate([in_w[0], lora_w, jnp.zeros((d, pad), F32)], axis=1).astype(BF16)
    pass_taps = jnp.concatenate([jnp.zeros((1, n_lora + pad), F32), jnp.ones((1, n_lora + pad), F32),
                                 jnp.zeros((1, n_lora + pad), F32)], axis=0)
    cw_all = jnp.concatenate([conv_w[0], pass_taps], axis=1)
    cb_all = jnp.concatenate([conv_b[0], jnp.zeros((n_lora + pad,), F32)])[None]
    g1 = norm1_g[0][None]
    zx = _inproj(x, mod_x[0], mod_x[1], g1, w_all, cw_all, cb_all, 1024, 512)
    zc = _inproj(ctx, mod_c[0], mod_c[1], g1, w_all[:, n_hy:], cw_all[:, n_hy:], cb_all[:, n_hy:], 256, 512)

    h_fwd, h_bwd = _hyena_filter(seq, hy_w1[0], hy_b1[0], hy_w2[0], hy_b2[0], hy_w3[0], hy_b3[0],
                                 hy_freq[0], hy_w4[0])
    fwd, inv = _dft_matrices(seq)
    taps = _taps_spectrum(fwd, h_fwd, h_bwd)
    spec = _dft_fwd(fwd, zx, hy, 1024, 512)
    y_hy = _dft_inv(inv, spec, taps, zx, hy_bias[0][None], hy_norm_g[0][None], 1024, 512)

    scan_args = (rw_w2[0].astype(BF16), rw_a2[0].astype(BF16), rw_w0[0], rw_a0[0],
                 rw_kk[0][None], rw_ka[0][None], rw_rk[0].reshape(1, rw))
    s0 = jnp.zeros((2, nb, rw // PACK_W, PACK_W, PACK_W), F32)
    _, _, s_ctx = _rwkv_scan(zc, 0, proj - n_hy, *scan_args, s0)
    ys, bonus, _ = _rwkv_scan(zx, n_hy, proj, *scan_args, s_ctx)

    ne = -(-n_exp // LANE) * LANE
    rw_pad = jnp.pad(router_w[0], ((0, 0), (0, ne - n_exp)))
    rw_hi = rw_pad.astype(BF16)
    rw_lo = (rw_pad - rw_hi.astype(F32)).astype(BF16)
    rb = jnp.pad(router_b[0], (0, ne - n_exp))[None]
    x1, h2, logits = _outproj(y_hy, ys, bonus, zx, proj + n_lora - rw_g1.shape[-1], rw_g2[0].astype(BF16),
                              rw_lnx_g[0][None], rw_lnx_b[0][None], out_w[0].astype(BF16), x,
                              mod_x[2], norm2_g[0][None], mod_x[3], mod_x[4], rw_hi, rw_lo, rb, 256)

    picked, gates = _moe(h2, logits[:, :n_exp], ex_w_gate[0], ex_b_gate[0], ex_w_up[0], ex_b_up[0],
                         ex_w_down[0], ex_b_down[0])
    return _combine(picked, gates, x1, mod_x[5], final_g[None], seq, 256).reshape(nb, seq, d)
```

```python
import functools
import math

import jax
import jax.numpy as jnp
from jax import lax
from jax.experimental import pallas as pl
from jax.experimental.pallas import tpu as pltpu

F32 = jnp.float32
BF16 = jnp.bfloat16

HEAD = 64
CHUNK = 64
PACK = 4
PACK_W = PACK * HEAD
SCAN_BATCH = 4
HALO = 16
INPROJ_ROWS = 256
FILTER_BANDS = 16
FILTER_TARGET = 1e-2
FAST_DECAY_PCT = 0.3
SLOW_DECAY_PCT = 1.5
TOP_K = 4
SWIGLU_LIMIT = 7.0
SWIGLU_ALPHA = 1.702
NORM_EPS = 1e-6
LNX_EPS = 64e-5
MOE_TILE = 512
MOE_PARTS = 4
LANE = 128
VMEM_LIMIT = 56 * 1024 * 1024


def _cparams(sem):
    return pltpu.CompilerParams(dimension_semantics=sem, vmem_limit_bytes=VMEM_LIMIT)


def _tile(n, want):
    if n <= want:
        return n
    t = want
    while n % t:
        t //= 2
    assert t >= 8, (n, want)
    return t


def _dot(a, b):
    return jnp.dot(a.astype(BF16), b.astype(BF16), preferred_element_type=F32)


def _group_ones(width):
    r = lax.broadcasted_iota(jnp.int32, (width, width), 0) // HEAD
    c = lax.broadcasted_iota(jnp.int32, (width, width), 1) // HEAD
    return jnp.where(r == c, 1.0, 0.0).astype(BF16)


def _group_sum(x, ones):
    width = ones.shape[0]
    parts = [_dot(x[:, o:o + width], ones) for o in range(0, x.shape[1], width)]
    return parts[0] if len(parts) == 1 else jnp.concatenate(parts, axis=1)


def _mm_kernel(a_ref, b_ref, o_ref):
    o_ref[...] = _dot(a_ref[...], b_ref[...]).astype(o_ref.dtype)


def _matmul(a, b, out_dtype, tm, tn):
    m, k = a.shape
    n = b.shape[1]
    tm, tn = _tile(m, tm), _tile(n, tn)
    return pl.pallas_call(
        _mm_kernel,
        grid=(n // tn, m // tm),
        in_specs=[pl.BlockSpec((tm, k), lambda j, i: (i, 0)),
                  pl.BlockSpec((k, tn), lambda j, i: (0, j))],
        out_specs=pl.BlockSpec((tm, tn), lambda j, i: (i, j)),
        out_shape=jax.ShapeDtypeStruct((m, n), out_dtype),
        compiler_params=_cparams(("parallel", "parallel")),
    )(a, b)


def _bmm_shared_lhs(a, b, out_dtype, tm, tn):
    m, k = a.shape
    nb, _, n = b.shape
    tm, tn = _tile(m, tm), _tile(n, tn)
    return pl.pallas_call(
        _mm_kernel,
        grid=(m // tm, nb, n // tn),
        in_specs=[pl.BlockSpec((tm, k), lambda i, bb, j: (i, 0)),
                  pl.BlockSpec((None, k, tn), lambda i, bb, j: (bb, 0, j))],
        out_specs=pl.BlockSpec((None, tm, tn), lambda i, bb, j: (bb, i, j)),
        out_shape=jax.ShapeDtypeStruct((nb, m, n), out_dtype),
        compiler_params=_cparams(("parallel", "parallel", "parallel")),
    )(a, b)


def _inproj_kernel(seq, xm_ref, xp_ref, xn_ref, shift_ref, scale_ref, g_ref, w_ref, cw_ref, cb_ref,
                   z_ref, h_ref, zz_ref):
    tm = xm_ref.shape[0]
    i = pl.program_id(0)

    @pl.when(pl.program_id(1) == 0)
    def _():
        def norm(xv):
            y = xv * lax.rsqrt(jnp.mean(xv * xv, axis=-1, keepdims=True) + NORM_EPS) * g_ref[...]
            return y * (1.0 + scale_ref[...]) + shift_ref[...]
        keep_prev = jnp.where((i * tm) % seq == 0, 0.0, 1.0)
        keep_next = jnp.where(((i + 1) * tm) % seq == 0, 0.0, 1.0)
        h_ref[0:HALO] = (norm(xp_ref[...]) * keep_prev).astype(BF16)
        h_ref[HALO:HALO + tm] = norm(xm_ref[...]).astype(BF16)
        h_ref[HALO + tm:] = (norm(xn_ref[...]) * keep_next).astype(BF16)

    cw = cw_ref[...]
    n_chunks = max(1, tm // INPROJ_ROWS)
    q = tm // n_chunks
    d0 = o0 = 0
    for ci in range(n_chunks):
        last = ci + 1 == n_chunks
        d1 = tm + 2 * HALO if last else (ci + 1) * q + HALO
        o1 = tm if last else d1 - 2 * HALO
        zz_ref[d0:d1] = jnp.dot(h_ref[d0:d1], w_ref[...], preferred_element_type=F32)
        z_ref[o0:o1] = (zz_ref[HALO - 1 + o0:HALO - 1 + o1] * cw[0:1] + zz_ref[HALO + o0:HALO + o1] * cw[1:2]
                        + zz_ref[HALO + 1 + o0:HALO + 1 + o1] * cw[2:3] + cb_ref[...]).astype(z_ref.dtype)
        d0, o0 = d1, o1


def _inproj(x, shift, scale, g, w, cw, cb, tm, tn):
    nb, seq, d = x.shape
    n = w.shape[1]
    tm, tn = _tile(seq, tm), _tile(n, tn)
    m = nb * seq
    n_halo = m // HALO
    per_batch = shift.shape[0] > 1
    mod_spec = pl.BlockSpec((None, 1, d), (lambda i, j: ((i * tm) // seq, 0, 0)) if per_batch
                            else (lambda i, j: (0, 0, 0)))
    x2 = x.reshape(m, d)
    out = pl.pallas_call(
        functools.partial(_inproj_kernel, seq),
        grid=(m // tm, n // tn),
        in_specs=[pl.BlockSpec((tm, d), lambda i, j: (i, 0)),
                  pl.BlockSpec((HALO, d), lambda i, j: (jnp.maximum(i * (tm // HALO) - 1, 0), 0)),
                  pl.BlockSpec((HALO, d), lambda i, j: (jnp.minimum((i + 1) * (tm // HALO), n_halo - 1), 0)),
                  mod_spec, mod_spec,
                  pl.BlockSpec((1, d), lambda i, j: (0, 0)),
                  pl.BlockSpec((d, tn), lambda i, j: (0, j)),
                  pl.BlockSpec((3, tn), lambda i, j: (0, j)),
                  pl.BlockSpec((1, tn), lambda i, j: (0, j))],
        out_specs=pl.BlockSpec((tm, tn), lambda i, j: (i, j)),
        out_shape=jax.ShapeDtypeStruct((m, n), BF16),
        scratch_shapes=[pltpu.VMEM((tm + 2 * HALO, d), BF16), pltpu.VMEM((tm + 2 * HALO, tn), F32)],
        compiler_params=_cparams(("parallel", "arbitrary")),
    )(x2, x2, x2, shift, scale, g, w, cw, cb)
    return out.reshape(nb, seq, n)


def _dft_matrices(length):
    n = 2 * length
    f = lax.broadcasted_iota(jnp.int32, (length, length), 0)
    t = lax.broadcasted_iota(jnp.int32, (length, length), 1)
    ang = ((f * t) % n).astype(F32) * (2.0 * math.pi / n)
    cos, sin = jnp.cos(ang), jnp.sin(ang)
    nyq = jnp.where(t % 2 == 0, 1.0, -1.0)
    fwd = jnp.concatenate([cos, jnp.where(f == 0, nyq, -sin)], axis=0)
    f_t = f.T
    inv = jnp.concatenate([jnp.where(f_t == 0, 1.0, 2.0 * cos.T),
                           jnp.where(f_t == 0, nyq.T, -2.0 * sin.T)], axis=1)
    return fwd.astype(BF16), inv.astype(BF16)


def _dft_fwd_kernel(f_ref, x1_ref, v_ref, o_ref, u_ref):
    @pl.when(pl.program_id(2) == 0)
    def _():
        u_ref[...] = (x1_ref[...].astype(F32) * v_ref[...].astype(F32)).astype(BF16)

    o_ref[...] = jnp.dot(f_ref[...], u_ref[...], preferred_element_type=F32).astype(o_ref.dtype)


def _dft_fwd(fwd, z, hy, tm, tn):
    nb, seq, _ = z.shape
    tm, tn = _tile(2 * seq, tm), _tile(hy, tn)
    nj = hy // tn
    return pl.pallas_call(
        _dft_fwd_kernel,
        grid=(nb, nj, 2 * seq // tm),
        in_specs=[pl.BlockSpec((tm, seq), lambda b, j, i: (i, 0)),
                  pl.BlockSpec((None, seq, tn), lambda b, j, i: (b, 0, nj + j)),
                  pl.BlockSpec((None, seq, tn), lambda b, j, i: (b, 0, 2 * nj + j))],
        out_specs=pl.BlockSpec((None, tm, tn), lambda b, j, i: (b, i, j)),
        out_shape=jax.ShapeDtypeStruct((nb, 2 * seq, hy), BF16),
        scratch_shapes=[pltpu.VMEM((seq, tn), BF16)],
        compiler_params=_cparams(("parallel", "parallel", "arbitrary")),
    )(fwd, z, z)


def _dft_inv_kernel(g_ref, s_ref, t_ref, x0_ref, x1_ref, v_ref, bias_ref, ng_ref, o_ref, y_ref):
    half = s_ref.shape[0] // 2

    @pl.when(pl.program_id(2) == 0)
    def _():
        re = s_ref[:half].astype(F32)
        im = s_ref[half:].astype(F32)
        t_re = t_ref[:half].astype(F32)
        t_im = t_ref[half:].astype(F32)
        row0 = lax.broadcasted_iota(jnp.int32, (half, 1), 0) == 0
        y_ref[:half] = (re * t_re - jnp.where(row0, 0.0, im * t_im)).astype(BF16)
        y_ref[half:] = (im * jnp.where(row0, t_im, t_re) + jnp.where(row0, 0.0, re * t_im)).astype(BF16)

    conv = jnp.dot(g_ref[...], y_ref[...], preferred_element_type=F32)
    u = x1_ref[...].astype(F32) * v_ref[...].astype(F32)
    y = x0_ref[...].astype(F32) * (conv + bias_ref[...] * u)
    ms = _group_sum(y * y, _group_ones(min(y.shape[1], PACK_W))) * (1.0 / HEAD)
    o_ref[...] = (y * lax.rsqrt(ms + NORM_EPS) * ng_ref[...]).astype(o_ref.dtype)


def _dft_inv(inv, spec, taps, z, bias, norm_g, tm, tn):
    nb, seq, _ = z.shape
    hy = spec.shape[2]
    tm, tn = _tile(seq, tm), _tile(hy, tn)
    nj = hy // tn
    row = lambda k: pl.BlockSpec((None, tm, tn), lambda b, j, i: (b, i, k * nj + j))
    vec = pl.BlockSpec((1, tn), lambda b, j, i: (0, j))
    return pl.pallas_call(
        _dft_inv_kernel,
        grid=(nb, nj, seq // tm),
        in_specs=[pl.BlockSpec((tm, 2 * seq), lambda b, j, i: (i, 0)),
                  pl.BlockSpec((None, 2 * seq, tn), lambda b, j, i: (b, 0, j)),
                  pl.BlockSpec((2 * seq, tn), lambda b, j, i: (0, j)),
                  row(0), row(1), row(2), vec, vec],
        out_specs=pl.BlockSpec((None, tm, tn), lambda b, j, i: (b, i, j)),
        out_shape=jax.ShapeDtypeStruct((nb, seq, hy), BF16),
        scratch_shapes=[pltpu.VMEM((2 * seq, tn), BF16)],
        compiler_params=_cparams(("parallel", "parallel", "arbitrary")),
    )(inv, spec, taps, z, z, z, bias, norm_g)


def _hyena_filter(length, w1, b1, w2, b2, w3, b3, freq, w4):
    hy = w4.shape[1] // 2
    t = jnp.linspace(0.0, 1.0, length, dtype=F32)[:, None]
    ang = (2.0 * math.pi / length) * jnp.arange(length, dtype=F32)[:, None]
    bands = jnp.linspace(1e-4, FILTER_BANDS - 1, FILTER_BANDS, dtype=F32)[None, :]
    feats = jnp.concatenate([t, jnp.cos(bands * ang), -jnp.sin(bands * ang)], axis=-1)
    hp = lax.Precision.HIGHEST
    h = jnp.sin(freq * (jnp.dot(feats, w1, precision=hp) + b1))
    h = jnp.sin(freq * (jnp.dot(h, w2, precision=hp) + b2))
    h = jnp.sin(freq * (jnp.dot(h, w3, precision=hp) + b3))
    h = jnp.dot(h, w4, precision=hp)
    deltas = jnp.abs(jnp.linspace(math.log(FILTER_TARGET) / SLOW_DECAY_PCT,
                                  math.log(FILTER_TARGET) / FAST_DECAY_PCT, hy, dtype=F32))
    h = h * jnp.exp(-t * jnp.tile(deltas, 2))
    return h[:, :hy], h[:, hy:]


def _taps_spectrum(fwd, h_fwd, h_bwd):
    length = h_fwd.shape[0]
    sig = jnp.stack([h_fwd, h_bwd.at[0].set(0.0)]).astype(BF16)
    spec = _bmm_shared_lhs(fwd, sig, F32, 1024, 1024)
    re, im = spec[:, :length], spec[:, length:]
    scale = 1.0 / (2 * length)
    row0 = lax.broadcasted_iota(jnp.int32, (length, 1), 0) == 0
    t_re = (re[0] + re[1]) * scale
    t_im = jnp.where(row0, im[0] + im[1], im[0] - im[1]) * scale
    return jnp.concatenate([t_re, t_im], axis=0).astype(BF16)


def _scan_kernel(r_ref, k_ref, v_ref, lw_ref, la_ref, w2_ref, a2_ref, w0_ref, a0_ref, kks_ref, ka_ref, rk_ref,
                 s0_ref, y_ref, bonus_ref, sfin_ref, ht_ref):
    c = CHUNK
    w = PACK_W
    direction = pl.program_id(0)
    step = pl.program_id(2)
    n_batch = r_ref.shape[0]
    n_groups = r_ref.shape[2] // w
    lora = w2_ref.shape[0]

    @pl.when(step == 0)
    def _():
        ht_ref[...] = s0_ref[...]

    sign = 1 - 2 * direction
    d_sq = (lax.broadcasted_iota(jnp.int32, (c, c), 1) - lax.broadcasted_iota(jnp.int32, (c, c), 0)) * sign
    incl_sq = jnp.where(d_sq <= 0, 1.0, 0.0).astype(BF16)
    d_c = (lax.broadcasted_iota(jnp.int32, (c, w), 1) % c - lax.broadcasted_iota(jnp.int32, (c, w), 0)) * sign
    strict_c = d_c < 0
    incl_c = d_c <= 0
    eye_c = jnp.where(d_c == 0, 1.0, 0.0)
    bd_mask = (lax.broadcasted_iota(jnp.int32, (w, w), 0) // c) == (lax.broadcasted_iota(jnp.int32, (w, w), 1) // HEAD)
    ones_bd = jnp.where(bd_mask, 1.0, 0.0).astype(BF16)

    def bd(x):
        xb = x.astype(BF16)
        return jnp.where(bd_mask, jnp.concatenate([xb] * PACK, axis=0), jnp.zeros((), BF16))

    def mm_nt(a, b):
        return lax.dot_general(a.astype(BF16), b.astype(BF16), (((1,), (1,)), ((), ())),
                               preferred_element_type=F32)

    fwd = direction == 0
    incl_2 = jnp.concatenate([incl_sq, incl_sq], axis=1)
    prep = []
    for bi in range(n_batch):
        r_all = r_ref[bi].astype(F32)
        k_all = k_ref[bi].astype(F32)
        v_all = v_ref[bi].astype(F32)
        lw_pre = lw_ref[bi].astype(F32)
        la_pre = la_ref[bi].astype(F32)
        lw_pre = jnp.where(fwd, lw_pre[:, :lora], lw_pre[:, lora:])
        la_pre = jnp.where(fwd, la_pre[:, :lora], la_pre[:, lora:])
        x_w = w0_ref[...] + _dot(jnp.tanh(lw_pre), w2_ref[...])
        lw_all = -math.exp(-0.5) * jax.nn.sigmoid(x_w)
        a_sig = jax.nn.sigmoid(a0_ref[...] + _dot(la_pre, a2_ref[...]))
        kk = k_all * kks_ref[...]
        kk = kk * lax.rsqrt(jnp.maximum(_group_sum(kk * kk, ones_bd), 1e-24))
        kd_all = k_all * (1.0 + (a_sig - 1.0) * ka_ref[...])
        bb_all = kk * a_sig
        bonus_ref[bi] = (_group_sum(r_all * kd_all * rk_ref[...], ones_bd) * v_all).astype(bonus_ref.dtype)

        lw_hi = lw_all.astype(BF16)
        lw_lo = (lw_all - lw_hi.astype(F32)).astype(BF16)
        cum = jnp.dot(incl_2, jnp.concatenate([lw_hi, lw_lo], axis=0), preferred_element_type=F32)
        tot = jnp.sum(lw_all, axis=0, keepdims=True)
        g_inv = jnp.exp(-cum)
        g_rem = jnp.exp(tot - cum)
        prep.append(dict(v=v_all, g_tot=jnp.exp(tot), a_t=-kk * jnp.exp(cum - lw_all), r_t=r_all * jnp.exp(cum),
                         b_t=bb_all * g_inv, k_t=kd_all * g_inv, b_s=bb_all * g_rem, k_s=kd_all * g_rem))

    probs = [(bi, g) for bi in range(n_batch) for g in range(n_groups)]
    idx = range(len(probs))
    cols = [slice(g * w, (g + 1) * w) for _, g in probs]
    take = lambda name: [prep[bi][name][:, cols[q]] for q, (bi, _) in enumerate(probs)]
    v_g, a_t, r_t, b_t, k_t, b_s, k_s = (take(nm) for nm in ("v", "a_t", "r_t", "b_t", "k_t", "b_s", "k_s"))
    hts = [ht_ref[bi, g] for bi, g in probs]
    lhs = [jnp.concatenate([a_t[q], r_t[q]], axis=0) for q in idx]
    scores = [_dot(lhs[q], jnp.concatenate([bd(b_t[q]).T, bd(k_t[q]).T], axis=1)) for q in idx]
    h0 = [_dot(lhs[q], hts[q].T) for q in idx]
    a_ab = [jnp.where(strict_c, s[:c, :w], 0.0) for s in scores]
    a_ak = [jnp.where(strict_c, s[:c, w:], 0.0) for s in scores]
    a_rb = [jnp.where(incl_c, s[c:, :w], 0.0) for s in scores]
    a_rk = [jnp.where(incl_c, s[c:, w:], 0.0) for s in scores]
    bd_v = [bd(vq) for vq in v_g]
    av = [_dot(jnp.concatenate([a_ak[q], a_rk[q]], axis=0), bd_v[q]) for q in idx]
    x = [h0[q][:c] + av[q][:c] for q in idx]

    t_inv = [eye_c + a for a in a_ab]
    p = [_dot(a, bd(a)) for a in a_ab]
    n_sq = int(math.log2(c)) - 1
    for it in range(n_sq):
        bd_p = [bd(pq) for pq in p]
        if it + 1 < n_sq:
            pt = [_dot(jnp.concatenate([p[q], t_inv[q]], axis=0), bd_p[q]) for q in idx]
            p = [m[:c] for m in pt]
            t_inv = [t_inv[q] + pt[q][c:] for q in idx]
        else:
            t_inv = [t_inv[q] + _dot(t_inv[q], bd_p[q]) for q in idx]
    u = [_dot(t_inv[q], bd(x[q])) for q in idx]

    y = [h0[q][c:] + av[q][c:] + _dot(a_rb[q], bd(u[q])) for q in idx]
    for q, (bi, _) in enumerate(probs):
        y_ref[bi, :, cols[q]] = y[q].astype(y_ref.dtype)

    for q, (bi, g) in enumerate(probs):
        uv = jnp.concatenate([u[q], v_g[q]], axis=0).astype(BF16)
        bk = jnp.concatenate([b_s[q], k_s[q]], axis=0).astype(BF16)
        upd = lax.dot_general(uv, bk, (((0,), (0,)), ((), ())), preferred_element_type=F32)
        ht_ref[bi, g] = hts[q] * prep[bi]["g_tot"][:, cols[q]] + jnp.where(bd_mask, upd, 0.0)

    @pl.when(step == pl.num_programs(2) - 1)
    def _():
        sfin_ref[...] = ht_ref[...]


def _rwkv_scan(z, col0, lora_col0, w2, a2, w0, a0, kks, ka, rk, s0):
    nb, seq, _ = z.shape
    lora, ch = w2.shape[1:]
    c = CHUNK
    assert CHUNK == HEAD and ch % PACK_W == 0 and seq % c == 0 and 2 * lora == LANE
    assert col0 % ch == 0 and lora_col0 % LANE == 0
    nc = seq // c
    ng = ch // PACK_W
    bs = SCAN_BATCH if nb % SCAN_BATCH == 0 else 1
    chunk_of = lambda n, i: jnp.where(n == 0, i, nc - 1 - i)
    zcol = lambda k: pl.BlockSpec((bs, c, ch), lambda n, b, i: (b, chunk_of(n, i), col0 // ch + k))
    zlora = lambda k: pl.BlockSpec((bs, c, LANE), lambda n, b, i: (b, chunk_of(n, i), lora_col0 // LANE + k))
    per_dir_w = pl.BlockSpec((None, lora, ch), lambda n, b, i: (n, 0, 0))
    per_dir_v = pl.BlockSpec((None, 1, ch), lambda n, b, i: (n, 0, 0))
    shared_v = pl.BlockSpec((1, ch), lambda n, b, i: (0, 0))
    state = pl.BlockSpec((None, bs, ng, PACK_W, PACK_W), lambda n, b, i: (n, b, 0, 0, 0))
    out = pl.BlockSpec((None, bs, c, ch), lambda n, b, i: (n, b, chunk_of(n, i), 0))
    return pl.pallas_call(
        _scan_kernel,
        grid=(2, nb // bs, nc),
        in_specs=[zcol(0), zcol(1), zcol(2), zlora(0), zlora(1), per_dir_w, per_dir_w, per_dir_v, per_dir_v,
                  shared_v, shared_v, shared_v, state],
        out_specs=[out, out, state],
        out_shape=[jax.ShapeDtypeStruct((2, nb, seq, ch), BF16), jax.ShapeDtypeStruct((2, nb, seq, ch), BF16),
                   jax.ShapeDtypeStruct((2, nb, ng, PACK_W, PACK_W), F32)],
        scratch_shapes=[pltpu.VMEM((bs, ng, PACK_W, PACK_W), F32)],
        compiler_params=_cparams(("parallel", "parallel", "arbitrary")),
    )(z, z, z, z, z, w2, a2, w0.reshape(2, 1, ch), a0.reshape(2, 1, ch), kks, ka, rk, s0)


def _outproj_kernel(yh_ref, ys0_ref, ys1_ref, b0_ref, b1_ref, lg_ref, g2_ref, lng_ref, lnb_ref, ow_ref,
                    x_ref, gate_ref, n2g_ref, shift_ref, scale_ref, rwh_ref, rwl_ref, rb_ref,
                    x1_ref, h2_ref, lg_out_ref, mix_ref):
    hy = yh_ref.shape[1]
    ones = _group_ones(PACK_W)
    y = ys0_ref[...].astype(F32) + ys1_ref[...].astype(F32)
    mu = _group_sum(y, ones) * (1.0 / HEAD)
    yc = y - mu
    var = _group_sum(yc * yc, ones) * (1.0 / HEAD)
    y = yc * lax.rsqrt(var + LNX_EPS) * lng_ref[...] + lnb_ref[...]
    gate = _dot(jax.nn.sigmoid(lg_ref[...].astype(F32)), g2_ref[...])
    y = (y + b0_ref[...].astype(F32) + b1_ref[...].astype(F32)) * gate
    mix_ref[:, :hy] = yh_ref[...]
    mix_ref[:, hy:] = y.astype(BF16)
    mix = jnp.dot(mix_ref[...], ow_ref[...], preferred_element_type=F32)
    x1 = x_ref[...] + gate_ref[...] * mix
    x1_ref[...] = x1
    h2 = x1 * lax.rsqrt(jnp.mean(x1 * x1, axis=-1, keepdims=True) + NORM_EPS) * n2g_ref[...]
    h2 = h2 * (1.0 + scale_ref[...]) + shift_ref[...]
    h2_ref[...] = h2.astype(BF16)
    h_hi = h2.astype(BF16)
    h_lo = (h2 - h_hi.astype(F32)).astype(BF16)
    logits = (jnp.dot(h_hi, rwh_ref[...], preferred_element_type=F32)
              + jnp.dot(h_hi, rwl_ref[...], preferred_element_type=F32)
              + jnp.dot(h_lo, rwh_ref[...], preferred_element_type=F32))
    lg_out_ref[...] = logits + rb_ref[...]


def _outproj(y_hy, ys, bonus, z, lora_g_col, g2, lnx_g, lnx_b, out_w, x, gate, n2g, shift, scale,
             rw_hi, rw_lo, rb, tm):
    nb, seq, d = x.shape
    m = nb * seq
    hy = y_hy.shape[2]
    rw = ys.shape[3]
    tm = _tile(seq, tm)
    lg = g2.shape[0]
    ne = rw_hi.shape[1]
    assert lora_g_col % lg == 0
    ys2 = ys.reshape(2, m, rw)
    bn2 = bonus.reshape(2, m, rw)
    nz = z.shape[2]
    full = lambda r, c: pl.BlockSpec((r, c), lambda i: (0, 0))
    dir_spec = lambda n: pl.BlockSpec((None, tm, rw), lambda i: (n, i, 0))
    mod_spec = pl.BlockSpec((None, 1, d), lambda i: ((i * tm) // seq, 0, 0))
    return pl.pallas_call(
        _outproj_kernel,
        grid=(m // tm,),
        in_specs=[pl.BlockSpec((tm, hy), lambda i: (i, 0)), dir_spec(0), dir_spec(1), dir_spec(0), dir_spec(1),
                  pl.BlockSpec((tm, lg), lambda i: (i, lora_g_col // lg)),
                  full(lg, rw), full(1, rw), full(1, rw), full(hy + rw, d),
                  pl.BlockSpec((tm, d), lambda i: (i, 0)), mod_spec, full(1, d), mod_spec, mod_spec,
                  full(d, ne), full(d, ne), full(1, ne)],
        out_specs=[pl.BlockSpec((tm, d), lambda i: (i, 0)), pl.BlockSpec((tm, d), lambda i: (i, 0)),
                   pl.BlockSpec((tm, ne), lambda i: (i, 0))],
        out_shape=[jax.ShapeDtypeStruct((m, d), F32), jax.ShapeDtypeStruct((m, d), BF16),
                   jax.ShapeDtypeStruct((m, ne), F32)],
        scratch_shapes=[pltpu.VMEM((tm, hy + rw), BF16)],
        compiler_params=_cparams(("parallel",)),
    )(y_hy.reshape(m, hy), ys2, ys2, bn2, bn2, z.reshape(m, nz), g2, lnx_g, lnx_b, out_w,
      x.reshape(m, d), gate, n2g, shift, scale, rw_hi, rw_lo, rb)


def _expert_changed(te_ref, i):
    prev = te_ref[jnp.maximum(i - 1, 0)]
    return jnp.logical_or(i == 0, te_ref[i] != prev)


def _moe_up_kernel(te_ref, nu_ref, x_ref, wg_ref, wu_ref, bg_ref, bu_ref, h_ref, wg_bf, wu_bf):
    i = pl.program_id(1)

    @pl.when(_expert_changed(te_ref, i))
    def _():
        wg_bf[...] = wg_ref[...].astype(BF16)
        wu_bf[...] = wu_ref[...].astype(BF16)

    @pl.when(i < nu_ref[0])
    def _():
        x = x_ref[...]
        g = jnp.dot(x, wg_bf[...], preferred_element_type=F32) + bg_ref[...]
        u = jnp.dot(x, wu_bf[...], preferred_element_type=F32) + bu_ref[...]
        g = jnp.minimum(g, SWIGLU_LIMIT)
        u = jnp.clip(u, -SWIGLU_LIMIT, SWIGLU_LIMIT)
        h_ref[...] = ((u + 1.0) * (g * jax.nn.sigmoid(SWIGLU_ALPHA * g))).astype(h_ref.dtype)

    @pl.when(i >= nu_ref[0])
    def _():
        h_ref[...] = jnp.zeros_like(h_ref)


def _moe_down_kernel(te_ref, nu_ref, h_ref, wd_ref, bd_ref, *rest):
    y_ref, wd_bf = rest[-2:]
    i = pl.program_id(1)

    @pl.when(_expert_changed(te_ref, i))
    def _():
        wd_bf[...] = wd_ref[...].astype(BF16)

    @pl.when(i < nu_ref[0])
    def _():
        y = jnp.dot(h_ref[...], wd_bf[...], preferred_element_type=F32) + bd_ref[...]
        y_ref[...] = y.astype(y_ref.dtype)

    @pl.when(i >= nu_ref[0])
    def _():
        y_ref[...] = jnp.zeros_like(y_ref)


def _moe_up(tile_expert, n_used, xs, wg, wu, bg, bu, tn):
    n_rows, d = xs.shape
    e, _, f = wg.shape
    tm = MOE_TILE
    tn = _tile(f, tn)
    w_spec = pl.BlockSpec((None, d, tn), lambda j, i, te, nu: (te[i], 0, j))
    b_spec = pl.BlockSpec((None, 1, tn), lambda j, i, te, nu: (te[i], 0, j))
    return pl.pallas_call(
        _moe_up_kernel,
        grid_spec=pltpu.PrefetchScalarGridSpec(
            num_scalar_prefetch=2,
            grid=(f // tn, n_rows // tm),
            in_specs=[pl.BlockSpec((tm, d), lambda j, i, te, nu: (i, 0)), w_spec, w_spec, b_spec, b_spec],
            out_specs=pl.BlockSpec((tm, tn), lambda j, i, te, nu: (i, j)),
            scratch_shapes=[pltpu.VMEM((d, tn), BF16), pltpu.VMEM((d, tn), BF16)]),
        out_shape=jax.ShapeDtypeStruct((n_rows, f), BF16),
        compiler_params=_cparams(("arbitrary", "arbitrary")),
    )(tile_expert, n_used, xs, wg, wu, bg.reshape(e, 1, f), bu.reshape(e, 1, f))


def _moe_down(tile_expert, n_used, hs, wd, bdn, tn, total_rows, tile0, ys_prev):
    n_rows, f = hs.shape
    e, _, d = wd.shape
    tm = MOE_TILE
    tn = _tile(d, tn)
    in_specs = [pl.BlockSpec((tm, f), lambda j, i, te, nu: (i, 0)),
                pl.BlockSpec((None, f, tn), lambda j, i, te, nu: (te[i], 0, j)),
                pl.BlockSpec((None, 1, tn), lambda j, i, te, nu: (te[i], 0, j))]
    args = [tile_expert, n_used, hs, wd, bdn.reshape(e, 1, d)]
    aliases = {}
    if ys_prev is not None:
        in_specs.append(pl.BlockSpec(memory_space=pl.ANY))
        args.append(ys_prev)
        aliases = {len(args) - 1: 0}
    return pl.pallas_call(
        _moe_down_kernel,
        grid_spec=pltpu.PrefetchScalarGridSpec(
            num_scalar_prefetch=2,
            grid=(d // tn, n_rows // tm),
            in_specs=in_specs,
            out_specs=pl.BlockSpec((tm, tn), lambda j, i, te, nu: (tile0 + i, j)),
            scratch_shapes=[pltpu.VMEM((f, tn), BF16)]),
        out_shape=jax.ShapeDtypeStruct((total_rows, d), BF16),
        input_output_aliases=aliases,
        compiler_params=_cparams(("arbitrary", "arbitrary")),
    )(*args)


def _moe(h2, logits, wg, bg, wu, bu, wd, bdn):
    n_tok, d = h2.shape
    e = logits.shape[1]
    tm = MOE_TILE
    top_logits, top_idx = lax.top_k(logits, TOP_K)
    gates = jax.nn.softmax(top_logits, axis=-1)
    n_asg = n_tok * TOP_K
    e_flat = top_idx.reshape(n_asg).astype(jnp.int32)
    asg = jnp.arange(n_asg, dtype=jnp.int32)
    e_sorted, order = lax.sort((e_flat, asg), num_keys=1, is_stable=True)
    counts = jnp.sum((e_flat[:, None] == jnp.arange(e, dtype=jnp.int32)[None, :]).astype(jnp.int32), axis=0)
    raw_start = jnp.cumsum(counts) - counts
    padded = (counts + tm - 1) // tm * tm
    pad_end = jnp.cumsum(padded)
    pad_start = pad_end - padded
    dest_sorted = (pad_start - raw_start)[e_sorted] + asg
    _, dest = lax.sort((order, dest_sorted), num_keys=1)
    n_tiles = -(-(n_asg + e * (tm - 1)) // tm)
    n_rows = n_tiles * tm
    tile_start = jnp.arange(n_tiles, dtype=jnp.int32) * tm
    tile_expert = jnp.minimum(jnp.sum((pad_end[None, :] <= tile_start[:, None]).astype(jnp.int32), axis=1), e - 1)
    n_used = pad_end[-1] // tm
    row = jnp.arange(n_rows, dtype=jnp.int32)
    row_e = jnp.repeat(tile_expert, tm)
    rank = row - pad_start[row_e]
    src = jnp.clip(raw_start[row_e] + rank, 0, n_asg - 1)
    row_tok = jnp.where(rank < counts[row_e], order[src] // TOP_K, 0)

    n_parts = next(p for p in (MOE_PARTS, 2, 1) if n_tiles % p == 0)
    tp = n_tiles // n_parts
    ys = None
    for q in range(n_parts):
        te_q = tile_expert[q * tp:(q + 1) * tp]
        nu_q = jnp.clip(n_used - q * tp, 0, tp).astype(jnp.int32).reshape(1)
        xs = h2[row_tok[q * tp * tm:(q + 1) * tp * tm]]
        hs = _moe_up(te_q, nu_q, xs, wg, wu, bg, bu, 1024)
        ys = _moe_down(te_q, nu_q, hs, wd, bdn, 1024, n_rows, q * tp, ys)
    picked = ys[dest.reshape(n_tok, TOP_K).T]
    return picked, gates


def _combine_kernel(y_ref, g_ref, x1_ref, gate_ref, fg_ref, o_ref):
    g = g_ref[...]
    acc = y_ref[0].astype(F32) * g[:, 0:1]
    for k in range(1, y_ref.shape[0]):
        acc = acc + y_ref[k].astype(F32) * g[:, k:k + 1]
    xo = x1_ref[...] + gate_ref[...] * acc
    o_ref[...] = xo * lax.rsqrt(jnp.mean(xo * xo, axis=-1, keepdims=True) + NORM_EPS) * fg_ref[...]


def _combine(picked, gates, x1, gate, final_g, seq, tm):
    nk, m, d = picked.shape
    tm = _tile(seq, tm)
    return pl.pallas_call(
        _combine_kernel,
        grid=(m // tm,),
        in_specs=[pl.BlockSpec((nk, tm, d), lambda i: (0, i, 0)),
                  pl.BlockSpec((tm, nk), lambda i: (i, 0)),
                  pl.BlockSpec((tm, d), lambda i: (i, 0)),
                  pl.BlockSpec((None, 1, d), lambda i: ((i * tm) // seq, 0, 0)),
                  pl.BlockSpec((1, d), lambda i: (0, 0))],
        out_specs=pl.BlockSpec((tm, d), lambda i: (i, 0)),
        out_shape=jax.ShapeDtypeStruct((m, d), F32),
        compiler_params=_cparams(("parallel",)),
    )(picked, gates, x1, gate, final_g)


def _rmsnorm(x, g):
    return x * lax.rsqrt(jnp.mean(x * x, axis=-1, keepdims=True) + NORM_EPS) * g


def kernel(x, c, ctx, c_ctx, ada_w, ada_b, norm1_g, norm2_g, in_w, conv_w, conv_b, hy_w1, hy_b1, hy_w2, hy_b2, hy_w3, hy_b3, hy_freq, hy_w4, hy_bias, hy_norm_g, rw_w0, rw_w1, rw_w2, rw_a0, rw_a1, rw_a2, rw_kk, rw_ka, rw_rk, rw_g1, rw_g2, rw_lnx_g, rw_lnx_b, out_w, router_w, router_b, ex_w_gate, ex_b_gate, ex_w_up, ex_b_up, ex_w_down, ex_b_down, final_g):
    assert ada_w.shape[0] == 1, "single-layer block: context outputs never reach a latent token"
    nb, seq, d = x.shape
    hy = hy_bias.shape[1]
    rw = rw_kk.shape[1]
    n_hy = 3 * hy
    proj = in_w.shape[2]
    n_exp = router_w.shape[2]

    cond = jnp.concatenate([jax.nn.silu(c), jax.nn.silu(c_ctx)[None]], axis=0)
    mod = _matmul(cond, ada_w[0], F32, cond.shape[0], 1024) + ada_b[0]
    mod_x = [m[:, None, :] for m in jnp.split(mod[:nb], 6, axis=-1)]
    mod_c = [m[:, None, :] for m in jnp.split(mod[nb:], 6, axis=-1)]

    lora_w = jnp.concatenate([rw_w1[0, 0], rw_w1[0, 1], rw_a1[0, 0], rw_a1[0, 1], rw_g1[0]], axis=1)
    n_lora = lora_w.shape[1]
    nz = -(-(proj + n_lora) // 512) * 512
    pad = nz - proj - n_lora
    w_all = jnp.concatenate([in_w[0], lora_w, jnp.zeros((d, pad), F32)], axis=1).astype(BF16)
    pass_taps = jnp.concatenate([jnp.zeros((1, n_lora + pad), F32), jnp.ones((1, n_lora + pad), F32),
                                 jnp.zeros((1, n_lora + pad), F32)], axis=0)
    cw_all = jnp.concatenate([conv_w[0], pass_taps], axis=1)
    cb_all = jnp.concatenate([conv_b[0], jnp.zeros((n_lora + pad,), F32)])[None]
    g1 = norm1_g[0][None]
    zx = _inproj(x, mod_x[0], mod_x[1], g1, w_all, cw_all, cb_all, 1024, 512)
    zc = _inproj(ctx, mod_c[0], mod_c[1], g1, w_all[:, n_hy:], cw_all[:, n_hy:], cb_all[:, n_hy:], 256, 512)

    h_fwd, h_bwd = _hyena_filter(seq, hy_w1[0], hy_b1[0], hy_w2[0], hy_b2[0], hy_w3[0], hy_b3[0],
                                 hy_freq[0], hy_w4[0])
    fwd, inv = _dft_matrices(seq)
    taps = _taps_spectrum(fwd, h_fwd, h_bwd)
    spec = _dft_fwd(fwd, zx, hy, 1024, 512)
    y_hy = _dft_inv(inv, spec, taps, zx, hy_bias[0][None], hy_norm_g[0][None], 1024, 512)

    scan_args = (rw_w2[0].astype(BF16), rw_a2[0].astype(BF16), rw_w0[0], rw_a0[0],
                 rw_kk[0][None], rw_ka[0][None], rw_rk[0].reshape(1, rw))
    s0 = jnp.zeros((2, nb, rw // PACK_W, PACK_W, PACK_W), F32)
    _, _, s_ctx = _rwkv_scan(zc, 0, proj - n_hy, *scan_args, s0)
    ys, bonus, _ = _rwkv_scan(zx, n_hy, proj, *scan_args, s_ctx)

    ne = -(-n_exp // LANE) * LANE
    rw_pad = jnp.pad(router_w[0], ((0, 0), (0, ne - n_exp)))
    rw_hi = rw_pad.astype(BF16)
    rw_lo = (rw_pad - rw_hi.astype(F32)).astype(BF16)
    rb = jnp.pad(router_b[0], (0, ne - n_exp))[None]
    x1, h2, logits = _outproj(y_hy, ys, bonus, zx, proj + n_lora - rw_g1.shape[-1], rw_g2[0].astype(BF16),
                              rw_lnx_g[0][None], rw_lnx_b[0][None], out_w[0].astype(BF16), x,
                              mod_x[2], norm2_g[0][None], mod_x[3], mod_x[4], rw_hi, rw_lo, rb, 256)

    picked, gates = _moe(h2, logits[:, :n_exp], ex_w_gate[0], ex_b_gate[0], ex_w_up[0], ex_b_up[0],
                         ex_w_down[0], ex_b_down[0])
    return _combine(picked, gates, x1, mod_x[5], final_g[None], seq, 256).reshape(nb, seq, d)
```

```python
import functools
import math

import jax
import jax.numpy as jnp
from jax import lax
from jax.experimental import pallas as pl
from jax.experimental.pallas import tpu as pltpu

F32 = jnp.float32
BF16 = jnp.bfloat16

HEAD = 64
CHUNK = 64
PACK = 4
PACK_W = PACK * HEAD
SCAN_BATCH = 4
HALO = 16
INPROJ_ROWS = 256
SUB_ROWS = 256
FILTER_BANDS = 16
FILTER_TARGET = 1e-2
FAST_DECAY_PCT = 0.3
SLOW_DECAY_PCT = 1.5
TOP_K = 4
SWIGLU_LIMIT = 7.0
SWIGLU_ALPHA = 1.702
NORM_EPS = 1e-6
LNX_EPS = 64e-5
MOE_TILE = 512
MOE_PARTS = 4
COMBINE_PARTS = 4
LANE = 128
VMEM_LIMIT = 56 * 1024 * 1024


def _cparams(sem):
    return pltpu.CompilerParams(dimension_semantics=sem, vmem_limit_bytes=VMEM_LIMIT)


def _tile(n, want):
    if n <= want:
        return n
    t = want
    while n % t:
        t //= 2
    assert t >= 8, (n, want)
    return t


def _dot(a, b):
    return jnp.dot(a.astype(BF16), b.astype(BF16), preferred_element_type=F32)


def _group_ones(width):
    r = lax.broadcasted_iota(jnp.int32, (width, width), 0) // HEAD
    c = lax.broadcasted_iota(jnp.int32, (width, width), 1) // HEAD
    return jnp.where(r == c, 1.0, 0.0).astype(BF16)


def _group_sum(x, ones):
    width = ones.shape[0]
    parts = [_dot(x[:, o:o + width], ones) for o in range(0, x.shape[1], width)]
    return parts[0] if len(parts) == 1 else jnp.concatenate(parts, axis=1)


def _mm_kernel(a_ref, b_ref, o_ref):
    o_ref[...] = _dot(a_ref[...], b_ref[...]).astype(o_ref.dtype)


def _matmul(a, b, out_dtype, tm, tn):
    m, k = a.shape
    n = b.shape[1]
    tm, tn = _tile(m, tm), _tile(n, tn)
    return pl.pallas_call(
        _mm_kernel,
        grid=(n // tn, m // tm),
        in_specs=[pl.BlockSpec((tm, k), lambda j, i: (i, 0)),
                  pl.BlockSpec((k, tn), lambda j, i: (0, j))],
        out_specs=pl.BlockSpec((tm, tn), lambda j, i: (i, j)),
        out_shape=jax.ShapeDtypeStruct((m, n), out_dtype),
        compiler_params=_cparams(("parallel", "parallel")),
    )(a, b)


def _bmm_shared_lhs(a, b, out_dtype, tm, tn):
    m, k = a.shape
    nb, _, n = b.shape
    tm, tn = _tile(m, tm), _tile(n, tn)
    return pl.pallas_call(
        _mm_kernel,
        grid=(m // tm, nb, n // tn),
        in_specs=[pl.BlockSpec((tm, k), lambda i, bb, j: (i, 0)),
                  pl.BlockSpec((None, k, tn), lambda i, bb, j: (bb, 0, j))],
        out_specs=pl.BlockSpec((None, tm, tn), lambda i, bb, j: (bb, i, j)),
        out_shape=jax.ShapeDtypeStruct((nb, m, n), out_dtype),
        compiler_params=_cparams(("parallel", "parallel", "parallel")),
    )(a, b)


def _inproj_kernel(seq, xm_ref, xp_ref, xn_ref, shift_ref, scale_ref, g_ref, w_ref, cw_ref, cb_ref,
                   z_ref, h_ref, zz_ref):
    tm = xm_ref.shape[0]
    i = pl.program_id(0)

    @pl.when(pl.program_id(1) == 0)
    def _():
        def norm(xv):
            y = xv * lax.rsqrt(jnp.mean(xv * xv, axis=-1, keepdims=True) + NORM_EPS) * g_ref[...]
            return y * (1.0 + scale_ref[...]) + shift_ref[...]
        keep_prev = jnp.where((i * tm) % seq == 0, 0.0, 1.0)
        keep_next = jnp.where(((i + 1) * tm) % seq == 0, 0.0, 1.0)
        h_ref[0:HALO] = (norm(xp_ref[...]) * keep_prev).astype(BF16)
        h_ref[HALO:HALO + tm] = norm(xm_ref[...]).astype(BF16)
        h_ref[HALO + tm:] = (norm(xn_ref[...]) * keep_next).astype(BF16)

    cw = cw_ref[...]
    n_chunks = max(1, tm // INPROJ_ROWS)
    q = tm // n_chunks
    d0 = o0 = 0
    for ci in range(n_chunks):
        last = ci + 1 == n_chunks
        d1 = tm + 2 * HALO if last else (ci + 1) * q + HALO
        o1 = tm if last else d1 - 2 * HALO
        zz_ref[d0:d1] = jnp.dot(h_ref[d0:d1], w_ref[...], preferred_element_type=F32)
        z_ref[o0:o1] = (zz_ref[HALO - 1 + o0:HALO - 1 + o1] * cw[0:1] + zz_ref[HALO + o0:HALO + o1] * cw[1:2]
                        + zz_ref[HALO + 1 + o0:HALO + 1 + o1] * cw[2:3] + cb_ref[...]).astype(z_ref.dtype)
        d0, o0 = d1, o1


def _inproj(x, shift, scale, g, w, cw, cb, tm, tn):
    nb, seq, d = x.shape
    n = w.shape[1]
    tm, tn = _tile(seq, tm), _tile(n, tn)
    m = nb * seq
    n_halo = m // HALO
    per_batch = shift.shape[0] > 1
    mod_spec = pl.BlockSpec((None, 1, d), (lambda i, j: ((i * tm) // seq, 0, 0)) if per_batch
                            else (lambda i, j: (0, 0, 0)))
    x2 = x.reshape(m, d)
    out = pl.pallas_call(
        functools.partial(_inproj_kernel, seq),
        grid=(m // tm, n // tn),
        in_specs=[pl.BlockSpec((tm, d), lambda i, j: (i, 0)),
                  pl.BlockSpec((HALO, d), lambda i, j: (jnp.maximum(i * (tm // HALO) - 1, 0), 0)),
                  pl.BlockSpec((HALO, d), lambda i, j: (jnp.minimum((i + 1) * (tm // HALO), n_halo - 1), 0)),
                  mod_spec, mod_spec,
                  pl.BlockSpec((1, d), lambda i, j: (0, 0)),
                  pl.BlockSpec((d, tn), lambda i, j: (0, j)),
                  pl.BlockSpec((3, tn), lambda i, j: (0, j)),
                  pl.BlockSpec((1, tn), lambda i, j: (0, j))],
        out_specs=pl.BlockSpec((tm, tn), lambda i, j: (i, j)),
        out_shape=jax.ShapeDtypeStruct((m, n), BF16),
        scratch_shapes=[pltpu.VMEM((tm + 2 * HALO, d), BF16), pltpu.VMEM((tm + 2 * HALO, tn), F32)],
        compiler_params=_cparams(("parallel", "arbitrary")),
    )(x2, x2, x2, shift, scale, g, w, cw, cb)
    return out.reshape(nb, seq, n)


def _dft_matrices(length):
    n = 2 * length
    f = lax.broadcasted_iota(jnp.int32, (length, length), 0)
    t = lax.broadcasted_iota(jnp.int32, (length, length), 1)
    ang = ((f * t) % n).astype(F32) * (2.0 * math.pi / n)
    cos, sin = jnp.cos(ang), jnp.sin(ang)
    nyq = jnp.where(t % 2 == 0, 1.0, -1.0)
    fwd = jnp.concatenate([cos, jnp.where(f == 0, nyq, -sin)], axis=0)
    f_t = f.T
    inv = jnp.concatenate([jnp.where(f_t == 0, 1.0, 2.0 * cos.T),
                           jnp.where(f_t == 0, nyq.T, -2.0 * sin.T)], axis=1)
    return fwd.astype(BF16), inv.astype(BF16)


def _dft_fwd_kernel(f_ref, x1_ref, v_ref, o_ref, u_ref):
    @pl.when(pl.program_id(2) == 0)
    def _():
        u_ref[...] = (x1_ref[...].astype(F32) * v_ref[...].astype(F32)).astype(BF16)

    o_ref[...] = jnp.dot(f_ref[...], u_ref[...], preferred_element_type=F32).astype(o_ref.dtype)


def _dft_fwd(fwd, z, hy, tm, tn):
    nb, seq, _ = z.shape
    tm, tn = _tile(2 * seq, tm), _tile(hy, tn)
    nj = hy // tn
    return pl.pallas_call(
        _dft_fwd_kernel,
        grid=(nb, nj, 2 * seq // tm),
        in_specs=[pl.BlockSpec((tm, seq), lambda b, j, i: (i, 0)),
                  pl.BlockSpec((None, seq, tn), lambda b, j, i: (b, 0, nj + j)),
                  pl.BlockSpec((None, seq, tn), lambda b, j, i: (b, 0, 2 * nj + j))],
        out_specs=pl.BlockSpec((None, tm, tn), lambda b, j, i: (b, i, j)),
        out_shape=jax.ShapeDtypeStruct((nb, 2 * seq, hy), BF16),
        scratch_shapes=[pltpu.VMEM((seq, tn), BF16)],
        compiler_params=_cparams(("parallel", "parallel", "arbitrary")),
    )(fwd, z, z)


def _dft_inv_kernel(g_ref, s_ref, t_ref, x0_ref, x1_ref, v_ref, bias_ref, ng_ref, o_ref, y_ref):
    half = s_ref.shape[0] // 2

    @pl.when(pl.program_id(2) == 0)
    def _():
        re = s_ref[:half].astype(F32)
        im = s_ref[half:].astype(F32)
        t_re = t_ref[:half].astype(F32)
        t_im = t_ref[half:].astype(F32)
        row0 = lax.broadcasted_iota(jnp.int32, (half, 1), 0) == 0
        y_ref[:half] = (re * t_re - jnp.where(row0, 0.0, im * t_im)).astype(BF16)
        y_ref[half:] = (im * jnp.where(row0, t_im, t_re) + jnp.where(row0, 0.0, re * t_im)).astype(BF16)

    tm, tn = o_ref.shape
    ones = _group_ones(min(tn, PACK_W))
    rows = _tile(tm, SUB_ROWS)
    for r0 in range(0, tm, rows):
        rs = slice(r0, r0 + rows)
        conv = jnp.dot(g_ref[rs], y_ref[...], preferred_element_type=F32)
        u = x1_ref[rs].astype(F32) * v_ref[rs].astype(F32)
        y = x0_ref[rs].astype(F32) * (conv + bias_ref[...] * u)
        ms = _group_sum(y * y, ones) * (1.0 / HEAD)
        o_ref[rs] = (y * lax.rsqrt(ms + NORM_EPS) * ng_ref[...]).astype(o_ref.dtype)


def _dft_inv(inv, spec, taps, z, bias, norm_g, tm, tn):
    nb, seq, _ = z.shape
    hy = spec.shape[2]
    tm, tn = _tile(seq, tm), _tile(hy, tn)
    nj = hy // tn
    row = lambda k: pl.BlockSpec((None, tm, tn), lambda b, j, i: (b, i, k * nj + j))
    vec = pl.BlockSpec((1, tn), lambda b, j, i: (0, j))
    return pl.pallas_call(
        _dft_inv_kernel,
        grid=(nb, nj, seq // tm),
        in_specs=[pl.BlockSpec((tm, 2 * seq), lambda b, j, i: (i, 0)),
                  pl.BlockSpec((None, 2 * seq, tn), lambda b, j, i: (b, 0, j)),
                  pl.BlockSpec((2 * seq, tn), lambda b, j, i: (0, j)),
                  row(0), row(1), row(2), vec, vec],
        out_specs=pl.BlockSpec((None, tm, tn), lambda b, j, i: (b, i, j)),
        out_shape=jax.ShapeDtypeStruct((nb, seq, hy), BF16),
        scratch_shapes=[pltpu.VMEM((2 * seq, tn), BF16)],
        compiler_params=_cparams(("parallel", "parallel", "arbitrary")),
    )(inv, spec, taps, z, z, z, bias, norm_g)


def _hyena_filter(length, w1, b1, w2, b2, w3, b3, freq, w4):
    hy = w4.shape[1] // 2
    t = jnp.linspace(0.0, 1.0, length, dtype=F32)[:, None]
    ang = (2.0 * math.pi / length) * jnp.arange(length, dtype=F32)[:, None]
    bands = jnp.linspace(1e-4, FILTER_BANDS - 1, FILTER_BANDS, dtype=F32)[None, :]
    feats = jnp.concatenate([t, jnp.cos(bands * ang), -jnp.sin(bands * ang)], axis=-1)
    hp = lax.Precision.HIGHEST
    h = jnp.sin(freq * (jnp.dot(feats, w1, precision=hp) + b1))
    h = jnp.sin(freq * (jnp.dot(h, w2, precision=hp) + b2))
    h = jnp.sin(freq * (jnp.dot(h, w3, precision=hp) + b3))
    h = jnp.dot(h, w4, precision=hp)
    deltas = jnp.abs(jnp.linspace(math.log(FILTER_TARGET) / SLOW_DECAY_PCT,
                                  math.log(FILTER_TARGET) / FAST_DECAY_PCT, hy, dtype=F32))
    h = h * jnp.exp(-t * jnp.tile(deltas, 2))
    return h[:, :hy], h[:, hy:]


def _taps_spectrum(fwd, h_fwd, h_bwd):
    length = h_fwd.shape[0]
    sig = jnp.stack([h_fwd, h_bwd.at[0].set(0.0)]).astype(BF16)
    spec = _bmm_shared_lhs(fwd, sig, F32, 1024, 1024)
    re, im = spec[:, :length], spec[:, length:]
    scale = 1.0 / (2 * length)
    row0 = lax.broadcasted_iota(jnp.int32, (length, 1), 0) == 0
    t_re = (re[0] + re[1]) * scale
    t_im = jnp.where(row0, im[0] + im[1], im[0] - im[1]) * scale
    return jnp.concatenate([t_re, t_im], axis=0).astype(BF16)


def _scan_kernel(r_ref, k_ref, v_ref, lw_ref, la_ref, w2_ref, a2_ref, w0_ref, a0_ref, kks_ref, ka_ref, rk_ref,
                 s0_ref, y_ref, bonus_ref, sfin_ref, ht_ref):
    c = CHUNK
    w = PACK_W
    direction = pl.program_id(0)
    step = pl.program_id(2)
    n_batch = r_ref.shape[0]
    n_groups = r_ref.shape[2] // w
    lora = w2_ref.shape[0]

    @pl.when(step == 0)
    def _():
        ht_ref[...] = s0_ref[...]

    sign = 1 - 2 * direction
    d_sq = (lax.broadcasted_iota(jnp.int32, (c, c), 1) - lax.broadcasted_iota(jnp.int32, (c, c), 0)) * sign
    incl_sq = jnp.where(d_sq <= 0, 1.0, 0.0).astype(BF16)
    d_c = (lax.broadcasted_iota(jnp.int32, (c, w), 1) % c - lax.broadcasted_iota(jnp.int32, (c, w), 0)) * sign
    strict_c = d_c < 0
    incl_c = d_c <= 0
    eye_c = jnp.where(d_c == 0, 1.0, 0.0)
    bd_mask = (lax.broadcasted_iota(jnp.int32, (w, w), 0) // c) == (lax.broadcasted_iota(jnp.int32, (w, w), 1) // HEAD)
    ones_bd = jnp.where(bd_mask, 1.0, 0.0).astype(BF16)

    def bd(x):
        xb = x.astype(BF16)
        return jnp.where(bd_mask, jnp.concatenate([xb] * PACK, axis=0), jnp.zeros((), BF16))

    def mm_nt(a, b):
        return lax.dot_general(a.astype(BF16), b.astype(BF16), (((1,), (1,)), ((), ())),
                               preferred_element_type=F32)

    fwd = direction == 0
    incl_2 = jnp.concatenate([incl_sq, incl_sq], axis=1)
    prep = []
    for bi in range(n_batch):
        r_all = r_ref[bi].astype(F32)
        k_all = k_ref[bi].astype(F32)
        v_all = v_ref[bi].astype(F32)
        lw_pre = lw_ref[bi].astype(F32)
        la_pre = la_ref[bi].astype(F32)
        lw_pre = jnp.where(fwd, lw_pre[:, :lora], lw_pre[:, lora:])
        la_pre = jnp.where(fwd, la_pre[:, :lora], la_pre[:, lora:])
        x_w = w0_ref[...] + _dot(jnp.tanh(lw_pre), w2_ref[...])
        lw_all = -math.exp(-0.5) * jax.nn.sigmoid(x_w)
        a_sig = jax.nn.sigmoid(a0_ref[...] + _dot(la_pre, a2_ref[...]))
        kk = k_all * kks_ref[...]
        kk = kk * lax.rsqrt(jnp.maximum(_group_sum(kk * kk, ones_bd), 1e-24))
        kd_all = k_all * (1.0 + (a_sig - 1.0) * ka_ref[...])
        bb_all = kk * a_sig
        bonus_ref[bi] = (_group_sum(r_all * kd_all * rk_ref[...], ones_bd) * v_all).astype(bonus_ref.dtype)

        lw_hi = lw_all.astype(BF16)
        lw_lo = (lw_all - lw_hi.astype(F32)).astype(BF16)
        cum = jnp.dot(incl_2, jnp.concatenate([lw_hi, lw_lo], axis=0), preferred_element_type=F32)
        tot = jnp.sum(lw_all, axis=0, keepdims=True)
        g_inv = jnp.exp(-cum)
        g_rem = jnp.exp(tot - cum)
        prep.append(dict(v=v_all, g_tot=jnp.exp(tot), a_t=-kk * jnp.exp(cum - lw_all), r_t=r_all * jnp.exp(cum),
                         b_t=bb_all * g_inv, k_t=kd_all * g_inv, b_s=bb_all * g_rem, k_s=kd_all * g_rem))

    probs = [(bi, g) for bi in range(n_batch) for g in range(n_groups)]
    idx = range(len(probs))
    cols = [slice(g * w, (g + 1) * w) for _, g in probs]
    take = lambda name: [prep[bi][name][:, cols[q]] for q, (bi, _) in enumerate(probs)]
    v_g, a_t, r_t, b_t, k_t, b_s, k_s = (take(nm) for nm in ("v", "a_t", "r_t", "b_t", "k_t", "b_s", "k_s"))
    hts = [ht_ref[bi, g] for bi, g in probs]
    lhs = [jnp.concatenate([a_t[q], r_t[q]], axis=0) for q in idx]
    scores = [_dot(lhs[q], jnp.concatenate([bd(b_t[q]).T, bd(k_t[q]).T], axis=1)) for q in idx]
    h0 = [_dot(lhs[q], hts[q].T) for q in idx]
    a_ab = [jnp.where(strict_c, s[:c, :w], 0.0) for s in scores]
    a_ak = [jnp.where(strict_c, s[:c, w:], 0.0) for s in scores]
    a_rb = [jnp.where(incl_c, s[c:, :w], 0.0) for s in scores]
    a_rk = [jnp.where(incl_c, s[c:, w:], 0.0) for s in scores]
    bd_v = [bd(vq) for vq in v_g]
    av = [_dot(jnp.concatenate([a_ak[q], a_rk[q]], axis=0), bd_v[q]) for q in idx]
    x = [h0[q][:c] + av[q][:c] for q in idx]

    t_inv = [eye_c + a for a in a_ab]
    p = [_dot(a, bd(a)) for a in a_ab]
    n_sq = int(math.log2(c)) - 1
    for it in range(n_sq):
        bd_p = [bd(pq) for pq in p]
        if it + 1 < n_sq:
            pt = [_dot(jnp.concatenate([p[q], t_inv[q]], axis=0), bd_p[q]) for q in idx]
            p = [m[:c] for m in pt]
            t_inv = [t_inv[q] + pt[q][c:] for q in idx]
        else:
            t_inv = [t_inv[q] + _dot(t_inv[q], bd_p[q]) for q in idx]
    u = [_dot(t_inv[q], bd(x[q])) for q in idx]

    y = [h0[q][c:] + av[q][c:] + _dot(a_rb[q], bd(u[q])) for q in idx]
    for q, (bi, _) in enumerate(probs):
        y_ref[bi, :, cols[q]] = y[q].astype(y_ref.dtype)

    for q, (bi, g) in enumerate(probs):
        uv = jnp.concatenate([u[q], v_g[q]], axis=0).astype(BF16)
        bk = jnp.concatenate([b_s[q], k_s[q]], axis=0).astype(BF16)
        upd = lax.dot_general(uv, bk, (((0,), (0,)), ((), ())), preferred_element_type=F32)
        ht_ref[bi, g] = hts[q] * prep[bi]["g_tot"][:, cols[q]] + jnp.where(bd_mask, upd, 0.0)

    @pl.when(step == pl.num_programs(2) - 1)
    def _():
        sfin_ref[...] = ht_ref[...]


def _rwkv_scan(z, col0, lora_col0, w2, a2, w0, a0, kks, ka, rk, s0):
    nb, seq, _ = z.shape
    lora, ch = w2.shape[1:]
    c = CHUNK
    assert CHUNK == HEAD and ch % PACK_W == 0 and seq % c == 0 and 2 * lora == LANE
    assert col0 % ch == 0 and lora_col0 % LANE == 0
    nc = seq // c
    ng = ch // PACK_W
    bs = SCAN_BATCH if nb % SCAN_BATCH == 0 else 1
    chunk_of = lambda n, i: jnp.where(n == 0, i, nc - 1 - i)
    zcol = lambda k: pl.BlockSpec((bs, c, ch), lambda n, b, i: (b, chunk_of(n, i), col0 // ch + k))
    zlora = lambda k: pl.BlockSpec((bs, c, LANE), lambda n, b, i: (b, chunk_of(n, i), lora_col0 // LANE + k))
    per_dir_w = pl.BlockSpec((None, lora, ch), lambda n, b, i: (n, 0, 0))
    per_dir_v = pl.BlockSpec((None, 1, ch), lambda n, b, i: (n, 0, 0))
    shared_v = pl.BlockSpec((1, ch), lambda n, b, i: (0, 0))
    state = pl.BlockSpec((None, bs, ng, PACK_W, PACK_W), lambda n, b, i: (n, b, 0, 0, 0))
    out = pl.BlockSpec((None, bs, c, ch), lambda n, b, i: (n, b, chunk_of(n, i), 0))
    return pl.pallas_call(
        _scan_kernel,
        grid=(2, nb // bs, nc),
        in_specs=[zcol(0), zcol(1), zcol(2), zlora(0), zlora(1), per_dir_w, per_dir_w, per_dir_v, per_dir_v,
                  shared_v, shared_v, shared_v, state],
        out_specs=[out, out, state],
        out_shape=[jax.ShapeDtypeStruct((2, nb, seq, ch), BF16), jax.ShapeDtypeStruct((2, nb, seq, ch), BF16),
                   jax.ShapeDtypeStruct((2, nb, ng, PACK_W, PACK_W), F32)],
        scratch_shapes=[pltpu.VMEM((bs, ng, PACK_W, PACK_W), F32)],
        compiler_params=_cparams(("parallel", "parallel", "arbitrary")),
    )(z, z, z, z, z, w2, a2, w0.reshape(2, 1, ch), a0.reshape(2, 1, ch), kks, ka, rk, s0)


def _outproj_kernel(yh_ref, ys0_ref, ys1_ref, b0_ref, b1_ref, lg_ref, g2_ref, lng_ref, lnb_ref, ow_ref,
                    x_ref, gate_ref, n2g_ref, shift_ref, scale_ref, rwh_ref, rwl_ref, rb_ref,
                    x1_ref, h2_ref, lg_out_ref, mix_ref):
    hy = yh_ref.shape[1]
    tm = x_ref.shape[0]
    ones = _group_ones(PACK_W)
    rows = _tile(tm, SUB_ROWS)
    subs = [slice(r0, r0 + rows) for r0 in range(0, tm, rows)]

    for rs in subs:
        y = ys0_ref[rs].astype(F32) + ys1_ref[rs].astype(F32)
        mu = _group_sum(y, ones) * (1.0 / HEAD)
        yc = y - mu
        var = _group_sum(yc * yc, ones) * (1.0 / HEAD)
        y = yc * lax.rsqrt(var + LNX_EPS) * lng_ref[...] + lnb_ref[...]
        gate = _dot(jax.nn.sigmoid(lg_ref[rs].astype(F32)), g2_ref[...])
        y = (y + b0_ref[rs].astype(F32) + b1_ref[rs].astype(F32)) * gate
        mix_ref[rs, :hy] = yh_ref[rs]
        mix_ref[rs, hy:] = y.astype(BF16)
    mixes = [jnp.dot(mix_ref[rs], ow_ref[...], preferred_element_type=F32) for rs in subs]
    for rs, mix in zip(subs, mixes):
        x1 = x_ref[rs] + gate_ref[...] * mix
        x1_ref[rs] = x1
        h2 = x1 * lax.rsqrt(jnp.mean(x1 * x1, axis=-1, keepdims=True) + NORM_EPS) * n2g_ref[...]
        h2 = h2 * (1.0 + scale_ref[...]) + shift_ref[...]
        h2_ref[rs] = h2.astype(BF16)
        h_hi = h2.astype(BF16)
        h_lo = (h2 - h_hi.astype(F32)).astype(BF16)
        logits = (jnp.dot(h_hi, rwh_ref[...], preferred_element_type=F32)
                  + jnp.dot(h_hi, rwl_ref[...], preferred_element_type=F32)
                  + jnp.dot(h_lo, rwh_ref[...], preferred_element_type=F32))
        lg_out_ref[rs] = logits + rb_ref[...]


def _outproj(y_hy, ys, bonus, z, lora_g_col, g2, lnx_g, lnx_b, out_w, x, gate, n2g, shift, scale,
             rw_hi, rw_lo, rb, tm):
    nb, seq, d = x.shape
    m = nb * seq
    hy = y_hy.shape[2]
    rw = ys.shape[3]
    tm = _tile(seq, tm)
    lg = g2.shape[0]
    ne = rw_hi.shape[1]
    assert lora_g_col % lg == 0
    ys2 = ys.reshape(2, m, rw)
    bn2 = bonus.reshape(2, m, rw)
    nz = z.shape[2]
    full = lambda r, c: pl.BlockSpec((r, c), lambda i: (0, 0))
    dir_spec = lambda n: pl.BlockSpec((None, tm, rw), lambda i: (n, i, 0))
    mod_spec = pl.BlockSpec((None, 1, d), lambda i: ((i * tm) // seq, 0, 0))
    return pl.pallas_call(
        _outproj_kernel,
        grid=(m // tm,),
        in_specs=[pl.BlockSpec((tm, hy), lambda i: (i, 0)), dir_spec(0), dir_spec(1), dir_spec(0), dir_spec(1),
                  pl.BlockSpec((tm, lg), lambda i: (i, lora_g_col // lg)),
                  full(lg, rw), full(1, rw), full(1, rw), full(hy + rw, d),
                  pl.BlockSpec((tm, d), lambda i: (i, 0)), mod_spec, full(1, d), mod_spec, mod_spec,
                  full(d, ne), full(d, ne), full(1, ne)],
        out_specs=[pl.BlockSpec((tm, d), lambda i: (i, 0)), pl.BlockSpec((tm, d), lambda i: (i, 0)),
                   pl.BlockSpec((tm, ne), lambda i: (i, 0))],
        out_shape=[jax.ShapeDtypeStruct((m, d), F32), jax.ShapeDtypeStruct((m, d), BF16),
                   jax.ShapeDtypeStruct((m, ne), F32)],
        scratch_shapes=[pltpu.VMEM((tm, hy + rw), BF16)],
        compiler_params=_cparams(("parallel",)),
    )(y_hy.reshape(m, hy), ys2, ys2, bn2, bn2, z.reshape(m, nz), g2, lnx_g, lnx_b, out_w,
      x.reshape(m, d), gate, n2g, shift, scale, rw_hi, rw_lo, rb)


def _expert_changed(te_ref, i):
    prev = te_ref[jnp.maximum(i - 1, 0)]
    return jnp.logical_or(i == 0, te_ref[i] != prev)


def _moe_up_kernel(te_ref, nu_ref, x_ref, wg_ref, wu_ref, bg_ref, bu_ref, h_ref, wg_bf, wu_bf):
    i = pl.program_id(1)

    @pl.when(_expert_changed(te_ref, i))
    def _():
        wg_bf[...] = wg_ref[...].astype(BF16)
        wu_bf[...] = wu_ref[...].astype(BF16)

    @pl.when(i < nu_ref[0])
    def _():
        x = x_ref[...]
        g = jnp.dot(x, wg_bf[...], preferred_element_type=F32) + bg_ref[...]
        u = jnp.dot(x, wu_bf[...], preferred_element_type=F32) + bu_ref[...]
        g = jnp.minimum(g, SWIGLU_LIMIT)
        u = jnp.clip(u, -SWIGLU_LIMIT, SWIGLU_LIMIT)
        h_ref[...] = ((u + 1.0) * (g * jax.nn.sigmoid(SWIGLU_ALPHA * g))).astype(h_ref.dtype)

    @pl.when(i >= nu_ref[0])
    def _():
        h_ref[...] = jnp.zeros_like(h_ref)


def _moe_down_kernel(te_ref, nu_ref, h_ref, wd_ref, bd_ref, *rest):
    y_ref, wd_bf = rest[-2:]
    i = pl.program_id(1)

    @pl.when(_expert_changed(te_ref, i))
    def _():
        wd_bf[...] = wd_ref[...].astype(BF16)

    @pl.when(i < nu_ref[0])
    def _():
        y = jnp.dot(h_ref[...], wd_bf[...], preferred_element_type=F32) + bd_ref[...]
        y_ref[...] = y.astype(y_ref.dtype)

    @pl.when(i >= nu_ref[0])
    def _():
        y_ref[...] = jnp.zeros_like(y_ref)


def _moe_up(tile_expert, n_used, xs, wg, wu, bg, bu, tn):
    n_rows, d = xs.shape
    e, _, f = wg.shape
    tm = MOE_TILE
    tn = _tile(f, tn)
    w_spec = pl.BlockSpec((None, d, tn), lambda j, i, te, nu: (te[i], 0, j))
    b_spec = pl.BlockSpec((None, 1, tn), lambda j, i, te, nu: (te[i], 0, j))
    return pl.pallas_call(
        _moe_up_kernel,
        grid_spec=pltpu.PrefetchScalarGridSpec(
            num_scalar_prefetch=2,
            grid=(f // tn, n_rows // tm),
            in_specs=[pl.BlockSpec((tm, d), lambda j, i, te, nu: (i, 0)), w_spec, w_spec, b_spec, b_spec],
            out_specs=pl.BlockSpec((tm, tn), lambda j, i, te, nu: (i, j)),
            scratch_shapes=[pltpu.VMEM((d, tn), BF16), pltpu.VMEM((d, tn), BF16)]),
        out_shape=jax.ShapeDtypeStruct((n_rows, f), BF16),
        compiler_params=_cparams(("arbitrary", "arbitrary")),
    )(tile_expert, n_used, xs, wg, wu, bg.reshape(e, 1, f), bu.reshape(e, 1, f))


def _moe_down(tile_expert, n_used, hs, wd, bdn, tn, total_rows, tile0, ys_prev):
    n_rows, f = hs.shape
    e, _, d = wd.shape
    tm = MOE_TILE
    tn = _tile(d, tn)
    in_specs = [pl.BlockSpec((tm, f), lambda j, i, te, nu: (i, 0)),
                pl.BlockSpec((None, f, tn), lambda j, i, te, nu: (te[i], 0, j)),
                pl.BlockSpec((None, 1, tn), lambda j, i, te, nu: (te[i], 0, j))]
    args = [tile_expert, n_used, hs, wd, bdn.reshape(e, 1, d)]
    aliases = {}
    if ys_prev is not None:
        in_specs.append(pl.BlockSpec(memory_space=pl.ANY))
        args.append(ys_prev)
        aliases = {len(args) - 1: 0}
    return pl.pallas_call(
        _moe_down_kernel,
        grid_spec=pltpu.PrefetchScalarGridSpec(
            num_scalar_prefetch=2,
            grid=(d // tn, n_rows // tm),
            in_specs=in_specs,
            out_specs=pl.BlockSpec((tm, tn), lambda j, i, te, nu: (tile0 + i, j)),
            scratch_shapes=[pltpu.VMEM((f, tn), BF16)]),
        out_shape=jax.ShapeDtypeStruct((total_rows, d), BF16),
        input_output_aliases=aliases,
        compiler_params=_cparams(("arbitrary", "arbitrary")),
    )(*args)


def _moe(h2, logits, wg, bg, wu, bu, wd, bdn):
    n_tok, d = h2.shape
    e = logits.shape[1]
    tm = MOE_TILE
    top_logits, top_idx = lax.top_k(logits, TOP_K)
    gates = jax.nn.softmax(top_logits, axis=-1)
    n_asg = n_tok * TOP_K
    e_flat = top_idx.reshape(n_asg).astype(jnp.int32)
    asg = jnp.arange(n_asg, dtype=jnp.int32)
    e_sorted, order = lax.sort((e_flat, asg), num_keys=1, is_stable=True)
    counts = jnp.sum((e_flat[:, None] == jnp.arange(e, dtype=jnp.int32)[None, :]).astype(jnp.int32), axis=0)
    raw_start = jnp.cumsum(counts) - counts
    padded = (counts + tm - 1) // tm * tm
    pad_end = jnp.cumsum(padded)
    pad_start = pad_end - padded
    dest_sorted = (pad_start - raw_start)[e_sorted] + asg
    _, dest = lax.sort((order, dest_sorted), num_keys=1)
    n_tiles = -(-(n_asg + e * (tm - 1)) // tm)
    n_rows = n_tiles * tm
    tile_start = jnp.arange(n_tiles, dtype=jnp.int32) * tm
    tile_expert = jnp.minimum(jnp.sum((pad_end[None, :] <= tile_start[:, None]).astype(jnp.int32), axis=1), e - 1)
    n_used = pad_end[-1] // tm
    row = jnp.arange(n_rows, dtype=jnp.int32)
    row_e = jnp.repeat(tile_expert, tm)
    rank = row - pad_start[row_e]
    src = jnp.clip(raw_start[row_e] + rank, 0, n_asg - 1)
    row_tok = jnp.where(rank < counts[row_e], order[src] // TOP_K, 0)

    n_parts = next(p for p in (MOE_PARTS, 2, 1) if n_tiles % p == 0)
    tp = n_tiles // n_parts
    ys = None
    for q in range(n_parts):
        te_q = tile_expert[q * tp:(q + 1) * tp]
        nu_q = jnp.clip(n_used - q * tp, 0, tp).astype(jnp.int32).reshape(1)
        xs = h2[row_tok[q * tp * tm:(q + 1) * tp * tm]]
        hs = _moe_up(te_q, nu_q, xs, wg, wu, bg, bu, 1024)
        ys = _moe_down(te_q, nu_q, hs, wd, bdn, 1024, n_rows, q * tp, ys)
    return ys, dest.reshape(n_tok, TOP_K).T, gates


def _combine_kernel(y_ref, g_ref, x1_ref, gate_ref, fg_ref, *rest):
    o_ref = rest[-1]
    g = g_ref[...]
    acc = y_ref[0].astype(F32) * g[:, 0:1]
    for k in range(1, y_ref.shape[0]):
        acc = acc + y_ref[k].astype(F32) * g[:, k:k + 1]
    xo = x1_ref[...] + gate_ref[...] * acc
    o_ref[...] = xo * lax.rsqrt(jnp.mean(xo * xo, axis=-1, keepdims=True) + NORM_EPS) * fg_ref[...]


def _combine(picked, gates, x1, gate, final_g, seq, tm, tile0, out_prev):
    nk, m, d = picked.shape
    total = x1.shape[0]
    in_specs = [pl.BlockSpec((nk, tm, d), lambda i: (0, i, 0)),
                pl.BlockSpec((tm, nk), lambda i: (tile0 + i, 0)),
                pl.BlockSpec((tm, d), lambda i: (tile0 + i, 0)),
                pl.BlockSpec((None, 1, d), lambda i: (((tile0 + i) * tm) // seq, 0, 0)),
                pl.BlockSpec((1, d), lambda i: (0, 0))]
    args = [picked, gates, x1, gate, final_g]
    aliases = {}
    if out_prev is not None:
        in_specs.append(pl.BlockSpec(memory_space=pl.ANY))
        args.append(out_prev)
        aliases = {len(args) - 1: 0}
    return pl.pallas_call(
        _combine_kernel,
        grid=(m // tm,),
        in_specs=in_specs,
        out_specs=pl.BlockSpec((tm, d), lambda i: (tile0 + i, 0)),
        out_shape=jax.ShapeDtypeStruct((total, d), F32),
        input_output_aliases=aliases,
        compiler_params=_cparams(("parallel",)),
    )(*args)


def _rmsnorm(x, g):
    return x * lax.rsqrt(jnp.mean(x * x, axis=-1, keepdims=True) + NORM_EPS) * g


def kernel(x, c, ctx, c_ctx, ada_w, ada_b, norm1_g, norm2_g, in_w, conv_w, conv_b, hy_w1, hy_b1, hy_w2, hy_b2, hy_w3, hy_b3, hy_freq, hy_w4, hy_bias, hy_norm_g, rw_w0, rw_w1, rw_w2, rw_a0, rw_a1, rw_a2, rw_kk, rw_ka, rw_rk, rw_g1, rw_g2, rw_lnx_g, rw_lnx_b, out_w, router_w, router_b, ex_w_gate, ex_b_gate, ex_w_up, ex_b_up, ex_w_down, ex_b_down, final_g):
    assert ada_w.shape[0] == 1, "single-layer block: context outputs never reach a latent token"
    nb, seq, d = x.shape
    hy = hy_bias.shape[1]
    rw = rw_kk.shape[1]
    n_hy = 3 * hy
    proj = in_w.shape[2]
    n_exp = router_w.shape[2]

    cond = jnp.concatenate([jax.nn.silu(c), jax.nn.silu(c_ctx)[None]], axis=0)
    mod = _matmul(cond, ada_w[0], F32, cond.shape[0], 1024) + ada_b[0]
    mod_x = [m[:, None, :] for m in jnp.split(mod[:nb], 6, axis=-1)]
    mod_c = [m[:, None, :] for m in jnp.split(mod[nb:], 6, axis=-1)]

    lora_w = jnp.concatenate([rw_w1[0, 0], rw_w1[0, 1], rw_a1[0, 0], rw_a1[0, 1], rw_g1[0]], axis=1)
    n_lora = lora_w.shape[1]
    nz = -(-(proj + n_lora) // 512) * 512
    pad = nz - proj - n_lora
    w_all = jnp.concatenate([in_w[0], lora_w, jnp.zeros((d, pad), F32)], axis=1).astype(BF16)
    pass_taps = jnp.concatenate([jnp.zeros((1, n_lora + pad), F32), jnp.ones((1, n_lora + pad), F32),
                                 jnp.zeros((1, n_lora + pad), F32)], axis=0)
    cw_all = jnp.concatenate([conv_w[0], pass_taps], axis=1)
    cb_all = jnp.concatenate([conv_b[0], jnp.zeros((n_lora + pad,), F32)])[None]
    g1 = norm1_g[0][None]
    zx = _inproj(x, mod_x[0], mod_x[1], g1, w_all, cw_all, cb_all, 1024, 512)
    zc = _inproj(ctx, mod_c[0], mod_c[1], g1, w_all[:, n_hy:], cw_all[:, n_hy:], cb_all[:, n_hy:], 256, 512)

    h_fwd, h_bwd = _hyena_filter(seq, hy_w1[0], hy_b1[0], hy_w2[0], hy_b2[0], hy_w3[0], hy_b3[0],
                                 hy_freq[0], hy_w4[0])
    fwd, inv = _dft_matrices(seq)
    taps = _taps_spectrum(fwd, h_fwd, h_bwd)
    spec = _dft_fwd(fwd, zx, hy, 1024, 512)
    y_hy = _dft_inv(inv, spec, taps, zx, hy_bias[0][None], hy_norm_g[0][None], 1024, 512)

    scan_args = (rw_w2[0].astype(BF16), rw_a2[0].astype(BF16), rw_w0[0], rw_a0[0],
                 rw_kk[0][None], rw_ka[0][None], rw_rk[0].reshape(1, rw))
    s0 = jnp.zeros((2, nb, rw // PACK_W, PACK_W, PACK_W), F32)
    _, _, s_ctx = _rwkv_scan(zc, 0, proj - n_hy, *scan_args, s0)
    ys, bonus, _ = _rwkv_scan(zx, n_hy, proj, *scan_args, s_ctx)

    ne = -(-n_exp // LANE) * LANE
    rw_pad = jnp.pad(router_w[0], ((0, 0), (0, ne - n_exp)))
    rw_hi = rw_pad.astype(BF16)
    rw_lo = (rw_pad - rw_hi.astype(F32)).astype(BF16)
    rb = jnp.pad(router_b[0], (0, ne - n_exp))[None]
    x1, h2, logits = _outproj(y_hy, ys, bonus, zx, proj + n_lora - rw_g1.shape[-1], rw_g2[0].astype(BF16),
                              rw_lnx_g[0][None], rw_lnx_b[0][None], out_w[0].astype(BF16), x,
                              mod_x[2], norm2_g[0][None], mod_x[3], mod_x[4], rw_hi, rw_lo, rb, 512)

    ys, dest_km, gates = _moe(h2, logits[:, :n_exp], ex_w_gate[0], ex_b_gate[0], ex_w_up[0], ex_b_up[0],
                              ex_w_down[0], ex_b_down[0])
    n_tok = nb * seq
    tm = _tile(seq, 256)
    n_parts = next(p for p in (COMBINE_PARTS, 2, 1) if (n_tok // tm) % p == 0)
    tq = n_tok // n_parts
    out = None
    for q in range(n_parts):
        picked = ys[dest_km[:, q * tq:(q + 1) * tq]]
        out = _combine(picked, gates, x1, mod_x[5], final_g[None], seq, tm, q * (tq // tm), out)
    return out.reshape(nb, seq, d)
```

```python
import functools
import math

import jax
import jax.numpy as jnp
from jax import lax
from jax.experimental import pallas as pl
from jax.experimental.pallas import tpu as pltpu

F32 = jnp.float32
BF16 = jnp.bfloat16

HEAD = 64
CHUNK = 64
PACK = 4
PACK_W = PACK * HEAD
SCAN_BATCH = 4
HALO = 16
INPROJ_ROWS = 256
SUB_ROWS = 256
FILTER_BANDS = 16
FILTER_TARGET = 1e-2
FAST_DECAY_PCT = 0.3
SLOW_DECAY_PCT = 1.5
TOP_K = 4
SWIGLU_LIMIT = 7.0
SWIGLU_ALPHA = 1.702
NORM_EPS = 1e-6
LNX_EPS = 64e-5
MOE_TILE = 512
ROUTE_TILE = 512
MOE_PARTS = 4
COMBINE_PARTS = 4
LANE = 128
VMEM_LIMIT = 56 * 1024 * 1024


def _cparams(sem):
    return pltpu.CompilerParams(dimension_semantics=sem, vmem_limit_bytes=VMEM_LIMIT)


def _tile(n, want):
    if n <= want:
        return n
    t = want
    while n % t:
        t //= 2
    assert t >= 8, (n, want)
    return t


def _dot(a, b):
    return jnp.dot(a.astype(BF16), b.astype(BF16), preferred_element_type=F32)


def _group_ones(width):
    r = lax.broadcasted_iota(jnp.int32, (width, width), 0) // HEAD
    c = lax.broadcasted_iota(jnp.int32, (width, width), 1) // HEAD
    return jnp.where(r == c, 1.0, 0.0).astype(BF16)


def _group_sum(x, ones):
    width = ones.shape[0]
    parts = [_dot(x[:, o:o + width], ones) for o in range(0, x.shape[1], width)]
    return parts[0] if len(parts) == 1 else jnp.concatenate(parts, axis=1)


def _mm_kernel(a_ref, b_ref, o_ref):
    o_ref[...] = _dot(a_ref[...], b_ref[...]).astype(o_ref.dtype)


def _matmul(a, b, out_dtype, tm, tn):
    m, k = a.shape
    n = b.shape[1]
    tm, tn = _tile(m, tm), _tile(n, tn)
    return pl.pallas_call(
        _mm_kernel,
        grid=(n // tn, m // tm),
        in_specs=[pl.BlockSpec((tm, k), lambda j, i: (i, 0)),
                  pl.BlockSpec((k, tn), lambda j, i: (0, j))],
        out_specs=pl.BlockSpec((tm, tn), lambda j, i: (i, j)),
        out_shape=jax.ShapeDtypeStruct((m, n), out_dtype),
        compiler_params=_cparams(("parallel", "parallel")),
    )(a, b)


def _bmm_shared_lhs(a, b, out_dtype, tm, tn):
    m, k = a.shape
    nb, _, n = b.shape
    tm, tn = _tile(m, tm), _tile(n, tn)
    return pl.pallas_call(
        _mm_kernel,
        grid=(m // tm, nb, n // tn),
        in_specs=[pl.BlockSpec((tm, k), lambda i, bb, j: (i, 0)),
                  pl.BlockSpec((None, k, tn), lambda i, bb, j: (bb, 0, j))],
        out_specs=pl.BlockSpec((None, tm, tn), lambda i, bb, j: (bb, i, j)),
        out_shape=jax.ShapeDtypeStruct((nb, m, n), out_dtype),
        compiler_params=_cparams(("parallel", "parallel", "parallel")),
    )(a, b)


def _inproj_kernel(seq, xm_ref, xp_ref, xn_ref, shift_ref, scale_ref, g_ref, w_ref, cw_ref, cb_ref,
                   z_ref, h_ref, zz_ref):
    tm = xm_ref.shape[0]
    i = pl.program_id(0)

    @pl.when(pl.program_id(1) == 0)
    def _():
        def norm(xv):
            y = xv * lax.rsqrt(jnp.mean(xv * xv, axis=-1, keepdims=True) + NORM_EPS) * g_ref[...]
            return y * (1.0 + scale_ref[...]) + shift_ref[...]
        keep_prev = jnp.where((i * tm) % seq == 0, 0.0, 1.0)
        keep_next = jnp.where(((i + 1) * tm) % seq == 0, 0.0, 1.0)
        h_ref[0:HALO] = (norm(xp_ref[...]) * keep_prev).astype(BF16)
        h_ref[HALO:HALO + tm] = norm(xm_ref[...]).astype(BF16)
        h_ref[HALO + tm:] = (norm(xn_ref[...]) * keep_next).astype(BF16)

    cw = cw_ref[...]
    n_chunks = max(1, tm // INPROJ_ROWS)
    q = tm // n_chunks
    d0 = o0 = 0
    for ci in range(n_chunks):
        last = ci + 1 == n_chunks
        d1 = tm + 2 * HALO if last else (ci + 1) * q + HALO
        o1 = tm if last else d1 - 2 * HALO
        zz_ref[d0:d1] = jnp.dot(h_ref[d0:d1], w_ref[...], preferred_element_type=F32)
        z_ref[o0:o1] = (zz_ref[HALO - 1 + o0:HALO - 1 + o1] * cw[0:1] + zz_ref[HALO + o0:HALO + o1] * cw[1:2]
                        + zz_ref[HALO + 1 + o0:HALO + 1 + o1] * cw[2:3] + cb_ref[...]).astype(z_ref.dtype)
        d0, o0 = d1, o1


def _inproj(x, shift, scale, g, w, cw, cb, tm, tn):
    nb, seq, d = x.shape
    n = w.shape[1]
    tm, tn = _tile(seq, tm), _tile(n, tn)
    m = nb * seq
    n_halo = m // HALO
    per_batch = shift.shape[0] > 1
    mod_spec = pl.BlockSpec((None, 1, d), (lambda i, j: ((i * tm) // seq, 0, 0)) if per_batch
                            else (lambda i, j: (0, 0, 0)))
    x2 = x.reshape(m, d)
    out = pl.pallas_call(
        functools.partial(_inproj_kernel, seq),
        grid=(m // tm, n // tn),
        in_specs=[pl.BlockSpec((tm, d), lambda i, j: (i, 0)),
                  pl.BlockSpec((HALO, d), lambda i, j: (jnp.maximum(i * (tm // HALO) - 1, 0), 0)),
                  pl.BlockSpec((HALO, d), lambda i, j: (jnp.minimum((i + 1) * (tm // HALO), n_halo - 1), 0)),
                  mod_spec, mod_spec,
                  pl.BlockSpec((1, d), lambda i, j: (0, 0)),
                  pl.BlockSpec((d, tn), lambda i, j: (0, j)),
                  pl.BlockSpec((3, tn), lambda i, j: (0, j)),
                  pl.BlockSpec((1, tn), lambda i, j: (0, j))],
        out_specs=pl.BlockSpec((tm, tn), lambda i, j: (i, j)),
        out_shape=jax.ShapeDtypeStruct((m, n), BF16),
        scratch_shapes=[pltpu.VMEM((tm + 2 * HALO, d), BF16), pltpu.VMEM((tm + 2 * HALO, tn), F32)],
        compiler_params=_cparams(("parallel", "arbitrary")),
    )(x2, x2, x2, shift, scale, g, w, cw, cb)
    return out.reshape(nb, seq, n)


def _dft_matrices(length):
    n = 2 * length
    f = lax.broadcasted_iota(jnp.int32, (length, length), 0)
    t = lax.broadcasted_iota(jnp.int32, (length, length), 1)
    ang = ((f * t) % n).astype(F32) * (2.0 * math.pi / n)
    cos, sin = jnp.cos(ang), jnp.sin(ang)
    nyq = jnp.where(t % 2 == 0, 1.0, -1.0)
    fwd = jnp.concatenate([cos, jnp.where(f == 0, nyq, -sin)], axis=0)
    f_t = f.T
    inv = jnp.concatenate([jnp.where(f_t == 0, 1.0, 2.0 * cos.T),
                           jnp.where(f_t == 0, nyq.T, -2.0 * sin.T)], axis=1)
    return fwd.astype(BF16), inv.astype(BF16)


def _dft_fwd_kernel(f_ref, x1_ref, v_ref, o_ref, u_ref):
    @pl.when(pl.program_id(2) == 0)
    def _():
        u_ref[...] = (x1_ref[...].astype(F32) * v_ref[...].astype(F32)).astype(BF16)

    o_ref[...] = jnp.dot(f_ref[...], u_ref[...], preferred_element_type=F32).astype(o_ref.dtype)


def _dft_fwd(fwd, z, hy, tm, tn):
    nb, seq, _ = z.shape
    tm, tn = _tile(2 * seq, tm), _tile(hy, tn)
    nj = hy // tn
    return pl.pallas_call(
        _dft_fwd_kernel,
        grid=(nb, nj, 2 * seq // tm),
        in_specs=[pl.BlockSpec((tm, seq), lambda b, j, i: (i, 0)),
                  pl.BlockSpec((None, seq, tn), lambda b, j, i: (b, 0, nj + j)),
                  pl.BlockSpec((None, seq, tn), lambda b, j, i: (b, 0, 2 * nj + j))],
        out_specs=pl.BlockSpec((None, tm, tn), lambda b, j, i: (b, i, j)),
        out_shape=jax.ShapeDtypeStruct((nb, 2 * seq, hy), BF16),
        scratch_shapes=[pltpu.VMEM((seq, tn), BF16)],
        compiler_params=_cparams(("parallel", "parallel", "arbitrary")),
    )(fwd, z, z)


def _dft_inv_kernel(g_ref, s_ref, t_ref, x0_ref, x1_ref, v_ref, bias_ref, ng_ref, o_ref, y_ref):
    half = s_ref.shape[0] // 2

    @pl.when(pl.program_id(2) == 0)
    def _():
        re = s_ref[:half].astype(F32)
        im = s_ref[half:].astype(F32)
        t_re = t_ref[:half].astype(F32)
        t_im = t_ref[half:].astype(F32)
        row0 = lax.broadcasted_iota(jnp.int32, (half, 1), 0) == 0
        y_ref[:half] = (re * t_re - jnp.where(row0, 0.0, im * t_im)).astype(BF16)
        y_ref[half:] = (im * jnp.where(row0, t_im, t_re) + jnp.where(row0, 0.0, re * t_im)).astype(BF16)

    tm, tn = o_ref.shape
    ones = _group_ones(min(tn, PACK_W))
    rows = _tile(tm, SUB_ROWS)
    for r0 in range(0, tm, rows):
        rs = slice(r0, r0 + rows)
        conv = jnp.dot(g_ref[rs], y_ref[...], preferred_element_type=F32)
        u = x1_ref[rs].astype(F32) * v_ref[rs].astype(F32)
        y = x0_ref[rs].astype(F32) * (conv + bias_ref[...] * u)
        ms = _group_sum(y * y, ones) * (1.0 / HEAD)
        o_ref[rs] = (y * lax.rsqrt(ms + NORM_EPS) * ng_ref[...]).astype(o_ref.dtype)


def _dft_inv(inv, spec, taps, z, bias, norm_g, tm, tn):
    nb, seq, _ = z.shape
    hy = spec.shape[2]
    tm, tn = _tile(seq, tm), _tile(hy, tn)
    nj = hy // tn
    row = lambda k: pl.BlockSpec((None, tm, tn), lambda b, j, i: (b, i, k * nj + j))
    vec = pl.BlockSpec((1, tn), lambda b, j, i: (0, j))
    return pl.pallas_call(
        _dft_inv_kernel,
        grid=(nb, nj, seq // tm),
        in_specs=[pl.BlockSpec((tm, 2 * seq), lambda b, j, i: (i, 0)),
                  pl.BlockSpec((None, 2 * seq, tn), lambda b, j, i: (b, 0, j)),
                  pl.BlockSpec((2 * seq, tn), lambda b, j, i: (0, j)),
                  row(0), row(1), row(2), vec, vec],
        out_specs=pl.BlockSpec((None, tm, tn), lambda b, j, i: (b, i, j)),
        out_shape=jax.ShapeDtypeStruct((nb, seq, hy), BF16),
        scratch_shapes=[pltpu.VMEM((2 * seq, tn), BF16)],
        compiler_params=_cparams(("parallel", "parallel", "arbitrary")),
    )(inv, spec, taps, z, z, z, bias, norm_g)


def _hyena_filter(length, w1, b1, w2, b2, w3, b3, freq, w4):
    hy = w4.shape[1] // 2
    t = jnp.linspace(0.0, 1.0, length, dtype=F32)[:, None]
    ang = (2.0 * math.pi / length) * jnp.arange(length, dtype=F32)[:, None]
    bands = jnp.linspace(1e-4, FILTER_BANDS - 1, FILTER_BANDS, dtype=F32)[None, :]
    feats = jnp.concatenate([t, jnp.cos(bands * ang), -jnp.sin(bands * ang)], axis=-1)
    hp = lax.Precision.HIGHEST
    h = jnp.sin(freq * (jnp.dot(feats, w1, precision=hp) + b1))
    h = jnp.sin(freq * (jnp.dot(h, w2, precision=hp) + b2))
    h = jnp.sin(freq * (jnp.dot(h, w3, precision=hp) + b3))
    h = jnp.dot(h, w4, precision=hp)
    deltas = jnp.abs(jnp.linspace(math.log(FILTER_TARGET) / SLOW_DECAY_PCT,
                                  math.log(FILTER_TARGET) / FAST_DECAY_PCT, hy, dtype=F32))
    h = h * jnp.exp(-t * jnp.tile(deltas, 2))
    return h[:, :hy], h[:, hy:]


def _taps_spectrum(fwd, h_fwd, h_bwd):
    length = h_fwd.shape[0]
    sig = jnp.stack([h_fwd, h_bwd.at[0].set(0.0)]).astype(BF16)
    spec = _bmm_shared_lhs(fwd, sig, F32, 1024, 1024)
    re, im = spec[:, :length], spec[:, length:]
    scale = 1.0 / (2 * length)
    row0 = lax.broadcasted_iota(jnp.int32, (length, 1), 0) == 0
    t_re = (re[0] + re[1]) * scale
    t_im = jnp.where(row0, im[0] + im[1], im[0] - im[1]) * scale
    return jnp.concatenate([t_re, t_im], axis=0).astype(BF16)


def _scan_kernel(r_ref, k_ref, v_ref, lw_ref, la_ref, w2_ref, a2_ref, w0_ref, a0_ref, kks_ref, ka_ref, rk_ref,
                 s0_ref, y_ref, bonus_ref, sfin_ref, ht_ref):
    c = CHUNK
    w = PACK_W
    direction = pl.program_id(0)
    step = pl.program_id(2)
    n_batch = r_ref.shape[0]
    n_groups = r_ref.shape[2] // w
    lora = w2_ref.shape[0]

    @pl.when(step == 0)
    def _():
        ht_ref[...] = s0_ref[...]

    sign = 1 - 2 * direction
    d_sq = (lax.broadcasted_iota(jnp.int32, (c, c), 1) - lax.broadcasted_iota(jnp.int32, (c, c), 0)) * sign
    incl_sq = jnp.where(d_sq <= 0, 1.0, 0.0).astype(BF16)
    d_c = (lax.broadcasted_iota(jnp.int32, (c, w), 1) % c - lax.broadcasted_iota(jnp.int32, (c, w), 0)) * sign
    strict_c = d_c < 0
    incl_c = d_c <= 0
    eye_c = jnp.where(d_c == 0, 1.0, 0.0)
    bd_mask = (lax.broadcasted_iota(jnp.int32, (w, w), 0) // c) == (lax.broadcasted_iota(jnp.int32, (w, w), 1) // HEAD)
    ones_bd = jnp.where(bd_mask, 1.0, 0.0).astype(BF16)

    def bd(x):
        xb = x.astype(BF16)
        return jnp.where(bd_mask, jnp.concatenate([xb] * PACK, axis=0), jnp.zeros((), BF16))

    def mm_nt(a, b):
        return lax.dot_general(a.astype(BF16), b.astype(BF16), (((1,), (1,)), ((), ())),
                               preferred_element_type=F32)

    fwd = direction == 0
    incl_2 = jnp.concatenate([incl_sq, incl_sq], axis=1)
    prep = []
    for bi in range(n_batch):
        r_all = r_ref[bi].astype(F32)
        k_all = k_ref[bi].astype(F32)
        v_all = v_ref[bi].astype(F32)
        lw_pre = lw_ref[bi].astype(F32)
        la_pre = la_ref[bi].astype(F32)
        lw_pre = jnp.where(fwd, lw_pre[:, :lora], lw_pre[:, lora:])
        la_pre = jnp.where(fwd, la_pre[:, :lora], la_pre[:, lora:])
        x_w = w0_ref[...] + _dot(jnp.tanh(lw_pre), w2_ref[...])
        lw_all = -math.exp(-0.5) * jax.nn.sigmoid(x_w)
        a_sig = jax.nn.sigmoid(a0_ref[...] + _dot(la_pre, a2_ref[...]))
        kk = k_all * kks_ref[...]
        kk = kk * lax.rsqrt(jnp.maximum(_group_sum(kk * kk, ones_bd), 1e-24))
        kd_all = k_all * (1.0 + (a_sig - 1.0) * ka_ref[...])
        bb_all = kk * a_sig
        bonus_ref[bi] = (_group_sum(r_all * kd_all * rk_ref[...], ones_bd) * v_all).astype(bonus_ref.dtype)

        lw_hi = lw_all.astype(BF16)
        lw_lo = (lw_all - lw_hi.astype(F32)).astype(BF16)
        cum = jnp.dot(incl_2, jnp.concatenate([lw_hi, lw_lo], axis=0), preferred_element_type=F32)
        tot = jnp.sum(lw_all, axis=0, keepdims=True)
        g_inv = jnp.exp(-cum)
        g_rem = jnp.exp(tot - cum)
        prep.append(dict(v=v_all, g_tot=jnp.exp(tot), a_t=-kk * jnp.exp(cum - lw_all), r_t=r_all * jnp.exp(cum),
                         b_t=bb_all * g_inv, k_t=kd_all * g_inv, b_s=bb_all * g_rem, k_s=kd_all * g_rem))

    probs = [(bi, g) for bi in range(n_batch) for g in range(n_groups)]
    idx = range(len(probs))
    cols = [slice(g * w, (g + 1) * w) for _, g in probs]
    take = lambda name: [prep[bi][name][:, cols[q]] for q, (bi, _) in enumerate(probs)]
    v_g, a_t, r_t, b_t, k_t, b_s, k_s = (take(nm) for nm in ("v", "a_t", "r_t", "b_t", "k_t", "b_s", "k_s"))
    hts = [ht_ref[bi, g] for bi, g in probs]
    lhs = [jnp.concatenate([a_t[q], r_t[q]], axis=0) for q in idx]
    scores = [_dot(lhs[q], jnp.concatenate([bd(b_t[q]).T, bd(k_t[q]).T], axis=1)) for q in idx]
    h0 = [_dot(lhs[q], hts[q].T) for q in idx]
    a_ab = [jnp.where(strict_c, s[:c, :w], 0.0) for s in scores]
    a_ak = [jnp.where(strict_c, s[:c, w:], 0.0) for s in scores]
    a_rb = [jnp.where(incl_c, s[c:, :w], 0.0) for s in scores]
    a_rk = [jnp.where(incl_c, s[c:, w:], 0.0) for s in scores]
    bd_v = [bd(vq) for vq in v_g]
    av = [_dot(jnp.concatenate([a_ak[q], a_rk[q]], axis=0), bd_v[q]) for q in idx]
    x = [h0[q][:c] + av[q][:c] for q in idx]

    t_inv = [eye_c + a for a in a_ab]
    p = [_dot(a, bd(a)) for a in a_ab]
    n_sq = int(math.log2(c)) - 1
    for it in range(n_sq):
        bd_p = [bd(pq) for pq in p]
        if it + 1 < n_sq:
            pt = [_dot(jnp.concatenate([p[q], t_inv[q]], axis=0), bd_p[q]) for q in idx]
            p = [m[:c] for m in pt]
            t_inv = [t_inv[q] + pt[q][c:] for q in idx]
        else:
            t_inv = [t_inv[q] + _dot(t_inv[q], bd_p[q]) for q in idx]
    u = [_dot(t_inv[q], bd(x[q])) for q in idx]

    y = [h0[q][c:] + av[q][c:] + _dot(a_rb[q], bd(u[q])) for q in idx]
    for q, (bi, _) in enumerate(probs):
        y_ref[bi, :, cols[q]] = y[q].astype(y_ref.dtype)

    for q, (bi, g) in enumerate(probs):
        uv = jnp.concatenate([u[q], v_g[q]], axis=0).astype(BF16)
        bk = jnp.concatenate([b_s[q], k_s[q]], axis=0).astype(BF16)
        upd = lax.dot_general(uv, bk, (((0,), (0,)), ((), ())), preferred_element_type=F32)
        ht_ref[bi, g] = hts[q] * prep[bi]["g_tot"][:, cols[q]] + jnp.where(bd_mask, upd, 0.0)

    @pl.when(step == pl.num_programs(2) - 1)
    def _():
        sfin_ref[...] = ht_ref[...]


def _rwkv_scan(z, col0, lora_col0, w2, a2, w0, a0, kks, ka, rk, s0):
    nb, seq, _ = z.shape
    lora, ch = w2.shape[1:]
    c = CHUNK
    assert CHUNK == HEAD and ch % PACK_W == 0 and seq % c == 0 and 2 * lora == LANE
    assert col0 % ch == 0 and lora_col0 % LANE == 0
    nc = seq // c
    ng = ch // PACK_W
    bs = SCAN_BATCH if nb % SCAN_BATCH == 0 else 1
    chunk_of = lambda n, i: jnp.where(n == 0, i, nc - 1 - i)
    zcol = lambda k: pl.BlockSpec((bs, c, ch), lambda n, b, i: (b, chunk_of(n, i), col0 // ch + k))
    zlora = lambda k: pl.BlockSpec((bs, c, LANE), lambda n, b, i: (b, chunk_of(n, i), lora_col0 // LANE + k))
    per_dir_w = pl.BlockSpec((None, lora, ch), lambda n, b, i: (n, 0, 0))
    per_dir_v = pl.BlockSpec((None, 1, ch), lambda n, b, i: (n, 0, 0))
    shared_v = pl.BlockSpec((1, ch), lambda n, b, i: (0, 0))
    state = pl.BlockSpec((None, bs, ng, PACK_W, PACK_W), lambda n, b, i: (n, b, 0, 0, 0))
    out = pl.BlockSpec((None, bs, c, ch), lambda n, b, i: (n, b, chunk_of(n, i), 0))
    return pl.pallas_call(
        _scan_kernel,
        grid=(2, nb // bs, nc),
        in_specs=[zcol(0), zcol(1), zcol(2), zlora(0), zlora(1), per_dir_w, per_dir_w, per_dir_v, per_dir_v,
                  shared_v, shared_v, shared_v, state],
        out_specs=[out, out, state],
        out_shape=[jax.ShapeDtypeStruct((2, nb, seq, ch), BF16), jax.ShapeDtypeStruct((2, nb, seq, ch), BF16),
                   jax.ShapeDtypeStruct((2, nb, ng, PACK_W, PACK_W), F32)],
        scratch_shapes=[pltpu.VMEM((bs, ng, PACK_W, PACK_W), F32)],
        compiler_params=_cparams(("parallel", "parallel", "arbitrary")),
    )(z, z, z, z, z, w2, a2, w0.reshape(2, 1, ch), a0.reshape(2, 1, ch), kks, ka, rk, s0)


def _outproj_kernel(yh_ref, ys0_ref, ys1_ref, b0_ref, b1_ref, lg_ref, g2_ref, lng_ref, lnb_ref, ow_ref,
                    x_ref, gate_ref, n2g_ref, shift_ref, scale_ref, rwh_ref, rwl_ref, rb_ref,
                    x1_ref, h2_ref, lg_out_ref, mix_ref):
    hy = yh_ref.shape[1]
    tm = x_ref.shape[0]
    ones = _group_ones(PACK_W)
    rows = _tile(tm, SUB_ROWS)
    subs = [slice(r0, r0 + rows) for r0 in range(0, tm, rows)]

    for rs in subs:
        y = ys0_ref[rs].astype(F32) + ys1_ref[rs].astype(F32)
        mu = _group_sum(y, ones) * (1.0 / HEAD)
        yc = y - mu
        var = _group_sum(yc * yc, ones) * (1.0 / HEAD)
        y = yc * lax.rsqrt(var + LNX_EPS) * lng_ref[...] + lnb_ref[...]
        gate = _dot(jax.nn.sigmoid(lg_ref[rs].astype(F32)), g2_ref[...])
        y = (y + b0_ref[rs].astype(F32) + b1_ref[rs].astype(F32)) * gate
        mix_ref[rs, :hy] = yh_ref[rs]
        mix_ref[rs, hy:] = y.astype(BF16)
    mixes = [jnp.dot(mix_ref[rs], ow_ref[...], preferred_element_type=F32) for rs in subs]
    for rs, mix in zip(subs, mixes):
        x1 = x_ref[rs] + gate_ref[...] * mix
        x1_ref[rs] = x1
        h2 = x1 * lax.rsqrt(jnp.mean(x1 * x1, axis=-1, keepdims=True) + NORM_EPS) * n2g_ref[...]
        h2 = h2 * (1.0 + scale_ref[...]) + shift_ref[...]
        h2_ref[rs] = h2.astype(BF16)
        h_hi = h2.astype(BF16)
        h_lo = (h2 - h_hi.astype(F32)).astype(BF16)
        logits = (jnp.dot(h_hi, rwh_ref[...], preferred_element_type=F32)
                  + jnp.dot(h_hi, rwl_ref[...], preferred_element_type=F32)
                  + jnp.dot(h_lo, rwh_ref[...], preferred_element_type=F32))
        lg_out_ref[rs] = logits + rb_ref[...]


def _outproj(y_hy, ys, bonus, z, lora_g_col, g2, lnx_g, lnx_b, out_w, x, gate, n2g, shift, scale,
             rw_hi, rw_lo, rb, tm):
    nb, seq, d = x.shape
    m = nb * seq
    hy = y_hy.shape[2]
    rw = ys.shape[3]
    tm = _tile(seq, tm)
    lg = g2.shape[0]
    ne = rw_hi.shape[1]
    assert lora_g_col % lg == 0
    ys2 = ys.reshape(2, m, rw)
    bn2 = bonus.reshape(2, m, rw)
    nz = z.shape[2]
    full = lambda r, c: pl.BlockSpec((r, c), lambda i: (0, 0))
    dir_spec = lambda n: pl.BlockSpec((None, tm, rw), lambda i: (n, i, 0))
    mod_spec = pl.BlockSpec((None, 1, d), lambda i: ((i * tm) // seq, 0, 0))
    return pl.pallas_call(
        _outproj_kernel,
        grid=(m // tm,),
        in_specs=[pl.BlockSpec((tm, hy), lambda i: (i, 0)), dir_spec(0), dir_spec(1), dir_spec(0), dir_spec(1),
                  pl.BlockSpec((tm, lg), lambda i: (i, lora_g_col // lg)),
                  full(lg, rw), full(1, rw), full(1, rw), full(hy + rw, d),
                  pl.BlockSpec((tm, d), lambda i: (i, 0)), mod_spec, full(1, d), mod_spec, mod_spec,
                  full(d, ne), full(d, ne), full(1, ne)],
        out_specs=[pl.BlockSpec((tm, d), lambda i: (i, 0)), pl.BlockSpec((tm, d), lambda i: (i, 0)),
                   pl.BlockSpec((tm, ne), lambda i: (i, 0))],
        out_shape=[jax.ShapeDtypeStruct((m, d), F32), jax.ShapeDtypeStruct((m, d), BF16),
                   jax.ShapeDtypeStruct((m, ne), F32)],
        scratch_shapes=[pltpu.VMEM((tm, hy + rw), BF16)],
        compiler_params=_cparams(("parallel",)),
    )(y_hy.reshape(m, hy), ys2, ys2, bn2, bn2, z.reshape(m, nz), g2, lnx_g, lnx_b, out_w,
      x.reshape(m, d), gate, n2g, shift, scale, rw_hi, rw_lo, rb)


def _expert_changed(te_ref, i):
    prev = te_ref[jnp.maximum(i - 1, 0)]
    return jnp.logical_or(i == 0, te_ref[i] != prev)


def _moe_up_kernel(te_ref, nu_ref, x_ref, wg_ref, wu_ref, bg_ref, bu_ref, h_ref, wg_bf, wu_bf):
    i = pl.program_id(1)

    @pl.when(_expert_changed(te_ref, i))
    def _():
        wg_bf[...] = wg_ref[...].astype(BF16)
        wu_bf[...] = wu_ref[...].astype(BF16)

    @pl.when(i < nu_ref[0])
    def _():
        x = x_ref[...]
        g = jnp.dot(x, wg_bf[...], preferred_element_type=F32) + bg_ref[...]
        u = jnp.dot(x, wu_bf[...], preferred_element_type=F32) + bu_ref[...]
        g = jnp.minimum(g, SWIGLU_LIMIT)
        u = jnp.clip(u, -SWIGLU_LIMIT, SWIGLU_LIMIT)
        h_ref[...] = ((u + 1.0) * (g * jax.nn.sigmoid(SWIGLU_ALPHA * g))).astype(h_ref.dtype)

    @pl.when(i >= nu_ref[0])
    def _():
        h_ref[...] = jnp.zeros_like(h_ref)


def _moe_down_kernel(te_ref, nu_ref, h_ref, wd_ref, bd_ref, *rest):
    y_ref, wd_bf = rest[-2:]
    i = pl.program_id(1)

    @pl.when(_expert_changed(te_ref, i))
    def _():
        wd_bf[...] = wd_ref[...].astype(BF16)

    @pl.when(i < nu_ref[0])
    def _():
        y = jnp.dot(h_ref[...], wd_bf[...], preferred_element_type=F32) + bd_ref[...]
        y_ref[...] = y.astype(y_ref.dtype)

    @pl.when(i >= nu_ref[0])
    def _():
        y_ref[...] = jnp.zeros_like(y_ref)


def _moe_up(tile_expert, n_used, xs, wg, wu, bg, bu, tn):
    n_rows, d = xs.shape
    e, _, f = wg.shape
    tm = MOE_TILE
    tn = _tile(f, tn)
    w_spec = pl.BlockSpec((None, d, tn), lambda j, i, te, nu: (te[i], 0, j))
    b_spec = pl.BlockSpec((None, 1, tn), lambda j, i, te, nu: (te[i], 0, j))
    return pl.pallas_call(
        _moe_up_kernel,
        grid_spec=pltpu.PrefetchScalarGridSpec(
            num_scalar_prefetch=2,
            grid=(f // tn, n_rows // tm),
            in_specs=[pl.BlockSpec((tm, d), lambda j, i, te, nu: (i, 0)), w_spec, w_spec, b_spec, b_spec],
            out_specs=pl.BlockSpec((tm, tn), lambda j, i, te, nu: (i, j)),
            scratch_shapes=[pltpu.VMEM((d, tn), BF16), pltpu.VMEM((d, tn), BF16)]),
        out_shape=jax.ShapeDtypeStruct((n_rows, f), BF16),
        compiler_params=_cparams(("arbitrary", "arbitrary")),
    )(tile_expert, n_used, xs, wg, wu, bg.reshape(e, 1, f), bu.reshape(e, 1, f))


def _moe_down(tile_expert, n_used, hs, wd, bdn, tn, total_rows, tile0, ys_prev):
    n_rows, f = hs.shape
    e, _, d = wd.shape
    tm = MOE_TILE
    tn = _tile(d, tn)
    in_specs = [pl.BlockSpec((tm, f), lambda j, i, te, nu: (i, 0)),
                pl.BlockSpec((None, f, tn), lambda j, i, te, nu: (te[i], 0, j)),
                pl.BlockSpec((None, 1, tn), lambda j, i, te, nu: (te[i], 0, j))]
    args = [tile_expert, n_used, hs, wd, bdn.reshape(e, 1, d)]
    aliases = {}
    if ys_prev is not None:
        in_specs.append(pl.BlockSpec(memory_space=pl.ANY))
        args.append(ys_prev)
        aliases = {len(args) - 1: 0}
    return pl.pallas_call(
        _moe_down_kernel,
        grid_spec=pltpu.PrefetchScalarGridSpec(
            num_scalar_prefetch=2,
            grid=(d // tn, n_rows // tm),
            in_specs=in_specs,
            out_specs=pl.BlockSpec((tm, tn), lambda j, i, te, nu: (tile0 + i, j)),
            scratch_shapes=[pltpu.VMEM((f, tn), BF16)]),
        out_shape=jax.ShapeDtypeStruct((total_rows, d), BF16),
        input_output_aliases=aliases,
        compiler_params=_cparams(("arbitrary", "arbitrary")),
    )(*args)


def _route_kernel(n_exp, lg_ref, idx_ref, gate_ref, rank_ref, cnt_ref, carry_ref):
    tm, lanes = lg_ref.shape

    @pl.when(pl.program_id(0) == 0)
    def _():
        carry_ref[...] = jnp.zeros_like(carry_ref)

    lane = lax.broadcasted_iota(jnp.int32, (tm, lanes), 1).astype(F32)
    lg = jnp.where(lane < n_exp, lg_ref[...], -jnp.inf)
    tops, hots, idxs = [], [], []
    for _ in range(TOP_K):
        m = jnp.max(lg, axis=-1, keepdims=True)
        idx = jnp.min(jnp.where(lg == m, lane, float(lanes)), axis=-1, keepdims=True)
        hot = lane == idx
        lg = jnp.where(hot, -jnp.inf, lg)
        tops.append(m)
        hots.append(hot)
        idxs.append(idx)
    exps = [jnp.exp(t - tops[0]) for t in tops]
    inv = 1.0 / sum(exps)
    occ = sum(jnp.where(h, 1.0, 0.0) for h in hots)
    earlier = jnp.where(lax.broadcasted_iota(jnp.int32, (tm, tm), 0) > lax.broadcasted_iota(jnp.int32, (tm, tm), 1),
                        1.0, 0.0).astype(BF16)
    before = jnp.dot(earlier, occ.astype(BF16), preferred_element_type=F32) + carry_ref[...]
    ranks = [jnp.sum(jnp.where(h, before, 0.0), axis=-1, keepdims=True) for h in hots]
    carry_ref[...] += jnp.sum(occ, axis=0, keepdims=True)
    cnt_ref[...] = carry_ref[...]
    spread = lambda cols: sum(jnp.where(lane == float(k), c, 0.0) for k, c in enumerate(cols))
    idx_ref[...] = spread(idxs)
    gate_ref[...] = spread([e * inv for e in exps])
    rank_ref[...] = spread(ranks)


def _route(logits, n_exp, tm):
    n_tok, lanes = logits.shape
    tm = _tile(n_tok, tm)
    row = pl.BlockSpec((tm, lanes), lambda i: (i, 0))
    out = jax.ShapeDtypeStruct((n_tok, lanes), F32)
    idx, gates, rank, cnt = pl.pallas_call(
        functools.partial(_route_kernel, n_exp),
        grid=(n_tok // tm,),
        in_specs=[row],
        out_specs=[row, row, row, pl.BlockSpec((1, lanes), lambda i: (0, 0))],
        out_shape=[out, out, out, jax.ShapeDtypeStruct((1, lanes), F32)],
        scratch_shapes=[pltpu.VMEM((1, lanes), F32)],
        compiler_params=_cparams(("arbitrary",)),
    )(logits)
    return (idx[:, :TOP_K].astype(jnp.int32), gates[:, :TOP_K], rank[:, :TOP_K].astype(jnp.int32),
            cnt[0, :n_exp].astype(jnp.int32))


def _moe(h2, logits, e, wg, bg, wu, bu, wd, bdn):
    n_tok, d = h2.shape
    tm = MOE_TILE
    top_idx, gates, rank, counts = _route(logits, e, ROUTE_TILE)
    n_asg = n_tok * TOP_K
    raw_start = jnp.cumsum(counts) - counts
    padded = (counts + tm - 1) // tm * tm
    pad_end = jnp.cumsum(padded)
    pad_start = pad_end - padded
    dest = (pad_start[top_idx] + rank).reshape(n_asg)
    tok = jnp.arange(n_asg, dtype=jnp.int32) // TOP_K
    _, order_tok = lax.sort((dest, tok), num_keys=1)
    n_tiles = -(-(n_asg + e * (tm - 1)) // tm)
    n_rows = n_tiles * tm
    tile_start = jnp.arange(n_tiles, dtype=jnp.int32) * tm
    tile_expert = jnp.minimum(jnp.sum((pad_end[None, :] <= tile_start[:, None]).astype(jnp.int32), axis=1), e - 1)
    n_used = pad_end[-1] // tm
    row = jnp.arange(n_rows, dtype=jnp.int32)
    row_e = jnp.repeat(tile_expert, tm)
    row_rank = row - pad_start[row_e]
    src = jnp.clip(raw_start[row_e] + row_rank, 0, n_asg - 1)
    row_tok = jnp.where(row_rank < counts[row_e], order_tok[src], 0)

    n_parts = next(p for p in (MOE_PARTS, 2, 1) if n_tiles % p == 0)
    tp = n_tiles // n_parts
    ys = None
    for q in range(n_parts):
        te_q = tile_expert[q * tp:(q + 1) * tp]
        nu_q = jnp.clip(n_used - q * tp, 0, tp).astype(jnp.int32).reshape(1)
        xs = h2[row_tok[q * tp * tm:(q + 1) * tp * tm]]
        hs = _moe_up(te_q, nu_q, xs, wg, wu, bg, bu, 1024)
        ys = _moe_down(te_q, nu_q, hs, wd, bdn, 1024, n_rows, q * tp, ys)
    return ys, dest.reshape(n_tok, TOP_K).T, gates


def _combine_kernel(y_ref, g_ref, x1_ref, gate_ref, fg_ref, *rest):
    o_ref = rest[-1]
    g = g_ref[...]
    acc = y_ref[0].astype(F32) * g[:, 0:1]
    for k in range(1, y_ref.shape[0]):
        acc = acc + y_ref[k].astype(F32) * g[:, k:k + 1]
    xo = x1_ref[...] + gate_ref[...] * acc
    o_ref[...] = xo * lax.rsqrt(jnp.mean(xo * xo, axis=-1, keepdims=True) + NORM_EPS) * fg_ref[...]


def _combine(picked, gates, x1, gate, final_g, seq, tm, tile0, out_prev):
    nk, m, d = picked.shape
    total = x1.shape[0]
    in_specs = [pl.BlockSpec((nk, tm, d), lambda i: (0, i, 0)),
                pl.BlockSpec((tm, nk), lambda i: (tile0 + i, 0)),
                pl.BlockSpec((tm, d), lambda i: (tile0 + i, 0)),
                pl.BlockSpec((None, 1, d), lambda i: (((tile0 + i) * tm) // seq, 0, 0)),
                pl.BlockSpec((1, d), lambda i: (0, 0))]
    args = [picked, gates, x1, gate, final_g]
    aliases = {}
    if out_prev is not None:
        in_specs.append(pl.BlockSpec(memory_space=pl.ANY))
        args.append(out_prev)
        aliases = {len(args) - 1: 0}
    return pl.pallas_call(
        _combine_kernel,
        grid=(m // tm,),
        in_specs=in_specs,
        out_specs=pl.BlockSpec((tm, d), lambda i: (tile0 + i, 0)),
        out_shape=jax.ShapeDtypeStruct((total, d), F32),
        input_output_aliases=aliases,
        compiler_params=_cparams(("parallel",)),
    )(*args)


def _rmsnorm(x, g):
    return x * lax.rsqrt(jnp.mean(x * x, axis=-1, keepdims=True) + NORM_EPS) * g


def kernel(x, c, ctx, c_ctx, ada_w, ada_b, norm1_g, norm2_g, in_w, conv_w, conv_b, hy_w1, hy_b1, hy_w2, hy_b2, hy_w3, hy_b3, hy_freq, hy_w4, hy_bias, hy_norm_g, rw_w0, rw_w1, rw_w2, rw_a0, rw_a1, rw_a2, rw_kk, rw_ka, rw_rk, rw_g1, rw_g2, rw_lnx_g, rw_lnx_b, out_w, router_w, router_b, ex_w_gate, ex_b_gate, ex_w_up, ex_b_up, ex_w_down, ex_b_down, final_g):
    assert ada_w.shape[0] == 1, "single-layer block: context outputs never reach a latent token"
    nb, seq, d = x.shape
    hy = hy_bias.shape[1]
    rw = rw_kk.shape[1]
    n_hy = 3 * hy
    proj = in_w.shape[2]
    n_exp = router_w.shape[2]

    cond = jnp.concatenate([jax.nn.silu(c), jax.nn.silu(c_ctx)[None]], axis=0)
    mod = _matmul(cond, ada_w[0], F32, cond.shape[0], 1024) + ada_b[0]
    mod_x = [m[:, None, :] for m in jnp.split(mod[:nb], 6, axis=-1)]
    mod_c = [m[:, None, :] for m in jnp.split(mod[nb:], 6, axis=-1)]

    lora_w = jnp.concatenate([rw_w1[0, 0], rw_w1[0, 1], rw_a1[0, 0], rw_a1[0, 1], rw_g1[0]], axis=1)
    n_lora = lora_w.shape[1]
    nz = -(-(proj + n_lora) // 512) * 512
    pad = nz - proj - n_lora
    w_all = jnp.concatenate([in_w[0], lora_w, jnp.zeros((d, pad), F32)], axis=1).astype(BF16)
    pass_taps = jnp.concatenate([jnp.zeros((1, n_lora + pad), F32), jnp.ones((1, n_lora + pad), F32),
                                 jnp.zeros((1, n_lora + pad), F32)], axis=0)
    cw_all = jnp.concatenate([conv_w[0], pass_taps], axis=1)
    cb_all = jnp.concatenate([conv_b[0], jnp.zeros((n_lora + pad,), F32)])[None]
    g1 = norm1_g[0][None]
    zx = _inproj(x, mod_x[0], mod_x[1], g1, w_all, cw_all, cb_all, 1024, 512)
    zc = _inproj(ctx, mod_c[0], mod_c[1], g1, w_all[:, n_hy:], cw_all[:, n_hy:], cb_all[:, n_hy:], 256, 512)

    h_fwd, h_bwd = _hyena_filter(seq, hy_w1[0], hy_b1[0], hy_w2[0], hy_b2[0], hy_w3[0], hy_b3[0],
                                 hy_freq[0], hy_w4[0])
    fwd, inv = _dft_matrices(seq)
    taps = _taps_spectrum(fwd, h_fwd, h_bwd)
    spec = _dft_fwd(fwd, zx, hy, 1024, 512)
    y_hy = _dft_inv(inv, spec, taps, zx, hy_bias[0][None], hy_norm_g[0][None], 1024, 512)

    scan_args = (rw_w2[0].astype(BF16), rw_a2[0].astype(BF16), rw_w0[0], rw_a0[0],
                 rw_kk[0][None], rw_ka[0][None], rw_rk[0].reshape(1, rw))
    s0 = jnp.zeros((2, nb, rw // PACK_W, PACK_W, PACK_W), F32)
    _, _, s_ctx = _rwkv_scan(zc, 0, proj - n_hy, *scan_args, s0)
    ys, bonus, _ = _rwkv_scan(zx, n_hy, proj, *scan_args, s_ctx)

    ne = -(-n_exp // LANE) * LANE
    rw_pad = jnp.pad(router_w[0], ((0, 0), (0, ne - n_exp)))
    rw_hi = rw_pad.astype(BF16)
    rw_lo = (rw_pad - rw_hi.astype(F32)).astype(BF16)
    rb = jnp.pad(router_b[0], (0, ne - n_exp))[None]
    x1, h2, logits = _outproj(y_hy, ys, bonus, zx, proj + n_lora - rw_g1.shape[-1], rw_g2[0].astype(BF16),
                              rw_lnx_g[0][None], rw_lnx_b[0][None], out_w[0].astype(BF16), x,
                              mod_x[2], norm2_g[0][None], mod_x[3], mod_x[4], rw_hi, rw_lo, rb, 256)

    ys, dest_km, gates = _moe(h2, logits, n_exp, ex_w_gate[0], ex_b_gate[0], ex_w_up[0], ex_b_up[0],
                              ex_w_down[0], ex_b_down[0])
    n_tok = nb * seq
    tm = _tile(seq, 256)
    n_parts = next(p for p in (COMBINE_PARTS, 2, 1) if (n_tok // tm) % p == 0)
    tq = n_tok // n_parts
    out = None
    for q in range(n_parts):
        picked = ys[dest_km[:, q * tq:(q + 1) * tq]]
        out = _combine(picked, gates, x1, mod_x[5], final_g[None], seq, tm, q * (tq // tm), out)
    return out.reshape(nb, seq, d)
```

```python
import functools
import math

import jax
import jax.numpy as jnp
from jax import lax
from jax.experimental import pallas as pl
from jax.experimental.pallas import tpu as pltpu

F32 = jnp.float32
BF16 = jnp.bfloat16

HEAD = 64
CHUNK = 64
PACK = 4
PACK_W = PACK * HEAD
SCAN_BATCH = 4
HALO = 16
INPROJ_ROWS = 256
SUB_ROWS = 256
FILTER_BANDS = 16
FILTER_TARGET = 1e-2
FAST_DECAY_PCT = 0.3
SLOW_DECAY_PCT = 1.5
TOP_K = 4
SWIGLU_LIMIT = 7.0
SWIGLU_ALPHA = 1.702
NORM_EPS = 1e-6
LNX_EPS = 64e-5
MOE_TILE = 512
ROUTE_TILE = 512
MOE_PARTS = 4
COMBINE_PARTS = 2
LANE = 128
VMEM_LIMIT = 56 * 1024 * 1024


def _cparams(sem):
    return pltpu.CompilerParams(dimension_semantics=sem, vmem_limit_bytes=VMEM_LIMIT)


def _tile(n, want):
    if n <= want:
        return n
    t = want
    while n % t:
        t //= 2
    assert t >= 8, (n, want)
    return t


def _dot(a, b):
    return jnp.dot(a.astype(BF16), b.astype(BF16), preferred_element_type=F32)


def _group_ones(width):
    r = lax.broadcasted_iota(jnp.int32, (width, width), 0) // HEAD
    c = lax.broadcasted_iota(jnp.int32, (width, width), 1) // HEAD
    return jnp.where(r == c, 1.0, 0.0).astype(BF16)


def _group_sum(x, ones):
    width = ones.shape[0]
    parts = [_dot(x[:, o:o + width], ones) for o in range(0, x.shape[1], width)]
    return parts[0] if len(parts) == 1 else jnp.concatenate(parts, axis=1)


def _mm_kernel(a_ref, b_ref, o_ref):
    o_ref[...] = _dot(a_ref[...], b_ref[...]).astype(o_ref.dtype)


def _matmul(a, b, out_dtype, tm, tn):
    m, k = a.shape
    n = b.shape[1]
    tm, tn = _tile(m, tm), _tile(n, tn)
    return pl.pallas_call(
        _mm_kernel,
        grid=(n // tn, m // tm),
        in_specs=[pl.BlockSpec((tm, k), lambda j, i: (i, 0)),
                  pl.BlockSpec((k, tn), lambda j, i: (0, j))],
        out_specs=pl.BlockSpec((tm, tn), lambda j, i: (i, j)),
        out_shape=jax.ShapeDtypeStruct((m, n), out_dtype),
        compiler_params=_cparams(("parallel", "parallel")),
    )(a, b)


def _bmm_shared_lhs(a, b, out_dtype, tm, tn):
    m, k = a.shape
    nb, _, n = b.shape
    tm, tn = _tile(m, tm), _tile(n, tn)
    return pl.pallas_call(
        _mm_kernel,
        grid=(m // tm, nb, n // tn),
        in_specs=[pl.BlockSpec((tm, k), lambda i, bb, j: (i, 0)),
                  pl.BlockSpec((None, k, tn), lambda i, bb, j: (bb, 0, j))],
        out_specs=pl.BlockSpec((None, tm, tn), lambda i, bb, j: (bb, i, j)),
        out_shape=jax.ShapeDtypeStruct((nb, m, n), out_dtype),
        compiler_params=_cparams(("parallel", "parallel", "parallel")),
    )(a, b)


def _inproj_kernel(seq, xm_ref, xp_ref, xn_ref, shift_ref, scale_ref, g_ref, w_ref, cw_ref, cb_ref,
                   z_ref, h_ref, zz_ref):
    tm = xm_ref.shape[0]
    i = pl.program_id(0)

    @pl.when(pl.program_id(1) == 0)
    def _():
        def norm(xv):
            y = xv * lax.rsqrt(jnp.mean(xv * xv, axis=-1, keepdims=True) + NORM_EPS) * g_ref[...]
            return y * (1.0 + scale_ref[...]) + shift_ref[...]
        keep_prev = jnp.where((i * tm) % seq == 0, 0.0, 1.0)
        keep_next = jnp.where(((i + 1) * tm) % seq == 0, 0.0, 1.0)
        h_ref[0:HALO] = (norm(xp_ref[...]) * keep_prev).astype(BF16)
        h_ref[HALO:HALO + tm] = norm(xm_ref[...]).astype(BF16)
        h_ref[HALO + tm:] = (norm(xn_ref[...]) * keep_next).astype(BF16)

    cw = cw_ref[...]
    n_chunks = max(1, tm // INPROJ_ROWS)
    q = tm // n_chunks
    d0 = o0 = 0
    for ci in range(n_chunks):
        last = ci + 1 == n_chunks
        d1 = tm + 2 * HALO if last else (ci + 1) * q + HALO
        o1 = tm if last else d1 - 2 * HALO
        zz_ref[d0:d1] = jnp.dot(h_ref[d0:d1], w_ref[...], preferred_element_type=F32)
        z_ref[o0:o1] = (zz_ref[HALO - 1 + o0:HALO - 1 + o1] * cw[0:1] + zz_ref[HALO + o0:HALO + o1] * cw[1:2]
                        + zz_ref[HALO + 1 + o0:HALO + 1 + o1] * cw[2:3] + cb_ref[...]).astype(z_ref.dtype)
        d0, o0 = d1, o1


def _inproj(x, shift, scale, g, w, cw, cb, tm, tn):
    nb, seq, d = x.shape
    n = w.shape[1]
    tm, tn = _tile(seq, tm), _tile(n, tn)
    m = nb * seq
    n_halo = m // HALO
    per_batch = shift.shape[0] > 1
    mod_spec = pl.BlockSpec((None, 1, d), (lambda i, j: ((i * tm) // seq, 0, 0)) if per_batch
                            else (lambda i, j: (0, 0, 0)))
    x2 = x.reshape(m, d)
    out = pl.pallas_call(
        functools.partial(_inproj_kernel, seq),
        grid=(m // tm, n // tn),
        in_specs=[pl.BlockSpec((tm, d), lambda i, j: (i, 0)),
                  pl.BlockSpec((HALO, d), lambda i, j: (jnp.maximum(i * (tm // HALO) - 1, 0), 0)),
                  pl.BlockSpec((HALO, d), lambda i, j: (jnp.minimum((i + 1) * (tm // HALO), n_halo - 1), 0)),
                  mod_spec, mod_spec,
                  pl.BlockSpec((1, d), lambda i, j: (0, 0)),
                  pl.BlockSpec((d, tn), lambda i, j: (0, j)),
                  pl.BlockSpec((3, tn), lambda i, j: (0, j)),
                  pl.BlockSpec((1, tn), lambda i, j: (0, j))],
        out_specs=pl.BlockSpec((tm, tn), lambda i, j: (i, j)),
        out_shape=jax.ShapeDtypeStruct((m, n), BF16),
        scratch_shapes=[pltpu.VMEM((tm + 2 * HALO, d), BF16), pltpu.VMEM((tm + 2 * HALO, tn), F32)],
        compiler_params=_cparams(("parallel", "arbitrary")),
    )(x2, x2, x2, shift, scale, g, w, cw, cb)
    return out.reshape(nb, seq, n)


def _dft_matrices(length):
    n = 2 * length
    f = lax.broadcasted_iota(jnp.int32, (length, length), 0)
    t = lax.broadcasted_iota(jnp.int32, (length, length), 1)
    ang = ((f * t) % n).astype(F32) * (2.0 * math.pi / n)
    cos, sin = jnp.cos(ang), jnp.sin(ang)
    nyq = jnp.where(t % 2 == 0, 1.0, -1.0)
    fwd = jnp.concatenate([cos, jnp.where(f == 0, nyq, -sin)], axis=0)
    f_t = f.T
    inv = jnp.concatenate([jnp.where(f_t == 0, 1.0, 2.0 * cos.T),
                           jnp.where(f_t == 0, nyq.T, -2.0 * sin.T)], axis=1)
    return fwd.astype(BF16), inv.astype(BF16)


def _dft_fwd_kernel(f_ref, x1_ref, v_ref, o_ref, u_ref):
    @pl.when(pl.program_id(2) == 0)
    def _():
        u_ref[...] = (x1_ref[...].astype(F32) * v_ref[...].astype(F32)).astype(BF16)

    o_ref[...] = jnp.dot(f_ref[...], u_ref[...], preferred_element_type=F32).astype(o_ref.dtype)


def _dft_fwd(fwd, z, hy, tm, tn):
    nb, seq, _ = z.shape
    tm, tn = _tile(2 * seq, tm), _tile(hy, tn)
    nj = hy // tn
    return pl.pallas_call(
        _dft_fwd_kernel,
        grid=(nb, nj, 2 * seq // tm),
        in_specs=[pl.BlockSpec((tm, seq), lambda b, j, i: (i, 0)),
                  pl.BlockSpec((None, seq, tn), lambda b, j, i: (b, 0, nj + j)),
                  pl.BlockSpec((None, seq, tn), lambda b, j, i: (b, 0, 2 * nj + j))],
        out_specs=pl.BlockSpec((None, tm, tn), lambda b, j, i: (b, i, j)),
        out_shape=jax.ShapeDtypeStruct((nb, 2 * seq, hy), BF16),
        scratch_shapes=[pltpu.VMEM((seq, tn), BF16)],
        compiler_params=_cparams(("parallel", "parallel", "arbitrary")),
    )(fwd, z, z)


def _dft_inv_kernel(g_ref, s_ref, t_ref, x0_ref, x1_ref, v_ref, bias_ref, ng_ref, o_ref, y_ref):
    half = s_ref.shape[0] // 2

    @pl.when(pl.program_id(2) == 0)
    def _():
        re = s_ref[:half].astype(F32)
        im = s_ref[half:].astype(F32)
        t_re = t_ref[:half].astype(F32)
        t_im = t_ref[half:].astype(F32)
        row0 = lax.broadcasted_iota(jnp.int32, (half, 1), 0) == 0
        y_ref[:half] = (re * t_re - jnp.where(row0, 0.0, im * t_im)).astype(BF16)
        y_ref[half:] = (im * jnp.where(row0, t_im, t_re) + jnp.where(row0, 0.0, re * t_im)).astype(BF16)

    conv = jnp.dot(g_ref[...], y_ref[...], preferred_element_type=F32)
    u = x1_ref[...].astype(F32) * v_ref[...].astype(F32)
    y = x0_ref[...].astype(F32) * (conv + bias_ref[...] * u)
    ms = _group_sum(y * y, _group_ones(min(y.shape[1], PACK_W))) * (1.0 / HEAD)
    o_ref[...] = (y * lax.rsqrt(ms + NORM_EPS) * ng_ref[...]).astype(o_ref.dtype)


def _dft_inv(inv, spec, taps, z, bias, norm_g, tm, tn):
    nb, seq, _ = z.shape
    hy = spec.shape[2]
    tm, tn = _tile(seq, tm), _tile(hy, tn)
    nj = hy // tn
    row = lambda k: pl.BlockSpec((None, tm, tn), lambda b, j, i: (b, i, k * nj + j))
    vec = pl.BlockSpec((1, tn), lambda b, j, i: (0, j))
    return pl.pallas_call(
        _dft_inv_kernel,
        grid=(nb, nj, seq // tm),
        in_specs=[pl.BlockSpec((tm, 2 * seq), lambda b, j, i: (i, 0)),
                  pl.BlockSpec((None, 2 * seq, tn), lambda b, j, i: (b, 0, j)),
                  pl.BlockSpec((2 * seq, tn), lambda b, j, i: (0, j)),
                  row(0), row(1), row(2), vec, vec],
        out_specs=pl.BlockSpec((None, tm, tn), lambda b, j, i: (b, i, j)),
        out_shape=jax.ShapeDtypeStruct((nb, seq, hy), BF16),
        scratch_shapes=[pltpu.VMEM((2 * seq, tn), BF16)],
        compiler_params=_cparams(("parallel", "parallel", "arbitrary")),
    )(inv, spec, taps, z, z, z, bias, norm_g)


def _dot3(a, b):
    a_hi = a.astype(BF16)
    b_hi = b.astype(BF16)
    a_lo = (a - a_hi.astype(F32)).astype(BF16)
    b_lo = (b - b_hi.astype(F32)).astype(BF16)
    dot = lambda p, q: jnp.dot(p, q, preferred_element_type=F32)
    return dot(a_hi, b_hi) + dot(a_hi, b_lo) + dot(a_lo, b_hi)


def _filter_kernel(f_ref, t_ref, w1_ref, b1_ref, w2_ref, b2_ref, w3_ref, b3_ref, fr_ref, w4_ref, dl_ref, o_ref):
    freq = fr_ref[...]
    h = jnp.sin(freq * (_dot3(f_ref[...], w1_ref[...]) + b1_ref[...]))
    h = jnp.sin(freq * (_dot3(h, w2_ref[...]) + b2_ref[...]))
    h = jnp.sin(freq * (_dot3(h, w3_ref[...]) + b3_ref[...]))
    o_ref[...] = _dot3(h, w4_ref[...]) * jnp.exp(-t_ref[...] * dl_ref[...])


def _hyena_filter(length, w1, b1, w2, b2, w3, b3, freq, w4):
    hy = w4.shape[1] // 2
    width = w2.shape[0]
    t = jnp.linspace(0.0, 1.0, length, dtype=F32)[:, None]
    ang = (2.0 * math.pi / length) * jnp.arange(length, dtype=F32)[:, None]
    bands = jnp.linspace(1e-4, FILTER_BANDS - 1, FILTER_BANDS, dtype=F32)[None, :]
    feats = jnp.concatenate([t, jnp.cos(bands * ang), -jnp.sin(bands * ang)], axis=-1)
    emb = feats.shape[1]
    emb_pad = -(-emb // LANE) * LANE
    feats = jnp.pad(feats, ((0, 0), (0, emb_pad - emb)))
    w1p = jnp.pad(w1, ((0, emb_pad - emb), (0, 0)))
    deltas = jnp.abs(jnp.linspace(math.log(FILTER_TARGET) / SLOW_DECAY_PCT,
                                  math.log(FILTER_TARGET) / FAST_DECAY_PCT, hy, dtype=F32))
    tm = _tile(length, 512)
    full = lambda r, c: pl.BlockSpec((r, c), lambda i: (0, 0))
    h = pl.pallas_call(
        _filter_kernel,
        grid=(length // tm,),
        in_specs=[pl.BlockSpec((tm, emb_pad), lambda i: (i, 0)), pl.BlockSpec((tm, 1), lambda i: (i, 0)),
                  full(emb_pad, width), full(1, width), full(width, width), full(1, width),
                  full(width, width), full(1, width), full(1, width), full(width, 2 * hy), full(1, 2 * hy)],
        out_specs=pl.BlockSpec((tm, 2 * hy), lambda i: (i, 0)),
        out_shape=jax.ShapeDtypeStruct((length, 2 * hy), F32),
        compiler_params=_cparams(("parallel",)),
    )(feats, t, w1p, b1[None], w2, b2[None], w3, b3[None], freq[None], w4, jnp.tile(deltas, 2)[None])
    return h[:, :hy], h[:, hy:]


def _taps_spectrum(fwd, h_fwd, h_bwd):
    length = h_fwd.shape[0]
    sig = jnp.stack([h_fwd, h_bwd.at[0].set(0.0)]).astype(BF16)
    spec = _bmm_shared_lhs(fwd, sig, F32, 1024, 1024)
    re, im = spec[:, :length], spec[:, length:]
    scale = 1.0 / (2 * length)
    row0 = lax.broadcasted_iota(jnp.int32, (length, 1), 0) == 0
    t_re = (re[0] + re[1]) * scale
    t_im = jnp.where(row0, im[0] + im[1], im[0] - im[1]) * scale
    return jnp.concatenate([t_re, t_im], axis=0).astype(BF16)


def _scan_kernel(r_ref, k_ref, v_ref, lw_ref, la_ref, w2_ref, a2_ref, w0_ref, a0_ref, kks_ref, ka_ref, rk_ref,
                 s0_ref, y_ref, bonus_ref, sfin_ref, ht_ref):
    c = CHUNK
    w = PACK_W
    direction = pl.program_id(0)
    step = pl.program_id(2)
    n_batch = r_ref.shape[0]
    n_groups = r_ref.shape[2] // w
    lora = w2_ref.shape[0]

    @pl.when(step == 0)
    def _():
        ht_ref[...] = s0_ref[...]

    sign = 1 - 2 * direction
    d_sq = (lax.broadcasted_iota(jnp.int32, (c, c), 1) - lax.broadcasted_iota(jnp.int32, (c, c), 0)) * sign
    incl_sq = jnp.where(d_sq <= 0, 1.0, 0.0).astype(BF16)
    d_c = (lax.broadcasted_iota(jnp.int32, (c, w), 1) % c - lax.broadcasted_iota(jnp.int32, (c, w), 0)) * sign
    strict_c = d_c < 0
    incl_c = d_c <= 0
    eye_c = jnp.where(d_c == 0, 1.0, 0.0)
    bd_mask = (lax.broadcasted_iota(jnp.int32, (w, w), 0) // c) == (lax.broadcasted_iota(jnp.int32, (w, w), 1) // HEAD)
    ones_bd = jnp.where(bd_mask, 1.0, 0.0).astype(BF16)

    def bd(x):
        xb = x.astype(BF16)
        return jnp.where(bd_mask, jnp.concatenate([xb] * PACK, axis=0), jnp.zeros((), BF16))

    fwd = direction == 0
    incl_2 = jnp.concatenate([incl_sq, incl_sq], axis=1)
    prep = []
    for bi in range(n_batch):
        r_all = r_ref[bi].astype(F32)
        k_all = k_ref[bi].astype(F32)
        v_all = v_ref[bi].astype(F32)
        lw_pre = lw_ref[bi].astype(F32)
        la_pre = la_ref[bi].astype(F32)
        lw_pre = jnp.where(fwd, lw_pre[:, :lora], lw_pre[:, lora:])
        la_pre = jnp.where(fwd, la_pre[:, :lora], la_pre[:, lora:])
        x_w = w0_ref[...] + _dot(jnp.tanh(lw_pre), w2_ref[...])
        lw_all = -math.exp(-0.5) * jax.nn.sigmoid(x_w)
        a_sig = jax.nn.sigmoid(a0_ref[...] + _dot(la_pre, a2_ref[...]))
        kk = k_all * kks_ref[...]
        kk = kk * lax.rsqrt(jnp.maximum(_group_sum(kk * kk, ones_bd), 1e-24))
        kd_all = k_all * (1.0 + (a_sig - 1.0) * ka_ref[...])
        bb_all = kk * a_sig
        bonus_ref[bi] = (_group_sum(r_all * kd_all * rk_ref[...], ones_bd) * v_all).astype(bonus_ref.dtype)

        lw_hi = lw_all.astype(BF16)
        lw_lo = (lw_all - lw_hi.astype(F32)).astype(BF16)
        cum = jnp.dot(incl_2, jnp.concatenate([lw_hi, lw_lo], axis=0), preferred_element_type=F32)
        tot = jnp.sum(lw_all, axis=0, keepdims=True)
        g_inv = jnp.exp(-cum)
        g_rem = jnp.exp(tot - cum)
        prep.append(dict(v=v_all, g_tot=jnp.exp(tot), a_t=-kk * jnp.exp(cum - lw_all), r_t=r_all * jnp.exp(cum),
                         b_t=bb_all * g_inv, k_t=kd_all * g_inv, b_s=bb_all * g_rem, k_s=kd_all * g_rem))

    probs = [(bi, g) for bi in range(n_batch) for g in range(n_groups)]
    idx = range(len(probs))
    cols = [slice(g * w, (g + 1) * w) for _, g in probs]
    take = lambda name: [prep[bi][name][:, cols[q]] for q, (bi, _) in enumerate(probs)]
    v_g, a_t, r_t, b_t, k_t, b_s, k_s = (take(nm) for nm in ("v", "a_t", "r_t", "b_t", "k_t", "b_s", "k_s"))
    hts = [ht_ref[bi, g] for bi, g in probs]
    lhs = [jnp.concatenate([a_t[q], r_t[q]], axis=0) for q in idx]
    scores = [_dot(lhs[q], jnp.concatenate([bd(b_t[q]).T, bd(k_t[q]).T], axis=1)) for q in idx]
    h0 = [_dot(lhs[q], hts[q].T) for q in idx]
    a_ab = [jnp.where(strict_c, s[:c, :w], 0.0) for s in scores]
    a_ak = [jnp.where(strict_c, s[:c, w:], 0.0) for s in scores]
    a_rb = [jnp.where(incl_c, s[c:, :w], 0.0) for s in scores]
    a_rk = [jnp.where(incl_c, s[c:, w:], 0.0) for s in scores]
    bd_v = [bd(vq) for vq in v_g]
    av = [_dot(jnp.concatenate([a_ak[q], a_rk[q]], axis=0), bd_v[q]) for q in idx]
    x = [h0[q][:c] + av[q][:c] for q in idx]

    t_inv = [eye_c + a for a in a_ab]
    p = [_dot(a, bd(a)) for a in a_ab]
    n_sq = int(math.log2(c)) - 1
    for it in range(n_sq):
        bd_p = [bd(pq) for pq in p]
        if it + 1 < n_sq:
            pt = [_dot(jnp.concatenate([p[q], t_inv[q]], axis=0), bd_p[q]) for q in idx]
            p = [m[:c] for m in pt]
            t_inv = [t_inv[q] + pt[q][c:] for q in idx]
        else:
            t_inv = [t_inv[q] + _dot(t_inv[q], bd_p[q]) for q in idx]
    u = [_dot(t_inv[q], bd(x[q])) for q in idx]

    y = [h0[q][c:] + av[q][c:] + _dot(a_rb[q], bd(u[q])) for q in idx]
    for q, (bi, _) in enumerate(probs):
        y_ref[bi, :, cols[q]] = y[q].astype(y_ref.dtype)

    for q, (bi, g) in enumerate(probs):
        uv = jnp.concatenate([u[q], v_g[q]], axis=0).astype(BF16)
        bk = jnp.concatenate([b_s[q], k_s[q]], axis=0).astype(BF16)
        upd = lax.dot_general(uv, bk, (((0,), (0,)), ((), ())), preferred_element_type=F32)
        ht_ref[bi, g] = hts[q] * prep[bi]["g_tot"][:, cols[q]] + jnp.where(bd_mask, upd, 0.0)

    @pl.when(step == pl.num_programs(2) - 1)
    def _():
        sfin_ref[...] = ht_ref[...]


def _rwkv_scan(z, col0, lora_col0, w2, a2, w0, a0, kks, ka, rk, s0):
    nb, seq, _ = z.shape
    lora, ch = w2.shape[1:]
    c = CHUNK
    assert CHUNK == HEAD and ch % PACK_W == 0 and seq % c == 0 and 2 * lora == LANE
    assert col0 % ch == 0 and lora_col0 % LANE == 0
    nc = seq // c
    ng = ch // PACK_W
    bs = SCAN_BATCH if nb % SCAN_BATCH == 0 else 1
    chunk_of = lambda n, i: jnp.where(n == 0, i, nc - 1 - i)
    zcol = lambda k: pl.BlockSpec((bs, c, ch), lambda n, b, i: (b, chunk_of(n, i), col0 // ch + k))
    zlora = lambda k: pl.BlockSpec((bs, c, LANE), lambda n, b, i: (b, chunk_of(n, i), lora_col0 // LANE + k))
    per_dir_w = pl.BlockSpec((None, lora, ch), lambda n, b, i: (n, 0, 0))
    per_dir_v = pl.BlockSpec((None, 1, ch), lambda n, b, i: (n, 0, 0))
    shared_v = pl.BlockSpec((1, ch), lambda n, b, i: (0, 0))
    state = pl.BlockSpec((None, bs, ng, PACK_W, PACK_W), lambda n, b, i: (n, b, 0, 0, 0))
    out = pl.BlockSpec((None, bs, c, ch), lambda n, b, i: (n, b, chunk_of(n, i), 0))
    return pl.pallas_call(
        _scan_kernel,
        grid=(2, nb // bs, nc),
        in_specs=[zcol(0), zcol(1), zcol(2), zlora(0), zlora(1), per_dir_w, per_dir_w, per_dir_v, per_dir_v,
                  shared_v, shared_v, shared_v, state],
        out_specs=[out, out, state],
        out_shape=[jax.ShapeDtypeStruct((2, nb, seq, ch), BF16), jax.ShapeDtypeStruct((2, nb, seq, ch), BF16),
                   jax.ShapeDtypeStruct((2, nb, ng, PACK_W, PACK_W), F32)],
        scratch_shapes=[pltpu.VMEM((bs, ng, PACK_W, PACK_W), F32)],
        compiler_params=_cparams(("parallel", "parallel", "arbitrary")),
    )(z, z, z, z, z, w2, a2, w0.reshape(2, 1, ch), a0.reshape(2, 1, ch), kks, ka, rk, s0)


def _outproj_kernel(yh_ref, ys0_ref, ys1_ref, b0_ref, b1_ref, lg_ref, g2_ref, lng_ref, lnb_ref, ow_ref,
                    x_ref, gate_ref, n2g_ref, shift_ref, scale_ref, rwh_ref, rwl_ref, rb_ref,
                    x1_ref, h2_ref, lg_out_ref, mix_ref):
    hy = yh_ref.shape[1]
    tm = x_ref.shape[0]
    ones = _group_ones(PACK_W)
    rows = _tile(tm, SUB_ROWS)
    subs = [slice(r0, r0 + rows) for r0 in range(0, tm, rows)]

    for rs in subs:
        y = ys0_ref[rs].astype(F32) + ys1_ref[rs].astype(F32)
        mu = _group_sum(y, ones) * (1.0 / HEAD)
        yc = y - mu
        var = _group_sum(yc * yc, ones) * (1.0 / HEAD)
        y = yc * lax.rsqrt(var + LNX_EPS) * lng_ref[...] + lnb_ref[...]
        gate = _dot(jax.nn.sigmoid(lg_ref[rs].astype(F32)), g2_ref[...])
        y = (y + b0_ref[rs].astype(F32) + b1_ref[rs].astype(F32)) * gate
        mix_ref[rs, :hy] = yh_ref[rs]
        mix_ref[rs, hy:] = y.astype(BF16)
    mixes = [jnp.dot(mix_ref[rs], ow_ref[...], preferred_element_type=F32) for rs in subs]
    for rs, mix in zip(subs, mixes):
        x1 = x_ref[rs] + gate_ref[...] * mix
        x1_ref[rs] = x1
        h2 = x1 * lax.rsqrt(jnp.mean(x1 * x1, axis=-1, keepdims=True) + NORM_EPS) * n2g_ref[...]
        h2 = h2 * (1.0 + scale_ref[...]) + shift_ref[...]
        h2_ref[rs] = h2.astype(BF16)
        h_hi = h2.astype(BF16)
        h_lo = (h2 - h_hi.astype(F32)).astype(BF16)
        logits = (jnp.dot(h_hi, rwh_ref[...], preferred_element_type=F32)
                  + jnp.dot(h_hi, rwl_ref[...], preferred_element_type=F32)
                  + jnp.dot(h_lo, rwh_ref[...], preferred_element_type=F32))
        lg_out_ref[rs] = logits + rb_ref[...]


def _outproj(y_hy, ys, bonus, z, lora_g_col, g2, lnx_g, lnx_b, out_w, x, gate, n2g, shift, scale,
             rw_hi, rw_lo, rb, tm):
    nb, seq, d = x.shape
    m = nb * seq
    hy = y_hy.shape[2]
    rw = ys.shape[3]
    tm = _tile(seq, tm)
    lg = g2.shape[0]
    ne = rw_hi.shape[1]
    assert lora_g_col % lg == 0
    ys2 = ys.reshape(2, m, rw)
    bn2 = bonus.reshape(2, m, rw)
    nz = z.shape[2]
    full = lambda r, c: pl.BlockSpec((r, c), lambda i: (0, 0))
    dir_spec = lambda n: pl.BlockSpec((None, tm, rw), lambda i: (n, i, 0))
    mod_spec = pl.BlockSpec((None, 1, d), lambda i: ((i * tm) // seq, 0, 0))
    return pl.pallas_call(
        _outproj_kernel,
        grid=(m // tm,),
        in_specs=[pl.BlockSpec((tm, hy), lambda i: (i, 0)), dir_spec(0), dir_spec(1), dir_spec(0), dir_spec(1),
                  pl.BlockSpec((tm, lg), lambda i: (i, lora_g_col // lg)),
                  full(lg, rw), full(1, rw), full(1, rw), full(hy + rw, d),
                  pl.BlockSpec((tm, d), lambda i: (i, 0)), mod_spec, full(1, d), mod_spec, mod_spec,
                  full(d, ne), full(d, ne), full(1, ne)],
        out_specs=[pl.BlockSpec((tm, d), lambda i: (i, 0)), pl.BlockSpec((tm, d), lambda i: (i, 0)),
                   pl.BlockSpec((tm, ne), lambda i: (i, 0))],
        out_shape=[jax.ShapeDtypeStruct((m, d), F32), jax.ShapeDtypeStruct((m, d), BF16),
                   jax.ShapeDtypeStruct((m, ne), F32)],
        scratch_shapes=[pltpu.VMEM((tm, hy + rw), BF16)],
        compiler_params=_cparams(("parallel",)),
    )(y_hy.reshape(m, hy), ys2, ys2, bn2, bn2, z.reshape(m, nz), g2, lnx_g, lnx_b, out_w,
      x.reshape(m, d), gate, n2g, shift, scale, rw_hi, rw_lo, rb)


def _expert_changed(te_ref, i):
    prev = te_ref[jnp.maximum(i - 1, 0)]
    return jnp.logical_or(i == 0, te_ref[i] != prev)


def _moe_up_kernel(te_ref, nu_ref, x_ref, wg_ref, wu_ref, bg_ref, bu_ref, h_ref, wg_bf, wu_bf):
    i = pl.program_id(1)

    @pl.when(_expert_changed(te_ref, i))
    def _():
        wg_bf[...] = wg_ref[...].astype(BF16)
        wu_bf[...] = wu_ref[...].astype(BF16)

    @pl.when(i < nu_ref[0])
    def _():
        x = x_ref[...]
        g = jnp.dot(x, wg_bf[...], preferred_element_type=F32) + bg_ref[...]
        u = jnp.dot(x, wu_bf[...], preferred_element_type=F32) + bu_ref[...]
        g = jnp.minimum(g, SWIGLU_LIMIT)
        u = jnp.clip(u, -SWIGLU_LIMIT, SWIGLU_LIMIT)
        h_ref[...] = ((u + 1.0) * (g * jax.nn.sigmoid(SWIGLU_ALPHA * g))).astype(h_ref.dtype)

    @pl.when(i >= nu_ref[0])
    def _():
        h_ref[...] = jnp.zeros_like(h_ref)


def _moe_down_kernel(te_ref, nu_ref, h_ref, wd_ref, bd_ref, *rest):
    y_ref, wd_bf = rest[-2:]
    i = pl.program_id(1)

    @pl.when(_expert_changed(te_ref, i))
    def _():
        wd_bf[...] = wd_ref[...].astype(BF16)

    @pl.when(i < nu_ref[0])
    def _():
        y = jnp.dot(h_ref[...], wd_bf[...], preferred_element_type=F32) + bd_ref[...]
        y_ref[...] = y.astype(y_ref.dtype)

    @pl.when(i >= nu_ref[0])
    def _():
        y_ref[...] = jnp.zeros_like(y_ref)


def _moe_up(tile_expert, n_used, xs, wg, wu, bg, bu, tn):
    n_rows, d = xs.shape
    e, _, f = wg.shape
    tm = MOE_TILE
    tn = _tile(f, tn)
    w_spec = pl.BlockSpec((None, d, tn), lambda j, i, te, nu: (te[i], 0, j))
    b_spec = pl.BlockSpec((None, 1, tn), lambda j, i, te, nu: (te[i], 0, j))
    return pl.pallas_call(
        _moe_up_kernel,
        grid_spec=pltpu.PrefetchScalarGridSpec(
            num_scalar_prefetch=2,
            grid=(f // tn, n_rows // tm),
            in_specs=[pl.BlockSpec((tm, d), lambda j, i, te, nu: (i, 0)), w_spec, w_spec, b_spec, b_spec],
            out_specs=pl.BlockSpec((tm, tn), lambda j, i, te, nu: (i, j)),
            scratch_shapes=[pltpu.VMEM((d, tn), BF16), pltpu.VMEM((d, tn), BF16)]),
        out_shape=jax.ShapeDtypeStruct((n_rows, f), BF16),
        compiler_params=_cparams(("arbitrary", "arbitrary")),
    )(tile_expert, n_used, xs, wg, wu, bg.reshape(e, 1, f), bu.reshape(e, 1, f))


def _moe_down(tile_expert, n_used, hs, wd, bdn, tn, total_rows, tile0, ys_prev):
    n_rows, f = hs.shape
    e, _, d = wd.shape
    tm = MOE_TILE
    tn = _tile(d, tn)
    in_specs = [pl.BlockSpec((tm, f), lambda j, i, te, nu: (i, 0)),
                pl.BlockSpec((None, f, tn), lambda j, i, te, nu: (te[i], 0, j)),
                pl.BlockSpec((None, 1, tn), lambda j, i, te, nu: (te[i], 0, j))]
    args = [tile_expert, n_used, hs, wd, bdn.reshape(e, 1, d)]
    aliases = {}
    if ys_prev is not None:
        in_specs.append(pl.BlockSpec(memory_space=pl.ANY))
        args.append(ys_prev)
        aliases = {len(args) - 1: 0}
    return pl.pallas_call(
        _moe_down_kernel,
        grid_spec=pltpu.PrefetchScalarGridSpec(
            num_scalar_prefetch=2,
            grid=(d // tn, n_rows // tm),
            in_specs=in_specs,
            out_specs=pl.BlockSpec((tm, tn), lambda j, i, te, nu: (tile0 + i, j)),
            scratch_shapes=[pltpu.VMEM((f, tn), BF16)]),
        out_shape=jax.ShapeDtypeStruct((total_rows, d), BF16),
        input_output_aliases=aliases,
        compiler_params=_cparams(("arbitrary", "arbitrary")),
    )(*args)


def _route_kernel(n_exp, lg_ref, idx_ref, gate_ref, rank_ref, cnt_ref, carry_ref):
    tm, lanes = lg_ref.shape

    @pl.when(pl.program_id(0) == 0)
    def _():
        carry_ref[...] = jnp.zeros_like(carry_ref)

    lane = lax.broadcasted_iota(jnp.int32, (tm, lanes), 1).astype(F32)
    lg = jnp.where(lane < n_exp, lg_ref[...], -jnp.inf)
    tops, hots, idxs = [], [], []
    for _ in range(TOP_K):
        m = jnp.max(lg, axis=-1, keepdims=True)
        idx = jnp.min(jnp.where(lg == m, lane, float(lanes)), axis=-1, keepdims=True)
        hot = lane == idx
        lg = jnp.where(hot, -jnp.inf, lg)
        tops.append(m)
        hots.append(hot)
        idxs.append(idx)
    exps = [jnp.exp(t - tops[0]) for t in tops]
    inv = 1.0 / sum(exps)
    occ = sum(jnp.where(h, 1.0, 0.0) for h in hots)
    earlier = jnp.where(lax.broadcasted_iota(jnp.int32, (tm, tm), 0) > lax.broadcasted_iota(jnp.int32, (tm, tm), 1),
                        1.0, 0.0).astype(BF16)
    before = jnp.dot(earlier, occ.astype(BF16), preferred_element_type=F32) + carry_ref[...]
    ranks = [jnp.sum(jnp.where(h, before, 0.0), axis=-1, keepdims=True) for h in hots]
    carry_ref[...] += jnp.sum(occ, axis=0, keepdims=True)
    cnt_ref[...] = carry_ref[...]
    spread = lambda cols: sum(jnp.where(lane == float(k), c, 0.0) for k, c in enumerate(cols))
    idx_ref[...] = spread(idxs)
    gate_ref[...] = spread([e * inv for e in exps])
    rank_ref[...] = spread(ranks)


def _route(logits, n_exp, tm):
    n_tok, lanes = logits.shape
    tm = _tile(n_tok, tm)
    row = pl.BlockSpec((tm, lanes), lambda i: (i, 0))
    out = jax.ShapeDtypeStruct((n_tok, lanes), F32)
    idx, gates, rank, cnt = pl.pallas_call(
        functools.partial(_route_kernel, n_exp),
        grid=(n_tok // tm,),
        in_specs=[row],
        out_specs=[row, row, row, pl.BlockSpec((1, lanes), lambda i: (0, 0))],
        out_shape=[out, out, out, jax.ShapeDtypeStruct((1, lanes), F32)],
        scratch_shapes=[pltpu.VMEM((1, lanes), F32)],
        compiler_params=_cparams(("arbitrary",)),
    )(logits)
    return (idx[:, :TOP_K].astype(jnp.int32), gates[:, :TOP_K], rank[:, :TOP_K].astype(jnp.int32),
            cnt[0, :n_exp].astype(jnp.int32))


def _moe(h2, logits, e, wg, bg, wu, bu, wd, bdn):
    n_tok, d = h2.shape
    tm = MOE_TILE
    top_idx, gates, rank, counts = _route(logits, e, ROUTE_TILE)
    n_asg = n_tok * TOP_K
    raw_start = jnp.cumsum(counts) - counts
    padded = (counts + tm - 1) // tm * tm
    pad_end = jnp.cumsum(padded)
    pad_start = pad_end - padded
    dest = (pad_start[top_idx] + rank).reshape(n_asg)
    tok = jnp.arange(n_asg, dtype=jnp.int32) // TOP_K
    _, order_tok = lax.sort((dest, tok), num_keys=1)
    n_tiles = -(-(n_asg + e * (tm - 1)) // tm)
    n_rows = n_tiles * tm
    tile_start = jnp.arange(n_tiles, dtype=jnp.int32) * tm
    tile_expert = jnp.minimum(jnp.sum((pad_end[None, :] <= tile_start[:, None]).astype(jnp.int32), axis=1), e - 1)
    n_used = pad_end[-1] // tm
    row = jnp.arange(n_rows, dtype=jnp.int32)
    row_e = jnp.repeat(tile_expert, tm)
    row_rank = row - pad_start[row_e]
    src = jnp.clip(raw_start[row_e] + row_rank, 0, n_asg - 1)
    row_tok = jnp.where(row_rank < counts[row_e], order_tok[src], 0)

    n_parts = next(p for p in (MOE_PARTS, 2, 1) if n_tiles % p == 0)
    tp = n_tiles // n_parts
    ys = None
    for q in range(n_parts):
        te_q = tile_expert[q * tp:(q + 1) * tp]
        nu_q = jnp.clip(n_used - q * tp, 0, tp).astype(jnp.int32).reshape(1)
        xs = h2[row_tok[q * tp * tm:(q + 1) * tp * tm]]
        hs = _moe_up(te_q, nu_q, xs, wg, wu, bg, bu, 1024)
        ys = _moe_down(te_q, nu_q, hs, wd, bdn, 1024, n_rows, q * tp, ys)
    return ys, dest.reshape(n_tok, TOP_K).T, gates


def _combine_kernel(y_ref, g_ref, x1_ref, gate_ref, fg_ref, *rest):
    o_ref = rest[-1]
    g = g_ref[...]
    acc = y_ref[0].astype(F32) * g[:, 0:1]
    for k in range(1, y_ref.shape[0]):
        acc = acc + y_ref[k].astype(F32) * g[:, k:k + 1]
    xo = x1_ref[...] + gate_ref[...] * acc
    o_ref[...] = xo * lax.rsqrt(jnp.mean(xo * xo, axis=-1, keepdims=True) + NORM_EPS) * fg_ref[...]


def _combine(picked, gates, x1, gate, final_g, seq, tm, tile0, out_prev):
    nk, m, d = picked.shape
    total = x1.shape[0]
    in_specs = [pl.BlockSpec((nk, tm, d), lambda i: (0, i, 0)),
                pl.BlockSpec((tm, nk), lambda i: (tile0 + i, 0)),
                pl.BlockSpec((tm, d), lambda i: (tile0 + i, 0)),
                pl.BlockSpec((None, 1, d), lambda i: (((tile0 + i) * tm) // seq, 0, 0)),
                pl.BlockSpec((1, d), lambda i: (0, 0))]
    args = [picked, gates, x1, gate, final_g]
    aliases = {}
    if out_prev is not None:
        in_specs.append(pl.BlockSpec(memory_space=pl.ANY))
        args.append(out_prev)
        aliases = {len(args) - 1: 0}
    return pl.pallas_call(
        _combine_kernel,
        grid=(m // tm,),
        in_specs=in_specs,
        out_specs=pl.BlockSpec((tm, d), lambda i: (tile0 + i, 0)),
        out_shape=jax.ShapeDtypeStruct((total, d), F32),
        input_output_aliases=aliases,
        compiler_params=_cparams(("parallel",)),
    )(*args)


def kernel(x, c, ctx, c_ctx, ada_w, ada_b, norm1_g, norm2_g, in_w, conv_w, conv_b, hy_w1, hy_b1, hy_w2, hy_b2, hy_w3, hy_b3, hy_freq, hy_w4, hy_bias, hy_norm_g, rw_w0, rw_w1, rw_w2, rw_a0, rw_a1, rw_a2, rw_kk, rw_ka, rw_rk, rw_g1, rw_g2, rw_lnx_g, rw_lnx_b, out_w, router_w, router_b, ex_w_gate, ex_b_gate, ex_w_up, ex_b_up, ex_w_down, ex_b_down, final_g):
    assert ada_w.shape[0] == 1, "single-layer block: context outputs never reach a latent token"
    nb, seq, d = x.shape
    hy = hy_bias.shape[1]
    rw = rw_kk.shape[1]
    n_hy = 3 * hy
    proj = in_w.shape[2]
    n_exp = router_w.shape[2]

    cond = jnp.concatenate([jax.nn.silu(c), jax.nn.silu(c_ctx)[None]], axis=0)
    mod = _matmul(cond, ada_w[0], F32, cond.shape[0], 1024) + ada_b[0]
    mod_x = [m[:, None, :] for m in jnp.split(mod[:nb], 6, axis=-1)]
    mod_c = [m[:, None, :] for m in jnp.split(mod[nb:], 6, axis=-1)]

    lora_w = jnp.concatenate([rw_w1[0, 0], rw_w1[0, 1], rw_a1[0, 0], rw_a1[0, 1], rw_g1[0]], axis=1)
    n_lora = lora_w.shape[1]
    nz = -(-(proj + n_lora) // 512) * 512
    pad = nz - proj - n_lora
    w_all = jnp.concatenate([in_w[0], lora_w, jnp.zeros((d, pad), F32)], axis=1).astype(BF16)
    pass_taps = jnp.concatenate([jnp.zeros((1, n_lora + pad), F32), jnp.ones((1, n_lora + pad), F32),
                                 jnp.zeros((1, n_lora + pad), F32)], axis=0)
    cw_all = jnp.concatenate([conv_w[0], pass_taps], axis=1)
    cb_all = jnp.concatenate([conv_b[0], jnp.zeros((n_lora + pad,), F32)])[None]
    g1 = norm1_g[0][None]
    zx = _inproj(x, mod_x[0], mod_x[1], g1, w_all, cw_all, cb_all, 1024, 512)
    zc = _inproj(ctx, mod_c[0], mod_c[1], g1, w_all[:, n_hy:], cw_all[:, n_hy:], cb_all[:, n_hy:], 256, 512)

    h_fwd, h_bwd = _hyena_filter(seq, hy_w1[0], hy_b1[0], hy_w2[0], hy_b2[0], hy_w3[0], hy_b3[0],
                                 hy_freq[0], hy_w4[0])
    fwd, inv = _dft_matrices(seq)
    taps = _taps_spectrum(fwd, h_fwd, h_bwd)
    spec = _dft_fwd(fwd, zx, hy, 1024, 512)
    y_hy = _dft_inv(inv, spec, taps, zx, hy_bias[0][None], hy_norm_g[0][None], 1024, 512)

    scan_args = (rw_w2[0].astype(BF16), rw_a2[0].astype(BF16), rw_w0[0], rw_a0[0],
                 rw_kk[0][None], rw_ka[0][None], rw_rk[0].reshape(1, rw))
    s0 = jnp.zeros((2, nb, rw // PACK_W, PACK_W, PACK_W), F32)
    _, _, s_ctx = _rwkv_scan(zc, 0, proj - n_hy, *scan_args, s0)
    ys, bonus, _ = _rwkv_scan(zx, n_hy, proj, *scan_args, s_ctx)

    ne = -(-n_exp // LANE) * LANE
    rw_pad = jnp.pad(router_w[0], ((0, 0), (0, ne - n_exp)))
    rw_hi = rw_pad.astype(BF16)
    rw_lo = (rw_pad - rw_hi.astype(F32)).astype(BF16)
    rb = jnp.pad(router_b[0], (0, ne - n_exp))[None]
    x1, h2, logits = _outproj(y_hy, ys, bonus, zx, proj + n_lora - rw_g1.shape[-1], rw_g2[0].astype(BF16),
                              rw_lnx_g[0][None], rw_lnx_b[0][None], out_w[0].astype(BF16), x,
                              mod_x[2], norm2_g[0][None], mod_x[3], mod_x[4], rw_hi, rw_lo, rb, 512)

    ys, dest_km, gates = _moe(h2, logits, n_exp, ex_w_gate[0], ex_b_gate[0], ex_w_up[0], ex_b_up[0],
                              ex_w_down[0], ex_b_down[0])
    n_tok = nb * seq
    tm = _tile(seq, 256)
    n_parts = next(p for p in (COMBINE_PARTS, 2, 1) if (n_tok // tm) % p == 0)
    tq = n_tok // n_parts
    out = None
    for q in range(n_parts):
        picked = ys[dest_km[:, q * tq:(q + 1) * tq]]
        out = _combine(picked, gates, x1, mod_x[5], final_g[None], seq, tm, q * (tq // tm), out)
    return out.reshape(nb, seq, d)
```

```python
import functools
import math

import jax
import jax.numpy as jnp
from jax import lax
from jax.experimental import pallas as pl
from jax.experimental.pallas import tpu as pltpu

F32 = jnp.float32
BF16 = jnp.bfloat16

HEAD = 64
CHUNK = 64
PACK = 4
PACK_W = PACK * HEAD
SCAN_BATCH = 4
HALO = 16
INPROJ_ROWS = 256
SUB_ROWS = 256
FILTER_BANDS = 16
FILTER_TARGET = 1e-2
FAST_DECAY_PCT = 0.3
SLOW_DECAY_PCT = 1.5
TOP_K = 4
SWIGLU_LIMIT = 7.0
SWIGLU_ALPHA = 1.702
NORM_EPS = 1e-6
LNX_EPS = 64e-5
MOE_TILE = 512
ROUTE_TILE = 512
MOE_PARTS = 4
COMBINE_PARTS = 1
LANE = 128
PROJ_TILE = (1024, 512)
CTX_PROJ_TILE = (256, 512)
DFT_TILE = (1024, 512)
OUTPROJ_ROWS = 512
COMBINE_ROWS = 256
FILTER_ROWS = 512
MOE_COLS = 1024
MM_TILE = 1024
VMEM_LIMIT = 56 * 1024 * 1024


def _cparams(sem):
    return pltpu.CompilerParams(dimension_semantics=sem, vmem_limit_bytes=VMEM_LIMIT)


def _tile(n, want):
    if n <= want:
        return n
    t = want
    while n % t:
        t //= 2
    assert t >= 8, (n, want)
    return t


def _dot(a, b):
    return jnp.dot(a.astype(BF16), b.astype(BF16), preferred_element_type=F32)


def _group_ones(width):
    r = lax.broadcasted_iota(jnp.int32, (width, width), 0) // HEAD
    c = lax.broadcasted_iota(jnp.int32, (width, width), 1) // HEAD
    return jnp.where(r == c, 1.0, 0.0).astype(BF16)


def _group_sum(x, ones):
    width = ones.shape[0]
    parts = [_dot(x[:, o:o + width], ones) for o in range(0, x.shape[1], width)]
    return parts[0] if len(parts) == 1 else jnp.concatenate(parts, axis=1)


def _mm_kernel(a_ref, b_ref, o_ref):
    o_ref[...] = _dot(a_ref[...], b_ref[...]).astype(o_ref.dtype)


def _matmul(a, b, out_dtype, tm, tn):
    m, k = a.shape
    n = b.shape[1]
    tm, tn = _tile(m, tm), _tile(n, tn)
    return pl.pallas_call(
        _mm_kernel,
        grid=(n // tn, m // tm),
        in_specs=[pl.BlockSpec((tm, k), lambda j, i: (i, 0)),
                  pl.BlockSpec((k, tn), lambda j, i: (0, j))],
        out_specs=pl.BlockSpec((tm, tn), lambda j, i: (i, j)),
        out_shape=jax.ShapeDtypeStruct((m, n), out_dtype),
        compiler_params=_cparams(("parallel", "parallel")),
    )(a, b)


def _bmm_shared_lhs(a, b, out_dtype, tm, tn):
    m, k = a.shape
    nb, _, n = b.shape
    tm, tn = _tile(m, tm), _tile(n, tn)
    return pl.pallas_call(
        _mm_kernel,
        grid=(m // tm, nb, n // tn),
        in_specs=[pl.BlockSpec((tm, k), lambda i, bb, j: (i, 0)),
                  pl.BlockSpec((None, k, tn), lambda i, bb, j: (bb, 0, j))],
        out_specs=pl.BlockSpec((None, tm, tn), lambda i, bb, j: (bb, i, j)),
        out_shape=jax.ShapeDtypeStruct((nb, m, n), out_dtype),
        compiler_params=_cparams(("parallel", "parallel", "parallel")),
    )(a, b)


def _inproj_kernel(seq, xm_ref, xp_ref, xn_ref, shift_ref, scale_ref, g_ref, w_ref, cw_ref, cb_ref,
                   z_ref, h_ref, zz_ref):
    tm = xm_ref.shape[0]
    i = pl.program_id(0)

    @pl.when(pl.program_id(1) == 0)
    def _():
        def norm(xv):
            y = xv * lax.rsqrt(jnp.mean(xv * xv, axis=-1, keepdims=True) + NORM_EPS) * g_ref[...]
            return y * (1.0 + scale_ref[...]) + shift_ref[...]
        keep_prev = jnp.where((i * tm) % seq == 0, 0.0, 1.0)
        keep_next = jnp.where(((i + 1) * tm) % seq == 0, 0.0, 1.0)
        h_ref[0:HALO] = (norm(xp_ref[...]) * keep_prev).astype(BF16)
        h_ref[HALO:HALO + tm] = norm(xm_ref[...]).astype(BF16)
        h_ref[HALO + tm:] = (norm(xn_ref[...]) * keep_next).astype(BF16)

    cw = cw_ref[...]
    n_chunks = max(1, tm // INPROJ_ROWS)
    q = tm // n_chunks
    d0 = o0 = 0
    for ci in range(n_chunks):
        last = ci + 1 == n_chunks
        d1 = tm + 2 * HALO if last else (ci + 1) * q + HALO
        o1 = tm if last else d1 - 2 * HALO
        zz_ref[d0:d1] = jnp.dot(h_ref[d0:d1], w_ref[...], preferred_element_type=F32)
        z_ref[o0:o1] = (zz_ref[HALO - 1 + o0:HALO - 1 + o1] * cw[0:1] + zz_ref[HALO + o0:HALO + o1] * cw[1:2]
                        + zz_ref[HALO + 1 + o0:HALO + 1 + o1] * cw[2:3] + cb_ref[...]).astype(z_ref.dtype)
        d0, o0 = d1, o1


def _inproj(x, shift, scale, g, w, cw, cb, tm, tn):
    nb, seq, d = x.shape
    n = w.shape[1]
    tm, tn = _tile(seq, tm), _tile(n, tn)
    m = nb * seq
    n_halo = m // HALO
    per_batch = shift.shape[0] > 1
    mod_spec = pl.BlockSpec((None, 1, d), (lambda i, j: ((i * tm) // seq, 0, 0)) if per_batch
                            else (lambda i, j: (0, 0, 0)))
    x2 = x.reshape(m, d)
    out = pl.pallas_call(
        functools.partial(_inproj_kernel, seq),
        grid=(m // tm, n // tn),
        in_specs=[pl.BlockSpec((tm, d), lambda i, j: (i, 0)),
                  pl.BlockSpec((HALO, d), lambda i, j: (jnp.maximum(i * (tm // HALO) - 1, 0), 0)),
                  pl.BlockSpec((HALO, d), lambda i, j: (jnp.minimum((i + 1) * (tm // HALO), n_halo - 1), 0)),
                  mod_spec, mod_spec,
                  pl.BlockSpec((1, d), lambda i, j: (0, 0)),
                  pl.BlockSpec((d, tn), lambda i, j: (0, j)),
                  pl.BlockSpec((3, tn), lambda i, j: (0, j)),
                  pl.BlockSpec((1, tn), lambda i, j: (0, j))],
        out_specs=pl.BlockSpec((tm, tn), lambda i, j: (i, j)),
        out_shape=jax.ShapeDtypeStruct((m, n), BF16),
        scratch_shapes=[pltpu.VMEM((tm + 2 * HALO, d), BF16), pltpu.VMEM((tm + 2 * HALO, tn), F32)],
        compiler_params=_cparams(("parallel", "arbitrary")),
    )(x2, x2, x2, shift, scale, g, w, cw, cb)
    return out.reshape(nb, seq, n)


def _dft_matrices(length):
    n = 2 * length
    f = lax.broadcasted_iota(jnp.int32, (length, length), 0)
    t = lax.broadcasted_iota(jnp.int32, (length, length), 1)
    ang = ((f * t) % n).astype(F32) * (2.0 * math.pi / n)
    cos, sin = jnp.cos(ang), jnp.sin(ang)
    nyq = jnp.where(t % 2 == 0, 1.0, -1.0)
    fwd = jnp.concatenate([cos, jnp.where(f == 0, nyq, -sin)], axis=0)
    f_t = f.T
    inv = jnp.concatenate([jnp.where(f_t == 0, 1.0, 2.0 * cos.T),
                           jnp.where(f_t == 0, nyq.T, -2.0 * sin.T)], axis=1)
    return fwd.astype(BF16), inv.astype(BF16)


def _dft_fwd_kernel(f_ref, x1_ref, v_ref, o_ref, u_ref):
    @pl.when(pl.program_id(2) == 0)
    def _():
        u_ref[...] = (x1_ref[...].astype(F32) * v_ref[...].astype(F32)).astype(BF16)

    o_ref[...] = jnp.dot(f_ref[...], u_ref[...], preferred_element_type=F32).astype(o_ref.dtype)


def _dft_fwd(fwd, z, hy, tm, tn):
    nb, seq, _ = z.shape
    tm, tn = _tile(2 * seq, tm), _tile(hy, tn)
    nj = hy // tn
    return pl.pallas_call(
        _dft_fwd_kernel,
        grid=(nb, nj, 2 * seq // tm),
        in_specs=[pl.BlockSpec((tm, seq), lambda b, j, i: (i, 0)),
                  pl.BlockSpec((None, seq, tn), lambda b, j, i: (b, 0, nj + j)),
                  pl.BlockSpec((None, seq, tn), lambda b, j, i: (b, 0, 2 * nj + j))],
        out_specs=pl.BlockSpec((None, tm, tn), lambda b, j, i: (b, i, j)),
        out_shape=jax.ShapeDtypeStruct((nb, 2 * seq, hy), BF16),
        scratch_shapes=[pltpu.VMEM((seq, tn), BF16)],
        compiler_params=_cparams(("parallel", "parallel", "arbitrary")),
    )(fwd, z, z)


def _dft_inv_kernel(g_ref, s_ref, t_ref, x0_ref, x1_ref, v_ref, bias_ref, ng_ref, o_ref, y_ref):
    half = s_ref.shape[0] // 2

    @pl.when(pl.program_id(2) == 0)
    def _():
        re = s_ref[:half].astype(F32)
        im = s_ref[half:].astype(F32)
        t_re = t_ref[:half].astype(F32)
        t_im = t_ref[half:].astype(F32)
        row0 = lax.broadcasted_iota(jnp.int32, (half, 1), 0) == 0
        y_ref[:half] = (re * t_re - jnp.where(row0, 0.0, im * t_im)).astype(BF16)
        y_ref[half:] = (im * jnp.where(row0, t_im, t_re) + jnp.where(row0, 0.0, re * t_im)).astype(BF16)

    conv = jnp.dot(g_ref[...], y_ref[...], preferred_element_type=F32)
    u = x1_ref[...].astype(F32) * v_ref[...].astype(F32)
    y = x0_ref[...].astype(F32) * (conv + bias_ref[...] * u)
    ms = _group_sum(y * y, _group_ones(min(y.shape[1], PACK_W))) * (1.0 / HEAD)
    o_ref[...] = (y * lax.rsqrt(ms + NORM_EPS) * ng_ref[...]).astype(o_ref.dtype)


def _dft_inv(inv, spec, taps, z, bias, norm_g, tm, tn):
    nb, seq, _ = z.shape
    hy = spec.shape[2]
    tm, tn = _tile(seq, tm), _tile(hy, tn)
    nj = hy // tn
    row = lambda k: pl.BlockSpec((None, tm, tn), lambda b, j, i: (b, i, k * nj + j))
    vec = pl.BlockSpec((1, tn), lambda b, j, i: (0, j))
    return pl.pallas_call(
        _dft_inv_kernel,
        grid=(nb, nj, seq // tm),
        in_specs=[pl.BlockSpec((tm, 2 * seq), lambda b, j, i: (i, 0)),
                  pl.BlockSpec((None, 2 * seq, tn), lambda b, j, i: (b, 0, j)),
                  pl.BlockSpec((2 * seq, tn), lambda b, j, i: (0, j)),
                  row(0), row(1), row(2), vec, vec],
        out_specs=pl.BlockSpec((None, tm, tn), lambda b, j, i: (b, i, j)),
        out_shape=jax.ShapeDtypeStruct((nb, seq, hy), BF16),
        scratch_shapes=[pltpu.VMEM((2 * seq, tn), BF16)],
        compiler_params=_cparams(("parallel", "parallel", "arbitrary")),
    )(inv, spec, taps, z, z, z, bias, norm_g)


def _dot3(a, b):
    a_hi = a.astype(BF16)
    b_hi = b.astype(BF16)
    a_lo = (a - a_hi.astype(F32)).astype(BF16)
    b_lo = (b - b_hi.astype(F32)).astype(BF16)
    dot = lambda p, q: jnp.dot(p, q, preferred_element_type=F32)
    return dot(a_hi, b_hi) + dot(a_hi, b_lo) + dot(a_lo, b_hi)


def _filter_kernel(f_ref, t_ref, w1_ref, b1_ref, w2_ref, b2_ref, w3_ref, b3_ref, fr_ref, w4_ref, dl_ref, o_ref):
    freq = fr_ref[...]
    h = jnp.sin(freq * (_dot3(f_ref[...], w1_ref[...]) + b1_ref[...]))
    h = jnp.sin(freq * (_dot3(h, w2_ref[...]) + b2_ref[...]))
    h = jnp.sin(freq * (_dot3(h, w3_ref[...]) + b3_ref[...]))
    o_ref[...] = _dot3(h, w4_ref[...]) * jnp.exp(-t_ref[...] * dl_ref[...])


def _hyena_filter(length, w1, b1, w2, b2, w3, b3, freq, w4):
    hy = w4.shape[1] // 2
    width = w2.shape[0]
    t = jnp.linspace(0.0, 1.0, length, dtype=F32)[:, None]
    ang = (2.0 * math.pi / length) * jnp.arange(length, dtype=F32)[:, None]
    bands = jnp.linspace(1e-4, FILTER_BANDS - 1, FILTER_BANDS, dtype=F32)[None, :]
    feats = jnp.concatenate([t, jnp.cos(bands * ang), -jnp.sin(bands * ang)], axis=-1)
    emb = feats.shape[1]
    emb_pad = -(-emb // LANE) * LANE
    feats = jnp.pad(feats, ((0, 0), (0, emb_pad - emb)))
    w1p = jnp.pad(w1, ((0, emb_pad - emb), (0, 0)))
    deltas = jnp.abs(jnp.linspace(math.log(FILTER_TARGET) / SLOW_DECAY_PCT,
                                  math.log(FILTER_TARGET) / FAST_DECAY_PCT, hy, dtype=F32))
    tm = _tile(length, FILTER_ROWS)
    full = lambda r, c: pl.BlockSpec((r, c), lambda i: (0, 0))
    h = pl.pallas_call(
        _filter_kernel,
        grid=(length // tm,),
        in_specs=[pl.BlockSpec((tm, emb_pad), lambda i: (i, 0)), pl.BlockSpec((tm, 1), lambda i: (i, 0)),
                  full(emb_pad, width), full(1, width), full(width, width), full(1, width),
                  full(width, width), full(1, width), full(1, width), full(width, 2 * hy), full(1, 2 * hy)],
        out_specs=pl.BlockSpec((tm, 2 * hy), lambda i: (i, 0)),
        out_shape=jax.ShapeDtypeStruct((length, 2 * hy), F32),
        compiler_params=_cparams(("parallel",)),
    )(feats, t, w1p, b1[None], w2, b2[None], w3, b3[None], freq[None], w4, jnp.tile(deltas, 2)[None])
    return h[:, :hy], h[:, hy:]


def _taps_spectrum(fwd, h_fwd, h_bwd):
    length = h_fwd.shape[0]
    sig = jnp.stack([h_fwd, h_bwd.at[0].set(0.0)]).astype(BF16)
    spec = _bmm_shared_lhs(fwd, sig, F32, MM_TILE, MM_TILE)
    re, im = spec[:, :length], spec[:, length:]
    scale = 1.0 / (2 * length)
    row0 = lax.broadcasted_iota(jnp.int32, (length, 1), 0) == 0
    t_re = (re[0] + re[1]) * scale
    t_im = jnp.where(row0, im[0] + im[1], im[0] - im[1]) * scale
    return jnp.concatenate([t_re, t_im], axis=0).astype(BF16)


def _scan_kernel(r_ref, k_ref, v_ref, lw_ref, la_ref, w2_ref, a2_ref, w0_ref, a0_ref, kks_ref, ka_ref, rk_ref,
                 s0_ref, y_ref, bonus_ref, sfin_ref, ht_ref):
    c = CHUNK
    w = PACK_W
    direction = pl.program_id(0)
    step = pl.program_id(2)
    n_batch = r_ref.shape[0]
    n_groups = r_ref.shape[2] // w
    lora = w2_ref.shape[0]

    @pl.when(step == 0)
    def _():
        ht_ref[...] = s0_ref[...]

    sign = 1 - 2 * direction
    d_sq = (lax.broadcasted_iota(jnp.int32, (c, c), 1) - lax.broadcasted_iota(jnp.int32, (c, c), 0)) * sign
    incl_sq = jnp.where(d_sq <= 0, 1.0, 0.0).astype(BF16)
    d_c = (lax.broadcasted_iota(jnp.int32, (c, w), 1) % c - lax.broadcasted_iota(jnp.int32, (c, w), 0)) * sign
    strict_c = d_c < 0
    incl_c = d_c <= 0
    eye_c = jnp.where(d_c == 0, 1.0, 0.0)
    bd_mask = (lax.broadcasted_iota(jnp.int32, (w, w), 0) // c) == (lax.broadcasted_iota(jnp.int32, (w, w), 1) // HEAD)
    ones_bd = jnp.where(bd_mask, 1.0, 0.0).astype(BF16)

    def bd(x):
        xb = x.astype(BF16)
        return jnp.where(bd_mask, jnp.concatenate([xb] * PACK, axis=0), jnp.zeros((), BF16))

    fwd = direction == 0
    incl_2 = jnp.concatenate([incl_sq, incl_sq], axis=1)
    prep = []
    for bi in range(n_batch):
        r_all = r_ref[bi].astype(F32)
        k_all = k_ref[bi].astype(F32)
        v_all = v_ref[bi].astype(F32)
        lw_pre = lw_ref[bi].astype(F32)
        la_pre = la_ref[bi].astype(F32)
        lw_pre = jnp.where(fwd, lw_pre[:, :lora], lw_pre[:, lora:])
        la_pre = jnp.where(fwd, la_pre[:, :lora], la_pre[:, lora:])
        x_w = w0_ref[...] + _dot(jnp.tanh(lw_pre), w2_ref[...])
        lw_all = -math.exp(-0.5) * jax.nn.sigmoid(x_w)
        a_sig = jax.nn.sigmoid(a0_ref[...] + _dot(la_pre, a2_ref[...]))
        kk = k_all * kks_ref[...]
        kk = kk * lax.rsqrt(jnp.maximum(_group_sum(kk * kk, ones_bd), 1e-24))
        kd_all = k_all * (1.0 + (a_sig - 1.0) * ka_ref[...])
        bb_all = kk * a_sig
        bonus_ref[bi] = (_group_sum(r_all * kd_all * rk_ref[...], ones_bd) * v_all).astype(bonus_ref.dtype)

        lw_hi = lw_all.astype(BF16)
        lw_lo = (lw_all - lw_hi.astype(F32)).astype(BF16)
        cum = jnp.dot(incl_2, jnp.concatenate([lw_hi, lw_lo], axis=0), preferred_element_type=F32)
        tot = jnp.sum(lw_all, axis=0, keepdims=True)
        g_inv = jnp.exp(-cum)
        g_rem = jnp.exp(tot - cum)
        prep.append(dict(v=v_all, g_tot=jnp.exp(tot), a_t=-kk * jnp.exp(cum - lw_all), r_t=r_all * jnp.exp(cum),
                         b_t=bb_all * g_inv, k_t=kd_all * g_inv, b_s=bb_all * g_rem, k_s=kd_all * g_rem))

    probs = [(bi, g) for bi in range(n_batch) for g in range(n_groups)]
    idx = range(len(probs))
    cols = [slice(g * w, (g + 1) * w) for _, g in probs]
    take = lambda name: [prep[bi][name][:, cols[q]] for q, (bi, _) in enumerate(probs)]
    v_g, a_t, r_t, b_t, k_t, b_s, k_s = (take(nm) for nm in ("v", "a_t", "r_t", "b_t", "k_t", "b_s", "k_s"))
    hts = [ht_ref[bi, g] for bi, g in probs]
    lhs = [jnp.concatenate([a_t[q], r_t[q]], axis=0) for q in idx]
    scores = [_dot(lhs[q], jnp.concatenate([bd(b_t[q]).T, bd(k_t[q]).T], axis=1)) for q in idx]
    h0 = [_dot(lhs[q], hts[q].T) for q in idx]
    a_ab = [jnp.where(strict_c, s[:c, :w], 0.0) for s in scores]
    a_ak = [jnp.where(strict_c, s[:c, w:], 0.0) for s in scores]
    a_rb = [jnp.where(incl_c, s[c:, :w], 0.0) for s in scores]
    a_rk = [jnp.where(incl_c, s[c:, w:], 0.0) for s in scores]
    bd_v = [bd(vq) for vq in v_g]
    av = [_dot(jnp.concatenate([a_ak[q], a_rk[q]], axis=0), bd_v[q]) for q in idx]
    x = [h0[q][:c] + av[q][:c] for q in idx]

    t_inv = [eye_c + a for a in a_ab]
    p = [_dot(a, bd(a)) for a in a_ab]
    n_sq = int(math.log2(c)) - 1
    for it in range(n_sq):
        bd_p = [bd(pq) for pq in p]
        if it + 1 < n_sq:
            pt = [_dot(jnp.concatenate([p[q], t_inv[q]], axis=0), bd_p[q]) for q in idx]
            p = [m[:c] for m in pt]
            t_inv = [t_inv[q] + pt[q][c:] for q in idx]
        else:
            t_inv = [t_inv[q] + _dot(t_inv[q], bd_p[q]) for q in idx]
    u = [_dot(t_inv[q], bd(x[q])) for q in idx]

    y = [h0[q][c:] + av[q][c:] + _dot(a_rb[q], bd(u[q])) for q in idx]
    for q, (bi, _) in enumerate(probs):
        y_ref[bi, :, cols[q]] = y[q].astype(y_ref.dtype)

    for q, (bi, g) in enumerate(probs):
        uv = jnp.concatenate([u[q], v_g[q]], axis=0).astype(BF16)
        bk = jnp.concatenate([b_s[q], k_s[q]], axis=0).astype(BF16)
        upd = lax.dot_general(uv, bk, (((0,), (0,)), ((), ())), preferred_element_type=F32)
        ht_ref[bi, g] = hts[q] * prep[bi]["g_tot"][:, cols[q]] + jnp.where(bd_mask, upd, 0.0)

    @pl.when(step == pl.num_programs(2) - 1)
    def _():
        sfin_ref[...] = ht_ref[...]


def _rwkv_scan(z, col0, lora_col0, w2, a2, w0, a0, kks, ka, rk, s0):
    nb, seq, _ = z.shape
    lora, ch = w2.shape[1:]
    c = CHUNK
    assert CHUNK == HEAD and ch % PACK_W == 0 and seq % c == 0 and 2 * lora == LANE
    assert col0 % ch == 0 and lora_col0 % LANE == 0
    nc = seq // c
    ng = ch // PACK_W
    bs = SCAN_BATCH if nb % SCAN_BATCH == 0 else 1
    chunk_of = lambda n, i: jnp.where(n == 0, i, nc - 1 - i)
    zcol = lambda k: pl.BlockSpec((bs, c, ch), lambda n, b, i: (b, chunk_of(n, i), col0 // ch + k))
    zlora = lambda k: pl.BlockSpec((bs, c, LANE), lambda n, b, i: (b, chunk_of(n, i), lora_col0 // LANE + k))
    per_dir_w = pl.BlockSpec((None, lora, ch), lambda n, b, i: (n, 0, 0))
    per_dir_v = pl.BlockSpec((None, 1, ch), lambda n, b, i: (n, 0, 0))
    shared_v = pl.BlockSpec((1, ch), lambda n, b, i: (0, 0))
    state = pl.BlockSpec((None, bs, ng, PACK_W, PACK_W), lambda n, b, i: (n, b, 0, 0, 0))
    out = pl.BlockSpec((None, bs, c, ch), lambda n, b, i: (n, b, chunk_of(n, i), 0))
    return pl.pallas_call(
        _scan_kernel,
        grid=(2, nb // bs, nc),
        in_specs=[zcol(0), zcol(1), zcol(2), zlora(0), zlora(1), per_dir_w, per_dir_w, per_dir_v, per_dir_v,
                  shared_v, shared_v, shared_v, state],
        out_specs=[out, out, state],
        out_shape=[jax.ShapeDtypeStruct((2, nb, seq, ch), BF16), jax.ShapeDtypeStruct((2, nb, seq, ch), BF16),
                   jax.ShapeDtypeStruct((2, nb, ng, PACK_W, PACK_W), F32)],
        scratch_shapes=[pltpu.VMEM((bs, ng, PACK_W, PACK_W), F32)],
        compiler_params=_cparams(("parallel", "parallel", "arbitrary")),
    )(z, z, z, z, z, w2, a2, w0.reshape(2, 1, ch), a0.reshape(2, 1, ch), kks, ka, rk, s0)


def _outproj_kernel(yh_ref, ys0_ref, ys1_ref, b0_ref, b1_ref, lg_ref, g2_ref, lng_ref, lnb_ref, ow_ref,
                    x_ref, gate_ref, n2g_ref, shift_ref, scale_ref, rwh_ref, rwl_ref, rb_ref,
                    x1_ref, h2_ref, lg_out_ref, mix_ref):
    hy = yh_ref.shape[1]
    tm = x_ref.shape[0]
    ones = _group_ones(PACK_W)
    rows = _tile(tm, SUB_ROWS)
    subs = [slice(r0, r0 + rows) for r0 in range(0, tm, rows)]

    for rs in subs:
        y = ys0_ref[rs].astype(F32) + ys1_ref[rs].astype(F32)
        mu = _group_sum(y, ones) * (1.0 / HEAD)
        yc = y - mu
        var = _group_sum(yc * yc, ones) * (1.0 / HEAD)
        y = yc * lax.rsqrt(var + LNX_EPS) * lng_ref[...] + lnb_ref[...]
        gate = _dot(jax.nn.sigmoid(lg_ref[rs].astype(F32)), g2_ref[...])
        y = (y + b0_ref[rs].astype(F32) + b1_ref[rs].astype(F32)) * gate
        mix_ref[rs, :hy] = yh_ref[rs]
        mix_ref[rs, hy:] = y.astype(BF16)
    mixes = [jnp.dot(mix_ref[rs], ow_ref[...], preferred_element_type=F32) for rs in subs]
    for rs, mix in zip(subs, mixes):
        x1 = x_ref[rs] + gate_ref[...] * mix
        x1_ref[rs] = x1
        h2 = x1 * lax.rsqrt(jnp.mean(x1 * x1, axis=-1, keepdims=True) + NORM_EPS) * n2g_ref[...]
        h2 = h2 * (1.0 + scale_ref[...]) + shift_ref[...]
        h2_ref[rs] = h2.astype(BF16)
        h_hi = h2.astype(BF16)
        h_lo = (h2 - h_hi.astype(F32)).astype(BF16)
        logits = (jnp.dot(h_hi, rwh_ref[...], preferred_element_type=F32)
                  + jnp.dot(h_hi, rwl_ref[...], preferred_element_type=F32)
                  + jnp.dot(h_lo, rwh_ref[...], preferred_element_type=F32))
        lg_out_ref[rs] = logits + rb_ref[...]


def _outproj(y_hy, ys, bonus, z, lora_g_col, g2, lnx_g, lnx_b, out_w, x, gate, n2g, shift, scale,
             rw_hi, rw_lo, rb, tm):
    nb, seq, d = x.shape
    m = nb * seq
    hy = y_hy.shape[2]
    rw = ys.shape[3]
    tm = _tile(seq, tm)
    lg = g2.shape[0]
    ne = rw_hi.shape[1]
    assert lora_g_col % lg == 0
    ys2 = ys.reshape(2, m, rw)
    bn2 = bonus.reshape(2, m, rw)
    nz = z.shape[2]
    full = lambda r, c: pl.BlockSpec((r, c), lambda i: (0, 0))
    dir_spec = lambda n: pl.BlockSpec((None, tm, rw), lambda i: (n, i, 0))
    mod_spec = pl.BlockSpec((None, 1, d), lambda i: ((i * tm) // seq, 0, 0))
    return pl.pallas_call(
        _outproj_kernel,
        grid=(m // tm,),
        in_specs=[pl.BlockSpec((tm, hy), lambda i: (i, 0)), dir_spec(0), dir_spec(1), dir_spec(0), dir_spec(1),
                  pl.BlockSpec((tm, lg), lambda i: (i, lora_g_col // lg)),
                  full(lg, rw), full(1, rw), full(1, rw), full(hy + rw, d),
                  pl.BlockSpec((tm, d), lambda i: (i, 0)), mod_spec, full(1, d), mod_spec, mod_spec,
                  full(d, ne), full(d, ne), full(1, ne)],
        out_specs=[pl.BlockSpec((tm, d), lambda i: (i, 0)), pl.BlockSpec((tm, d), lambda i: (i, 0)),
                   pl.BlockSpec((tm, ne), lambda i: (i, 0))],
        out_shape=[jax.ShapeDtypeStruct((m, d), F32), jax.ShapeDtypeStruct((m, d), BF16),
                   jax.ShapeDtypeStruct((m, ne), F32)],
        scratch_shapes=[pltpu.VMEM((tm, hy + rw), BF16)],
        compiler_params=_cparams(("parallel",)),
    )(y_hy.reshape(m, hy), ys2, ys2, bn2, bn2, z.reshape(m, nz), g2, lnx_g, lnx_b, out_w,
      x.reshape(m, d), gate, n2g, shift, scale, rw_hi, rw_lo, rb)


def _expert_changed(te_ref, i):
    prev = te_ref[jnp.maximum(i - 1, 0)]
    return jnp.logical_or(i == 0, te_ref[i] != prev)


def _moe_up_kernel(te_ref, nu_ref, x_ref, wg_ref, wu_ref, bg_ref, bu_ref, h_ref, wg_bf, wu_bf):
    i = pl.program_id(1)

    @pl.when(_expert_changed(te_ref, i))
    def _():
        wg_bf[...] = wg_ref[...].astype(BF16)
        wu_bf[...] = wu_ref[...].astype(BF16)

    @pl.when(i < nu_ref[0])
    def _():
        x = x_ref[...]
        g = jnp.dot(x, wg_bf[...], preferred_element_type=F32) + bg_ref[...]
        u = jnp.dot(x, wu_bf[...], preferred_element_type=F32) + bu_ref[...]
        g = jnp.minimum(g, SWIGLU_LIMIT)
        u = jnp.clip(u, -SWIGLU_LIMIT, SWIGLU_LIMIT)
        h_ref[...] = ((u + 1.0) * (g * jax.nn.sigmoid(SWIGLU_ALPHA * g))).astype(h_ref.dtype)

    @pl.when(i >= nu_ref[0])
    def _():
        h_ref[...] = jnp.zeros_like(h_ref)


def _moe_down_kernel(te_ref, nu_ref, h_ref, wd_ref, bd_ref, *rest):
    y_ref, wd_bf = rest[-2:]
    i = pl.program_id(1)

    @pl.when(_expert_changed(te_ref, i))
    def _():
        wd_bf[...] = wd_ref[...].astype(BF16)

    @pl.when(i < nu_ref[0])
    def _():
        y = jnp.dot(h_ref[...], wd_bf[...], preferred_element_type=F32) + bd_ref[...]
        y_ref[...] = y.astype(y_ref.dtype)

    @pl.when(i >= nu_ref[0])
    def _():
        y_ref[...] = jnp.zeros_like(y_ref)


def _moe_up(tile_expert, n_used, xs, wg, wu, bg, bu, tn):
    n_rows, d = xs.shape
    e, _, f = wg.shape
    tm = MOE_TILE
    tn = _tile(f, tn)
    w_spec = pl.BlockSpec((None, d, tn), lambda j, i, te, nu: (te[i], 0, j))
    b_spec = pl.BlockSpec((None, 1, tn), lambda j, i, te, nu: (te[i], 0, j))
    return pl.pallas_call(
        _moe_up_kernel,
        grid_spec=pltpu.PrefetchScalarGridSpec(
            num_scalar_prefetch=2,
            grid=(f // tn, n_rows // tm),
            in_specs=[pl.BlockSpec((tm, d), lambda j, i, te, nu: (i, 0)), w_spec, w_spec, b_spec, b_spec],
            out_specs=pl.BlockSpec((tm, tn), lambda j, i, te, nu: (i, j)),
            scratch_shapes=[pltpu.VMEM((d, tn), BF16), pltpu.VMEM((d, tn), BF16)]),
        out_shape=jax.ShapeDtypeStruct((n_rows, f), BF16),
        compiler_params=_cparams(("arbitrary", "arbitrary")),
    )(tile_expert, n_used, xs, wg, wu, bg.reshape(e, 1, f), bu.reshape(e, 1, f))


def _moe_down(tile_expert, n_used, hs, wd, bdn, tn, total_rows, tile0, ys_prev):
    n_rows, f = hs.shape
    e, _, d = wd.shape
    tm = MOE_TILE
    tn = _tile(d, tn)
    in_specs = [pl.BlockSpec((tm, f), lambda j, i, te, nu: (i, 0)),
                pl.BlockSpec((None, f, tn), lambda j, i, te, nu: (te[i], 0, j)),
                pl.BlockSpec((None, 1, tn), lambda j, i, te, nu: (te[i], 0, j))]
    args = [tile_expert, n_used, hs, wd, bdn.reshape(e, 1, d)]
    aliases = {}
    if ys_prev is not None:
        in_specs.append(pl.BlockSpec(memory_space=pl.ANY))
        args.append(ys_prev)
        aliases = {len(args) - 1: 0}
    return pl.pallas_call(
        _moe_down_kernel,
        grid_spec=pltpu.PrefetchScalarGridSpec(
            num_scalar_prefetch=2,
            grid=(d // tn, n_rows // tm),
            in_specs=in_specs,
            out_specs=pl.BlockSpec((tm, tn), lambda j, i, te, nu: (tile0 + i, j)),
            scratch_shapes=[pltpu.VMEM((f, tn), BF16)]),
        out_shape=jax.ShapeDtypeStruct((total_rows, d), BF16),
        input_output_aliases=aliases,
        compiler_params=_cparams(("arbitrary", "arbitrary")),
    )(*args)


def _route_kernel(n_exp, lg_ref, idx_ref, gate_ref, rank_ref, cnt_ref, carry_ref):
    tm, lanes = lg_ref.shape

    @pl.when(pl.program_id(0) == 0)
    def _():
        carry_ref[...] = jnp.zeros_like(carry_ref)

    lane = lax.broadcasted_iota(jnp.int32, (tm, lanes), 1).astype(F32)
    lg = jnp.where(lane < n_exp, lg_ref[...], -jnp.inf)
    tops, hots, idxs = [], [], []
    for _ in range(TOP_K):
        m = jnp.max(lg, axis=-1, keepdims=True)
        idx = jnp.min(jnp.where(lg == m, lane, float(lanes)), axis=-1, keepdims=True)
        hot = lane == idx
        lg = jnp.where(hot, -jnp.inf, lg)
        tops.append(m)
        hots.append(hot)
        idxs.append(idx)
    exps = [jnp.exp(t - tops[0]) for t in tops]
    inv = 1.0 / sum(exps)
    occ = sum(jnp.where(h, 1.0, 0.0) for h in hots)
    earlier = jnp.where(lax.broadcasted_iota(jnp.int32, (tm, tm), 0) > lax.broadcasted_iota(jnp.int32, (tm, tm), 1),
                        1.0, 0.0).astype(BF16)
    before = jnp.dot(earlier, occ.astype(BF16), preferred_element_type=F32) + carry_ref[...]
    ranks = [jnp.sum(jnp.where(h, before, 0.0), axis=-1, keepdims=True) for h in hots]
    carry_ref[...] += jnp.sum(occ, axis=0, keepdims=True)
    cnt_ref[...] = carry_ref[...]
    spread = lambda cols: sum(jnp.where(lane == float(k), c, 0.0) for k, c in enumerate(cols))
    idx_ref[...] = spread(idxs)
    gate_ref[...] = spread([e * inv for e in exps])
    rank_ref[...] = spread(ranks)


def _route(logits, n_exp, tm):
    n_tok, lanes = logits.shape
    tm = _tile(n_tok, tm)
    row = pl.BlockSpec((tm, lanes), lambda i: (i, 0))
    out = jax.ShapeDtypeStruct((n_tok, lanes), F32)
    idx, gates, rank, cnt = pl.pallas_call(
        functools.partial(_route_kernel, n_exp),
        grid=(n_tok // tm,),
        in_specs=[row],
        out_specs=[row, row, row, pl.BlockSpec((1, lanes), lambda i: (0, 0))],
        out_shape=[out, out, out, jax.ShapeDtypeStruct((1, lanes), F32)],
        scratch_shapes=[pltpu.VMEM((1, lanes), F32)],
        compiler_params=_cparams(("arbitrary",)),
    )(logits)
    return (idx[:, :TOP_K].astype(jnp.int32), gates[:, :TOP_K], rank[:, :TOP_K].astype(jnp.int32),
            cnt[0, :n_exp].astype(jnp.int32))


def _moe(h2, logits, e, wg, bg, wu, bu, wd, bdn):
    n_tok, d = h2.shape
    tm = MOE_TILE
    top_idx, gates, rank, counts = _route(logits, e, ROUTE_TILE)
    n_asg = n_tok * TOP_K
    raw_start = jnp.cumsum(counts) - counts
    padded = (counts + tm - 1) // tm * tm
    pad_end = jnp.cumsum(padded)
    pad_start = pad_end - padded
    dest = (pad_start[top_idx] + rank).reshape(n_asg)
    tok = jnp.arange(n_asg, dtype=jnp.int32) // TOP_K
    _, order_tok = lax.sort((dest, tok), num_keys=1)
    n_tiles = -(-(n_asg + e * (tm - 1)) // tm)
    n_rows = n_tiles * tm
    tile_start = jnp.arange(n_tiles, dtype=jnp.int32) * tm
    tile_expert = jnp.minimum(jnp.sum((pad_end[None, :] <= tile_start[:, None]).astype(jnp.int32), axis=1), e - 1)
    n_used = pad_end[-1] // tm
    row = jnp.arange(n_rows, dtype=jnp.int32)
    row_e = jnp.repeat(tile_expert, tm)
    row_rank = row - pad_start[row_e]
    src = jnp.clip(raw_start[row_e] + row_rank, 0, n_asg - 1)
    row_tok = jnp.where(row_rank < counts[row_e], order_tok[src], 0)

    n_parts = next(p for p in (MOE_PARTS, 2, 1) if n_tiles % p == 0)
    tp = n_tiles // n_parts
    ys = None
    for q in range(n_parts):
        te_q = tile_expert[q * tp:(q + 1) * tp]
        nu_q = jnp.clip(n_used - q * tp, 0, tp).astype(jnp.int32).reshape(1)
        xs = h2[row_tok[q * tp * tm:(q + 1) * tp * tm]]
        hs = _moe_up(te_q, nu_q, xs, wg, wu, bg, bu, MOE_COLS)
        ys = _moe_down(te_q, nu_q, hs, wd, bdn, MOE_COLS, n_rows, q * tp, ys)
    return ys, dest.reshape(n_tok, TOP_K).T, gates


def _combine_kernel(y_ref, g_ref, x1_ref, gate_ref, fg_ref, *rest):
    o_ref = rest[-1]
    g = g_ref[...]
    acc = y_ref[0].astype(F32) * g[:, 0:1]
    for k in range(1, y_ref.shape[0]):
        acc = acc + y_ref[k].astype(F32) * g[:, k:k + 1]
    xo = x1_ref[...] + gate_ref[...] * acc
    o_ref[...] = xo * lax.rsqrt(jnp.mean(xo * xo, axis=-1, keepdims=True) + NORM_EPS) * fg_ref[...]


def _combine(picked, gates, x1, gate, final_g, seq, tm, tile0, out_prev):
    nk, m, d = picked.shape
    total = x1.shape[0]
    in_specs = [pl.BlockSpec((nk, tm, d), lambda i: (0, i, 0)),
                pl.BlockSpec((tm, nk), lambda i: (tile0 + i, 0)),
                pl.BlockSpec((tm, d), lambda i: (tile0 + i, 0)),
                pl.BlockSpec((None, 1, d), lambda i: (((tile0 + i) * tm) // seq, 0, 0)),
                pl.BlockSpec((1, d), lambda i: (0, 0))]
    args = [picked, gates, x1, gate, final_g]
    aliases = {}
    if out_prev is not None:
        in_specs.append(pl.BlockSpec(memory_space=pl.ANY))
        args.append(out_prev)
        aliases = {len(args) - 1: 0}
    return pl.pallas_call(
        _combine_kernel,
        grid=(m // tm,),
        in_specs=in_specs,
        out_specs=pl.BlockSpec((tm, d), lambda i: (tile0 + i, 0)),
        out_shape=jax.ShapeDtypeStruct((total, d), F32),
        input_output_aliases=aliases,
        compiler_params=_cparams(("parallel",)),
    )(*args)


def kernel(x, c, ctx, c_ctx, ada_w, ada_b, norm1_g, norm2_g, in_w, conv_w, conv_b, hy_w1, hy_b1, hy_w2, hy_b2, hy_w3, hy_b3, hy_freq, hy_w4, hy_bias, hy_norm_g, rw_w0, rw_w1, rw_w2, rw_a0, rw_a1, rw_a2, rw_kk, rw_ka, rw_rk, rw_g1, rw_g2, rw_lnx_g, rw_lnx_b, out_w, router_w, router_b, ex_w_gate, ex_b_gate, ex_w_up, ex_b_up, ex_w_down, ex_b_down, final_g):
    assert ada_w.shape[0] == 1, "single-layer block: context outputs never reach a latent token"
    nb, seq, d = x.shape
    hy = hy_bias.shape[1]
    rw = rw_kk.shape[1]
    n_hy = 3 * hy
    proj = in_w.shape[2]
    n_exp = router_w.shape[2]

    cond = jnp.concatenate([jax.nn.silu(c), jax.nn.silu(c_ctx)[None]], axis=0)
    mod = _matmul(cond, ada_w[0], F32, cond.shape[0], MM_TILE) + ada_b[0]
    mod_x = [m[:, None, :] for m in jnp.split(mod[:nb], 6, axis=-1)]
    mod_c = [m[:, None, :] for m in jnp.split(mod[nb:], 6, axis=-1)]

    lora_w = jnp.concatenate([rw_w1[0, 0], rw_w1[0, 1], rw_a1[0, 0], rw_a1[0, 1], rw_g1[0]], axis=1)
    n_lora = lora_w.shape[1]
    nz = -(-(proj + n_lora) // 512) * 512
    pad = nz - proj - n_lora
    w_all = jnp.concatenate([in_w[0], lora_w, jnp.zeros((d, pad), F32)], axis=1).astype(BF16)
    pass_taps = jnp.concatenate([jnp.zeros((1, n_lora + pad), F32), jnp.ones((1, n_lora + pad), F32),
                                 jnp.zeros((1, n_lora + pad), F32)], axis=0)
    cw_all = jnp.concatenate([conv_w[0], pass_taps], axis=1)
    cb_all = jnp.concatenate([conv_b[0], jnp.zeros((n_lora + pad,), F32)])[None]
    g1 = norm1_g[0][None]
    zx = _inproj(x, mod_x[0], mod_x[1], g1, w_all, cw_all, cb_all, *PROJ_TILE)
    zc = _inproj(ctx, mod_c[0], mod_c[1], g1, w_all[:, n_hy:], cw_all[:, n_hy:], cb_all[:, n_hy:], *CTX_PROJ_TILE)

    h_fwd, h_bwd = _hyena_filter(seq, hy_w1[0], hy_b1[0], hy_w2[0], hy_b2[0], hy_w3[0], hy_b3[0],
                                 hy_freq[0], hy_w4[0])
    fwd, inv = _dft_matrices(seq)
    taps = _taps_spectrum(fwd, h_fwd, h_bwd)
    spec = _dft_fwd(fwd, zx, hy, *DFT_TILE)
    y_hy = _dft_inv(inv, spec, taps, zx, hy_bias[0][None], hy_norm_g[0][None], *DFT_TILE)

    scan_args = (rw_w2[0].astype(BF16), rw_a2[0].astype(BF16), rw_w0[0], rw_a0[0],
                 rw_kk[0][None], rw_ka[0][None], rw_rk[0].reshape(1, rw))
    s0 = jnp.zeros((2, nb, rw // PACK_W, PACK_W, PACK_W), F32)
    _, _, s_ctx = _rwkv_scan(zc, 0, proj - n_hy, *scan_args, s0)
    ys, bonus, _ = _rwkv_scan(zx, n_hy, proj, *scan_args, s_ctx)

    ne = -(-n_exp // LANE) * LANE
    rw_pad = jnp.pad(router_w[0], ((0, 0), (0, ne - n_exp)))
    rw_hi = rw_pad.astype(BF16)
    rw_lo = (rw_pad - rw_hi.astype(F32)).astype(BF16)
    rb = jnp.pad(router_b[0], (0, ne - n_exp))[None]
    x1, h2, logits = _outproj(y_hy, ys, bonus, zx, proj + n_lora - rw_g1.shape[-1], rw_g2[0].astype(BF16),
                              rw_lnx_g[0][None], rw_lnx_b[0][None], out_w[0].astype(BF16), x,
                              mod_x[2], norm2_g[0][None], mod_x[3], mod_x[4], rw_hi, rw_lo, rb, OUTPROJ_ROWS)

    ys, dest_km, gates = _moe(h2, logits, n_exp, ex_w_gate[0], ex_b_gate[0], ex_w_up[0], ex_b_up[0],
                              ex_w_down[0], ex_b_down[0])
    n_tok = nb * seq
    tm = _tile(seq, COMBINE_ROWS)
    n_parts = next(p for p in (COMBINE_PARTS, 2, 1) if (n_tok // tm) % p == 0)
    tq = n_tok // n_parts
    out = None
    for q in range(n_parts):
        picked = ys[dest_km[:, q * tq:(q + 1) * tq]]
        out = _combine(picked, gates, x1, mod_x[5], final_g[None], seq, tm, q * (tq // tm), out)
    return out.reshape(nb, seq, d)
```

```python
import functools
import math

import jax
import jax.numpy as jnp
from jax import lax
from jax.experimental import pallas as pl
from jax.experimental.pallas import tpu as pltpu

F32 = jnp.float32
BF16 = jnp.bfloat16

HEAD = 64
CHUNK = 64
PACK = 4
PACK_W = PACK * HEAD
SCAN_BATCH = 4
HALO = 16
INPROJ_ROWS = 256
SUB_ROWS = 256
FILTER_BANDS = 16
FILTER_TARGET = 1e-2
FAST_DECAY_PCT = 0.3
SLOW_DECAY_PCT = 1.5
TOP_K = 4
SWIGLU_LIMIT = 7.0
SWIGLU_ALPHA = 1.702
NORM_EPS = 1e-6
LNX_EPS = 64e-5
MOE_TILE = 512
ROUTE_TILE = 512
MOE_PARTS = 4
COMBINE_PARTS = 4
LANE = 128
PROJ_TILE = (1024, 512)
CTX_PROJ_TILE = (256, 512)
DFT_TILE = (1024, 512)
OUTPROJ_ROWS = 512
COMBINE_ROWS = 256
FILTER_ROWS = 512
MOE_COLS = 1024
MM_TILE = 1024
VMEM_LIMIT = 56 * 1024 * 1024


def _cparams(sem):
    return pltpu.CompilerParams(dimension_semantics=sem, vmem_limit_bytes=VMEM_LIMIT)


def _tile(n, want):
    if n <= want:
        return n
    t = want
    while n % t:
        t //= 2
    assert t >= 8, (n, want)
    return t


def _dot(a, b):
    return jnp.dot(a.astype(BF16), b.astype(BF16), preferred_element_type=F32)


def _group_ones(width):
    r = lax.broadcasted_iota(jnp.int32, (width, width), 0) // HEAD
    c = lax.broadcasted_iota(jnp.int32, (width, width), 1) // HEAD
    return jnp.where(r == c, 1.0, 0.0).astype(BF16)


def _group_sum(x, ones):
    width = ones.shape[0]
    parts = [_dot(x[:, o:o + width], ones) for o in range(0, x.shape[1], width)]
    return parts[0] if len(parts) == 1 else jnp.concatenate(parts, axis=1)


def _mm_kernel(a_ref, b_ref, o_ref):
    o_ref[...] = _dot(a_ref[...], b_ref[...]).astype(o_ref.dtype)


def _matmul(a, b, out_dtype, tm, tn):
    m, k = a.shape
    n = b.shape[1]
    tm, tn = _tile(m, tm), _tile(n, tn)
    return pl.pallas_call(
        _mm_kernel,
        grid=(n // tn, m // tm),
        in_specs=[pl.BlockSpec((tm, k), lambda j, i: (i, 0)),
                  pl.BlockSpec((k, tn), lambda j, i: (0, j))],
        out_specs=pl.BlockSpec((tm, tn), lambda j, i: (i, j)),
        out_shape=jax.ShapeDtypeStruct((m, n), out_dtype),
        compiler_params=_cparams(("parallel", "parallel")),
    )(a, b)


def _bmm_shared_lhs(a, b, out_dtype, tm, tn):
    m, k = a.shape
    nb, _, n = b.shape
    tm, tn = _tile(m, tm), _tile(n, tn)
    return pl.pallas_call(
        _mm_kernel,
        grid=(m // tm, nb, n // tn),
        in_specs=[pl.BlockSpec((tm, k), lambda i, bb, j: (i, 0)),
                  pl.BlockSpec((None, k, tn), lambda i, bb, j: (bb, 0, j))],
        out_specs=pl.BlockSpec((None, tm, tn), lambda i, bb, j: (bb, i, j)),
        out_shape=jax.ShapeDtypeStruct((nb, m, n), out_dtype),
        compiler_params=_cparams(("parallel", "parallel", "parallel")),
    )(a, b)


def _inproj_kernel(seq, xm_ref, xp_ref, xn_ref, shift_ref, scale_ref, g_ref, w_ref, cw_ref, cb_ref,
                   z_ref, h_ref, zz_ref):
    tm = xm_ref.shape[0]
    i = pl.program_id(0)

    @pl.when(pl.program_id(1) == 0)
    def _():
        def norm(xv):
            y = xv * lax.rsqrt(jnp.mean(xv * xv, axis=-1, keepdims=True) + NORM_EPS) * g_ref[...]
            return y * (1.0 + scale_ref[...]) + shift_ref[...]
        keep_prev = jnp.where((i * tm) % seq == 0, 0.0, 1.0)
        keep_next = jnp.where(((i + 1) * tm) % seq == 0, 0.0, 1.0)
        h_ref[0:HALO] = (norm(xp_ref[...]) * keep_prev).astype(BF16)
        h_ref[HALO:HALO + tm] = norm(xm_ref[...]).astype(BF16)
        h_ref[HALO + tm:] = (norm(xn_ref[...]) * keep_next).astype(BF16)

    cw = cw_ref[...]
    n_chunks = max(1, tm // INPROJ_ROWS)
    q = tm // n_chunks
    d0 = o0 = 0
    for ci in range(n_chunks):
        last = ci + 1 == n_chunks
        d1 = tm + 2 * HALO if last else (ci + 1) * q + HALO
        o1 = tm if last else d1 - 2 * HALO
        zz_ref[d0:d1] = jnp.dot(h_ref[d0:d1], w_ref[...], preferred_element_type=F32)
        z_ref[o0:o1] = (zz_ref[HALO - 1 + o0:HALO - 1 + o1] * cw[0:1] + zz_ref[HALO + o0:HALO + o1] * cw[1:2]
                        + zz_ref[HALO + 1 + o0:HALO + 1 + o1] * cw[2:3] + cb_ref[...]).astype(z_ref.dtype)
        d0, o0 = d1, o1


def _inproj(x, shift, scale, g, w, cw, cb, tm, tn):
    nb, seq, d = x.shape
    n = w.shape[1]
    tm, tn = _tile(seq, tm), _tile(n, tn)
    m = nb * seq
    n_halo = m // HALO
    per_batch = shift.shape[0] > 1
    mod_spec = pl.BlockSpec((None, 1, d), (lambda i, j: ((i * tm) // seq, 0, 0)) if per_batch
                            else (lambda i, j: (0, 0, 0)))
    x2 = x.reshape(m, d)
    out = pl.pallas_call(
        functools.partial(_inproj_kernel, seq),
        grid=(m // tm, n // tn),
        in_specs=[pl.BlockSpec((tm, d), lambda i, j: (i, 0)),
                  pl.BlockSpec((HALO, d), lambda i, j: (jnp.maximum(i * (tm // HALO) - 1, 0), 0)),
                  pl.BlockSpec((HALO, d), lambda i, j: (jnp.minimum((i + 1) * (tm // HALO), n_halo - 1), 0)),
                  mod_spec, mod_spec,
                  pl.BlockSpec((1, d), lambda i, j: (0, 0)),
                  pl.BlockSpec((d, tn), lambda i, j: (0, j)),
                  pl.BlockSpec((3, tn), lambda i, j: (0, j)),
                  pl.BlockSpec((1, tn), lambda i, j: (0, j))],
        out_specs=pl.BlockSpec((tm, tn), lambda i, j: (i, j)),
        out_shape=jax.ShapeDtypeStruct((m, n), BF16),
        scratch_shapes=[pltpu.VMEM((tm + 2 * HALO, d), BF16), pltpu.VMEM((tm + 2 * HALO, tn), F32)],
        compiler_params=_cparams(("parallel", "arbitrary")),
    )(x2, x2, x2, shift, scale, g, w, cw, cb)
    return out.reshape(nb, seq, n)


def _dft_matrices(length):
    n = 2 * length
    f = lax.broadcasted_iota(jnp.int32, (length, length), 0)
    t = lax.broadcasted_iota(jnp.int32, (length, length), 1)
    ang = ((f * t) % n).astype(F32) * (2.0 * math.pi / n)
    cos, sin = jnp.cos(ang), jnp.sin(ang)
    nyq = jnp.where(t % 2 == 0, 1.0, -1.0)
    fwd = jnp.concatenate([cos, jnp.where(f == 0, nyq, -sin)], axis=0)
    f_t = f.T
    inv = jnp.concatenate([jnp.where(f_t == 0, 1.0, 2.0 * cos.T),
                           jnp.where(f_t == 0, nyq.T, -2.0 * sin.T)], axis=1)
    return fwd.astype(BF16), inv.astype(BF16)


def _dft_fwd_kernel(f_ref, x1_ref, v_ref, o_ref, u_ref):
    @pl.when(pl.program_id(2) == 0)
    def _():
        u_ref[...] = (x1_ref[...].astype(F32) * v_ref[...].astype(F32)).astype(BF16)

    o_ref[...] = jnp.dot(f_ref[...], u_ref[...], preferred_element_type=F32).astype(o_ref.dtype)


def _dft_fwd(fwd, z, hy, tm, tn):
    nb, seq, _ = z.shape
    tm, tn = _tile(2 * seq, tm), _tile(hy, tn)
    nj = hy // tn
    return pl.pallas_call(
        _dft_fwd_kernel,
        grid=(nb, nj, 2 * seq // tm),
        in_specs=[pl.BlockSpec((tm, seq), lambda b, j, i: (i, 0)),
                  pl.BlockSpec((None, seq, tn), lambda b, j, i: (b, 0, nj + j)),
                  pl.BlockSpec((None, seq, tn), lambda b, j, i: (b, 0, 2 * nj + j))],
        out_specs=pl.BlockSpec((None, tm, tn), lambda b, j, i: (b, i, j)),
        out_shape=jax.ShapeDtypeStruct((nb, 2 * seq, hy), BF16),
        scratch_shapes=[pltpu.VMEM((seq, tn), BF16)],
        compiler_params=_cparams(("parallel", "parallel", "arbitrary")),
    )(fwd, z, z)


def _dft_inv_kernel(g_ref, s_ref, t_ref, x0_ref, x1_ref, v_ref, bias_ref, ng_ref, o_ref, y_ref):
    half = s_ref.shape[0] // 2

    @pl.when(pl.program_id(2) == 0)
    def _():
        re = s_ref[:half].astype(F32)
        im = s_ref[half:].astype(F32)
        t_re = t_ref[:half].astype(F32)
        t_im = t_ref[half:].astype(F32)
        row0 = lax.broadcasted_iota(jnp.int32, (half, 1), 0) == 0
        y_ref[:half] = (re * t_re - jnp.where(row0, 0.0, im * t_im)).astype(BF16)
        y_ref[half:] = (im * jnp.where(row0, t_im, t_re) + jnp.where(row0, 0.0, re * t_im)).astype(BF16)

    conv = jnp.dot(g_ref[...], y_ref[...], preferred_element_type=F32)
    u = x1_ref[...].astype(F32) * v_ref[...].astype(F32)
    y = x0_ref[...].astype(F32) * (conv + bias_ref[...] * u)
    ms = _group_sum(y * y, _group_ones(min(y.shape[1], PACK_W))) * (1.0 / HEAD)
    o_ref[...] = (y * lax.rsqrt(ms + NORM_EPS) * ng_ref[...]).astype(o_ref.dtype)


def _dft_inv(inv, spec, taps, z, bias, norm_g, tm, tn):
    nb, seq, _ = z.shape
    hy = spec.shape[2]
    tm, tn = _tile(seq, tm), _tile(hy, tn)
    nj = hy // tn
    row = lambda k: pl.BlockSpec((None, tm, tn), lambda b, j, i: (b, i, k * nj + j))
    vec = pl.BlockSpec((1, tn), lambda b, j, i: (0, j))
    return pl.pallas_call(
        _dft_inv_kernel,
        grid=(nb, nj, seq // tm),
        in_specs=[pl.BlockSpec((tm, 2 * seq), lambda b, j, i: (i, 0)),
                  pl.BlockSpec((None, 2 * seq, tn), lambda b, j, i: (b, 0, j)),
                  pl.BlockSpec((2 * seq, tn), lambda b, j, i: (0, j)),
                  row(0), row(1), row(2), vec, vec],
        out_specs=pl.BlockSpec((None, tm, tn), lambda b, j, i: (b, i, j)),
        out_shape=jax.ShapeDtypeStruct((nb, seq, hy), BF16),
        scratch_shapes=[pltpu.VMEM((2 * seq, tn), BF16)],
        compiler_params=_cparams(("parallel", "parallel", "arbitrary")),
    )(inv, spec, taps, z, z, z, bias, norm_g)


def _dot3(a, b):
    a_hi = a.astype(BF16)
    b_hi = b.astype(BF16)
    a_lo = (a - a_hi.astype(F32)).astype(BF16)
    b_lo = (b - b_hi.astype(F32)).astype(BF16)
    dot = lambda p, q: jnp.dot(p, q, preferred_element_type=F32)
    return dot(a_hi, b_hi) + dot(a_hi, b_lo) + dot(a_lo, b_hi)


def _filter_kernel(f_ref, t_ref, w1_ref, b1_ref, w2_ref, b2_ref, w3_ref, b3_ref, fr_ref, w4_ref, dl_ref, o_ref):
    freq = fr_ref[...]
    h = jnp.sin(freq * (_dot3(f_ref[...], w1_ref[...]) + b1_ref[...]))
    h = jnp.sin(freq * (_dot3(h, w2_ref[...]) + b2_ref[...]))
    h = jnp.sin(freq * (_dot3(h, w3_ref[...]) + b3_ref[...]))
    o_ref[...] = _dot3(h, w4_ref[...]) * jnp.exp(-t_ref[...] * dl_ref[...])


def _hyena_filter(length, w1, b1, w2, b2, w3, b3, freq, w4):
    hy = w4.shape[1] // 2
    width = w2.shape[0]
    t = jnp.linspace(0.0, 1.0, length, dtype=F32)[:, None]
    ang = (2.0 * math.pi / length) * jnp.arange(length, dtype=F32)[:, None]
    bands = jnp.linspace(1e-4, FILTER_BANDS - 1, FILTER_BANDS, dtype=F32)[None, :]
    feats = jnp.concatenate([t, jnp.cos(bands * ang), -jnp.sin(bands * ang)], axis=-1)
    emb = feats.shape[1]
    emb_pad = -(-emb // LANE) * LANE
    feats = jnp.pad(feats, ((0, 0), (0, emb_pad - emb)))
    w1p = jnp.pad(w1, ((0, emb_pad - emb), (0, 0)))
    deltas = jnp.abs(jnp.linspace(math.log(FILTER_TARGET) / SLOW_DECAY_PCT,
                                  math.log(FILTER_TARGET) / FAST_DECAY_PCT, hy, dtype=F32))
    tm = _tile(length, FILTER_ROWS)
    full = lambda r, c: pl.BlockSpec((r, c), lambda i: (0, 0))
    h = pl.pallas_call(
        _filter_kernel,
        grid=(length // tm,),
        in_specs=[pl.BlockSpec((tm, emb_pad), lambda i: (i, 0)), pl.BlockSpec((tm, 1), lambda i: (i, 0)),
                  full(emb_pad, width), full(1, width), full(width, width), full(1, width),
                  full(width, width), full(1, width), full(1, width), full(width, 2 * hy), full(1, 2 * hy)],
        out_specs=pl.BlockSpec((tm, 2 * hy), lambda i: (i, 0)),
        out_shape=jax.ShapeDtypeStruct((length, 2 * hy), F32),
        compiler_params=_cparams(("parallel",)),
    )(feats, t, w1p, b1[None], w2, b2[None], w3, b3[None], freq[None], w4, jnp.tile(deltas, 2)[None])
    return h[:, :hy], h[:, hy:]


def _taps_spectrum(fwd, h_fwd, h_bwd):
    length = h_fwd.shape[0]
    sig = jnp.stack([h_fwd, h_bwd.at[0].set(0.0)]).astype(BF16)
    spec = _bmm_shared_lhs(fwd, sig, F32, MM_TILE, MM_TILE)
    re, im = spec[:, :length], spec[:, length:]
    scale = 1.0 / (2 * length)
    row0 = lax.broadcasted_iota(jnp.int32, (length, 1), 0) == 0
    t_re = (re[0] + re[1]) * scale
    t_im = jnp.where(row0, im[0] + im[1], im[0] - im[1]) * scale
    return jnp.concatenate([t_re, t_im], axis=0).astype(BF16)


def _scan_kernel(r_ref, k_ref, v_ref, lw_ref, la_ref, w2_ref, a2_ref, w0_ref, a0_ref, kks_ref, ka_ref, rk_ref,
                 s0_ref, y_ref, bonus_ref, sfin_ref, ht_ref):
    c = CHUNK
    w = PACK_W
    direction = pl.program_id(0)
    step = pl.program_id(2)
    n_batch = r_ref.shape[0]
    n_groups = r_ref.shape[2] // w
    lora = w2_ref.shape[0]

    @pl.when(step == 0)
    def _():
        ht_ref[...] = s0_ref[...]

    sign = 1 - 2 * direction
    d_sq = (lax.broadcasted_iota(jnp.int32, (c, c), 1) - lax.broadcasted_iota(jnp.int32, (c, c), 0)) * sign
    incl_sq = jnp.where(d_sq <= 0, 1.0, 0.0).astype(BF16)
    d_c = (lax.broadcasted_iota(jnp.int32, (c, w), 1) % c - lax.broadcasted_iota(jnp.int32, (c, w), 0)) * sign
    strict_c = d_c < 0
    incl_c = d_c <= 0
    eye_c = jnp.where(d_c == 0, 1.0, 0.0)
    bd_mask = (lax.broadcasted_iota(jnp.int32, (w, w), 0) // c) == (lax.broadcasted_iota(jnp.int32, (w, w), 1) // HEAD)
    ones_bd = jnp.where(bd_mask, 1.0, 0.0).astype(BF16)

    def bd(x):
        xb = x.astype(BF16)
        return jnp.where(bd_mask, jnp.concatenate([xb] * PACK, axis=0), jnp.zeros((), BF16))

    fwd = direction == 0
    incl_2 = jnp.concatenate([incl_sq, incl_sq], axis=1)
    prep = []
    for bi in range(n_batch):
        r_all = r_ref[bi].astype(F32)
        k_all = k_ref[bi].astype(F32)
        v_all = v_ref[bi].astype(F32)
        lw_pre = lw_ref[bi].astype(F32)
        la_pre = la_ref[bi].astype(F32)
        lw_pre = jnp.where(fwd, lw_pre[:, :lora], lw_pre[:, lora:])
        la_pre = jnp.where(fwd, la_pre[:, :lora], la_pre[:, lora:])
        x_w = w0_ref[...] + _dot(jnp.tanh(lw_pre), w2_ref[...])
        lw_all = -math.exp(-0.5) * jax.nn.sigmoid(x_w)
        a_sig = jax.nn.sigmoid(a0_ref[...] + _dot(la_pre, a2_ref[...]))
        kk = k_all * kks_ref[...]
        kk = kk * lax.rsqrt(jnp.maximum(_group_sum(kk * kk, ones_bd), 1e-24))
        kd_all = k_all * (1.0 + (a_sig - 1.0) * ka_ref[...])
        bb_all = kk * a_sig
        bonus_ref[bi] = (_group_sum(r_all * kd_all * rk_ref[...], ones_bd) * v_all).astype(bonus_ref.dtype)

        lw_hi = lw_all.astype(BF16)
        lw_lo = (lw_all - lw_hi.astype(F32)).astype(BF16)
        cum = jnp.dot(incl_2, jnp.concatenate([lw_hi, lw_lo], axis=0), preferred_element_type=F32)
        tot = jnp.sum(lw_all, axis=0, keepdims=True)
        g_inv = jnp.exp(-cum)
        g_rem = jnp.exp(tot - cum)
        prep.append(dict(v=v_all, g_tot=jnp.exp(tot), a_t=-kk * jnp.exp(cum - lw_all), r_t=r_all * jnp.exp(cum),
                         b_t=bb_all * g_inv, k_t=kd_all * g_inv, b_s=bb_all * g_rem, k_s=kd_all * g_rem))

    probs = [(bi, g) for bi in range(n_batch) for g in range(n_groups)]
    idx = range(len(probs))
    cols = [slice(g * w, (g + 1) * w) for _, g in probs]
    take = lambda name: [prep[bi][name][:, cols[q]] for q, (bi, _) in enumerate(probs)]
    v_g, a_t, r_t, b_t, k_t, b_s, k_s = (take(nm) for nm in ("v", "a_t", "r_t", "b_t", "k_t", "b_s", "k_s"))
    hts = [ht_ref[bi, g] for bi, g in probs]
    lhs = [jnp.concatenate([a_t[q], r_t[q]], axis=0) for q in idx]
    scores = [_dot(lhs[q], jnp.concatenate([bd(b_t[q]).T, bd(k_t[q]).T], axis=1)) for q in idx]
    h0 = [_dot(lhs[q], hts[q].T) for q in idx]
    a_ab = [jnp.where(strict_c, s[:c, :w], 0.0) for s in scores]
    a_ak = [jnp.where(strict_c, s[:c, w:], 0.0) for s in scores]
    a_rb = [jnp.where(incl_c, s[c:, :w], 0.0) for s in scores]
    a_rk = [jnp.where(incl_c, s[c:, w:], 0.0) for s in scores]
    bd_v = [bd(vq) for vq in v_g]
    av = [_dot(jnp.concatenate([a_ak[q], a_rk[q]], axis=0), bd_v[q]) for q in idx]
    x = [h0[q][:c] + av[q][:c] for q in idx]

    t_inv = [eye_c + a for a in a_ab]
    p = [_dot(a, bd(a)) for a in a_ab]
    n_sq = int(math.log2(c)) - 1
    for it in range(n_sq):
        bd_p = [bd(pq) for pq in p]
        if it + 1 < n_sq:
            pt = [_dot(jnp.concatenate([p[q], t_inv[q]], axis=0), bd_p[q]) for q in idx]
            p = [m[:c] for m in pt]
            t_inv = [t_inv[q] + pt[q][c:] for q in idx]
        else:
            t_inv = [t_inv[q] + _dot(t_inv[q], bd_p[q]) for q in idx]
    u = [_dot(t_inv[q], bd(x[q])) for q in idx]

    y = [h0[q][c:] + av[q][c:] + _dot(a_rb[q], bd(u[q])) for q in idx]
    for q, (bi, _) in enumerate(probs):
        y_ref[bi, :, cols[q]] = y[q].astype(y_ref.dtype)

    for q, (bi, g) in enumerate(probs):
        uv = jnp.concatenate([u[q], v_g[q]], axis=0).astype(BF16)
        bk = jnp.concatenate([b_s[q], k_s[q]], axis=0).astype(BF16)
        upd = lax.dot_general(uv, bk, (((0,), (0,)), ((), ())), preferred_element_type=F32)
        ht_ref[bi, g] = hts[q] * prep[bi]["g_tot"][:, cols[q]] + jnp.where(bd_mask, upd, 0.0)

    @pl.when(step == pl.num_programs(2) - 1)
    def _():
        sfin_ref[...] = ht_ref[...]


def _rwkv_scan(z, col0, lora_col0, w2, a2, w0, a0, kks, ka, rk, s0):
    nb, seq, _ = z.shape
    lora, ch = w2.shape[1:]
    c = CHUNK
    assert CHUNK == HEAD and ch % PACK_W == 0 and seq % c == 0 and 2 * lora == LANE
    assert col0 % ch == 0 and lora_col0 % LANE == 0
    nc = seq // c
    ng = ch // PACK_W
    bs = SCAN_BATCH if nb % SCAN_BATCH == 0 else 1
    chunk_of = lambda n, i: jnp.where(n == 0, i, nc - 1 - i)
    zcol = lambda k: pl.BlockSpec((bs, c, ch), lambda n, b, i: (b, chunk_of(n, i), col0 // ch + k))
    zlora = lambda k: pl.BlockSpec((bs, c, LANE), lambda n, b, i: (b, chunk_of(n, i), lora_col0 // LANE + k))
    per_dir_w = pl.BlockSpec((None, lora, ch), lambda n, b, i: (n, 0, 0))
    per_dir_v = pl.BlockSpec((None, 1, ch), lambda n, b, i: (n, 0, 0))
    shared_v = pl.BlockSpec((1, ch), lambda n, b, i: (0, 0))
    state = pl.BlockSpec((None, bs, ng, PACK_W, PACK_W), lambda n, b, i: (n, b, 0, 0, 0))
    out = pl.BlockSpec((None, bs, c, ch), lambda n, b, i: (n, b, chunk_of(n, i), 0))
    return pl.pallas_call(
        _scan_kernel,
        grid=(2, nb // bs, nc),
        in_specs=[zcol(0), zcol(1), zcol(2), zlora(0), zlora(1), per_dir_w, per_dir_w, per_dir_v, per_dir_v,
                  shared_v, shared_v, shared_v, state],
        out_specs=[out, out, state],
        out_shape=[jax.ShapeDtypeStruct((2, nb, seq, ch), BF16), jax.ShapeDtypeStruct((2, nb, seq, ch), BF16),
                   jax.ShapeDtypeStruct((2, nb, ng, PACK_W, PACK_W), F32)],
        scratch_shapes=[pltpu.VMEM((bs, ng, PACK_W, PACK_W), F32)],
        compiler_params=_cparams(("parallel", "parallel", "arbitrary")),
    )(z, z, z, z, z, w2, a2, w0.reshape(2, 1, ch), a0.reshape(2, 1, ch), kks, ka, rk, s0)


def _outproj_kernel(yh_ref, ys0_ref, ys1_ref, b0_ref, b1_ref, lg_ref, g2_ref, lng_ref, lnb_ref, ow_ref,
                    x_ref, gate_ref, n2g_ref, shift_ref, scale_ref, rwh_ref, rwl_ref, rb_ref,
                    x1_ref, h2_ref, lg_out_ref, mix_ref):
    hy = yh_ref.shape[1]
    tm = x_ref.shape[0]
    ones = _group_ones(PACK_W)
    rows = _tile(tm, SUB_ROWS)
    subs = [slice(r0, r0 + rows) for r0 in range(0, tm, rows)]

    for rs in subs:
        y = ys0_ref[rs].astype(F32) + ys1_ref[rs].astype(F32)
        mu = _group_sum(y, ones) * (1.0 / HEAD)
        yc = y - mu
        var = _group_sum(yc * yc, ones) * (1.0 / HEAD)
        y = yc * lax.rsqrt(var + LNX_EPS) * lng_ref[...] + lnb_ref[...]
        gate = _dot(jax.nn.sigmoid(lg_ref[rs].astype(F32)), g2_ref[...])
        y = (y + b0_ref[rs].astype(F32) + b1_ref[rs].astype(F32)) * gate
        mix_ref[rs, :hy] = yh_ref[rs]
        mix_ref[rs, hy:] = y.astype(BF16)
    mixes = [jnp.dot(mix_ref[rs], ow_ref[...], preferred_element_type=F32) for rs in subs]
    for rs, mix in zip(subs, mixes):
        x1 = x_ref[rs] + gate_ref[...] * mix
        x1_ref[rs] = x1
        h2 = x1 * lax.rsqrt(jnp.mean(x1 * x1, axis=-1, keepdims=True) + NORM_EPS) * n2g_ref[...]
        h2 = h2 * (1.0 + scale_ref[...]) + shift_ref[...]
        h2_ref[rs] = h2.astype(BF16)
        h_hi = h2.astype(BF16)
        h_lo = (h2 - h_hi.astype(F32)).astype(BF16)
        logits = (jnp.dot(h_hi, rwh_ref[...], preferred_element_type=F32)
                  + jnp.dot(h_hi, rwl_ref[...], preferred_element_type=F32)
                  + jnp.dot(h_lo, rwh_ref[...], preferred_element_type=F32))
        lg_out_ref[rs] = logits + rb_ref[...]


def _outproj(y_hy, ys, bonus, z, lora_g_col, g2, lnx_g, lnx_b, out_w, x, gate, n2g, shift, scale,
             rw_hi, rw_lo, rb, tm):
    nb, seq, d = x.shape
    m = nb * seq
    hy = y_hy.shape[2]
    rw = ys.shape[3]
    tm = _tile(seq, tm)
    lg = g2.shape[0]
    ne = rw_hi.shape[1]
    assert lora_g_col % lg == 0
    ys2 = ys.reshape(2, m, rw)
    bn2 = bonus.reshape(2, m, rw)
    nz = z.shape[2]
    full = lambda r, c: pl.BlockSpec((r, c), lambda i: (0, 0))
    dir_spec = lambda n: pl.BlockSpec((None, tm, rw), lambda i: (n, i, 0))
    mod_spec = pl.BlockSpec((None, 1, d), lambda i: ((i * tm) // seq, 0, 0))
    return pl.pallas_call(
        _outproj_kernel,
        grid=(m // tm,),
        in_specs=[pl.BlockSpec((tm, hy), lambda i: (i, 0)), dir_spec(0), dir_spec(1), dir_spec(0), dir_spec(1),
                  pl.BlockSpec((tm, lg), lambda i: (i, lora_g_col // lg)),
                  full(lg, rw), full(1, rw), full(1, rw), full(hy + rw, d),
                  pl.BlockSpec((tm, d), lambda i: (i, 0)), mod_spec, full(1, d), mod_spec, mod_spec,
                  full(d, ne), full(d, ne), full(1, ne)],
        out_specs=[pl.BlockSpec((tm, d), lambda i: (i, 0)), pl.BlockSpec((tm, d), lambda i: (i, 0)),
                   pl.BlockSpec((tm, ne), lambda i: (i, 0))],
        out_shape=[jax.ShapeDtypeStruct((m, d), F32), jax.ShapeDtypeStruct((m, d), BF16),
                   jax.ShapeDtypeStruct((m, ne), F32)],
        scratch_shapes=[pltpu.VMEM((tm, hy + rw), BF16)],
        compiler_params=_cparams(("parallel",)),
    )(y_hy.reshape(m, hy), ys2, ys2, bn2, bn2, z.reshape(m, nz), g2, lnx_g, lnx_b, out_w,
      x.reshape(m, d), gate, n2g, shift, scale, rw_hi, rw_lo, rb)


def _moe_up_kernel(te_ref, nu_ref, x_ref, wg_ref, wu_ref, bg_ref, bu_ref, h_ref):
    i = pl.program_id(1)

    @pl.when(i < nu_ref[0])
    def _():
        x = x_ref[...]
        g = jnp.dot(x, wg_ref[...].astype(BF16), preferred_element_type=F32) + bg_ref[...]
        u = jnp.dot(x, wu_ref[...].astype(BF16), preferred_element_type=F32) + bu_ref[...]
        g = jnp.minimum(g, SWIGLU_LIMIT)
        u = jnp.clip(u, -SWIGLU_LIMIT, SWIGLU_LIMIT)
        h_ref[...] = ((u + 1.0) * (g * jax.nn.sigmoid(SWIGLU_ALPHA * g))).astype(h_ref.dtype)

    @pl.when(i >= nu_ref[0])
    def _():
        h_ref[...] = jnp.zeros_like(h_ref)


def _moe_down_kernel(te_ref, nu_ref, h_ref, wd_ref, bd_ref, *rest):
    y_ref = rest[-1]
    i = pl.program_id(1)

    @pl.when(i < nu_ref[0])
    def _():
        y = jnp.dot(h_ref[...], wd_ref[...].astype(BF16), preferred_element_type=F32) + bd_ref[...]
        y_ref[...] = y.astype(y_ref.dtype)

    @pl.when(i >= nu_ref[0])
    def _():
        y_ref[...] = jnp.zeros_like(y_ref)


def _moe_up(tile_expert, n_used, xs, wg, wu, bg, bu, tn):
    n_rows, d = xs.shape
    e, _, f = wg.shape
    tm = MOE_TILE
    tn = _tile(f, tn)
    w_spec = pl.BlockSpec((None, d, tn), lambda j, i, te, nu: (te[i], 0, j))
    b_spec = pl.BlockSpec((None, 1, tn), lambda j, i, te, nu: (te[i], 0, j))
    return pl.pallas_call(
        _moe_up_kernel,
        grid_spec=pltpu.PrefetchScalarGridSpec(
            num_scalar_prefetch=2,
            grid=(f // tn, n_rows // tm),
            in_specs=[pl.BlockSpec((tm, d), lambda j, i, te, nu: (i, 0)), w_spec, w_spec, b_spec, b_spec],
            out_specs=pl.BlockSpec((tm, tn), lambda j, i, te, nu: (i, j))),
        out_shape=jax.ShapeDtypeStruct((n_rows, f), BF16),
        compiler_params=_cparams(("arbitrary", "arbitrary")),
    )(tile_expert, n_used, xs, wg, wu, bg.reshape(e, 1, f), bu.reshape(e, 1, f))


def _moe_down(tile_expert, n_used, hs, wd, bdn, tn, total_rows, tile0, ys_prev):
    n_rows, f = hs.shape
    e, _, d = wd.shape
    tm = MOE_TILE
    tn = _tile(d, tn)
    in_specs = [pl.BlockSpec((tm, f), lambda j, i, te, nu: (i, 0)),
                pl.BlockSpec((None, f, tn), lambda j, i, te, nu: (te[i], 0, j)),
                pl.BlockSpec((None, 1, tn), lambda j, i, te, nu: (te[i], 0, j))]
    args = [tile_expert, n_used, hs, wd, bdn.reshape(e, 1, d)]
    aliases = {}
    if ys_prev is not None:
        in_specs.append(pl.BlockSpec(memory_space=pl.ANY))
        args.append(ys_prev)
        aliases = {len(args) - 1: 0}
    return pl.pallas_call(
        _moe_down_kernel,
        grid_spec=pltpu.PrefetchScalarGridSpec(
            num_scalar_prefetch=2,
            grid=(d // tn, n_rows // tm),
            in_specs=in_specs,
            out_specs=pl.BlockSpec((tm, tn), lambda j, i, te, nu: (tile0 + i, j))),
        out_shape=jax.ShapeDtypeStruct((total_rows, d), BF16),
        input_output_aliases=aliases,
        compiler_params=_cparams(("arbitrary", "arbitrary")),
    )(*args)


def _route_kernel(n_exp, lg_ref, idx_ref, gate_ref, rank_ref, cnt_ref, carry_ref):
    tm, lanes = lg_ref.shape

    @pl.when(pl.program_id(0) == 0)
    def _():
        carry_ref[...] = jnp.zeros_like(carry_ref)

    lane = lax.broadcasted_iota(jnp.int32, (tm, lanes), 1).astype(F32)
    lg = jnp.where(lane < n_exp, lg_ref[...], -jnp.inf)
    tops, hots, idxs = [], [], []
    for _ in range(TOP_K):
        m = jnp.max(lg, axis=-1, keepdims=True)
        idx = jnp.min(jnp.where(lg == m, lane, float(lanes)), axis=-1, keepdims=True)
        hot = lane == idx
        lg = jnp.where(hot, -jnp.inf, lg)
        tops.append(m)
        hots.append(hot)
        idxs.append(idx)
    exps = [jnp.exp(t - tops[0]) for t in tops]
    inv = 1.0 / sum(exps)
    occ = sum(jnp.where(h, 1.0, 0.0) for h in hots)
    earlier = jnp.where(lax.broadcasted_iota(jnp.int32, (tm, tm), 0) > lax.broadcasted_iota(jnp.int32, (tm, tm), 1),
                        1.0, 0.0).astype(BF16)
    before = jnp.dot(earlier, occ.astype(BF16), preferred_element_type=F32) + carry_ref[...]
    ranks = [jnp.sum(jnp.where(h, before, 0.0), axis=-1, keepdims=True) for h in hots]
    carry_ref[...] += jnp.sum(occ, axis=0, keepdims=True)
    cnt_ref[...] = carry_ref[...]
    spread = lambda cols: sum(jnp.where(lane == float(k), c, 0.0) for k, c in enumerate(cols))
    idx_ref[...] = spread(idxs)
    gate_ref[...] = spread([e * inv for e in exps])
    rank_ref[...] = spread(ranks)


def _route(logits, n_exp, tm):
    n_tok, lanes = logits.shape
    tm = _tile(n_tok, tm)
    row = pl.BlockSpec((tm, lanes), lambda i: (i, 0))
    out = jax.ShapeDtypeStruct((n_tok, lanes), F32)
    idx, gates, rank, cnt = pl.pallas_call(
        functools.partial(_route_kernel, n_exp),
        grid=(n_tok // tm,),
        in_specs=[row],
        out_specs=[row, row, row, pl.BlockSpec((1, lanes), lambda i: (0, 0))],
        out_shape=[out, out, out, jax.ShapeDtypeStruct((1, lanes), F32)],
        scratch_shapes=[pltpu.VMEM((1, lanes), F32)],
        compiler_params=_cparams(("arbitrary",)),
    )(logits)
    return (idx[:, :TOP_K].astype(jnp.int32), gates[:, :TOP_K], rank[:, :TOP_K].astype(jnp.int32),
            cnt[0, :n_exp].astype(jnp.int32))


def _moe(h2, logits, e, wg, bg, wu, bu, wd, bdn):
    n_tok, d = h2.shape
    tm = MOE_TILE
    top_idx, gates, rank, counts = _route(logits, e, ROUTE_TILE)
    n_asg = n_tok * TOP_K
    raw_start = jnp.cumsum(counts) - counts
    padded = (counts + tm - 1) // tm * tm
    pad_end = jnp.cumsum(padded)
    pad_start = pad_end - padded
    dest = (pad_start[top_idx] + rank).reshape(n_asg)
    tok = jnp.arange(n_asg, dtype=jnp.int32) // TOP_K
    _, order_tok = lax.sort((dest, tok), num_keys=1)
    n_tiles = -(-(n_asg + e * (tm - 1)) // tm)
    n_rows = n_tiles * tm
    tile_start = jnp.arange(n_tiles, dtype=jnp.int32) * tm
    tile_expert = jnp.minimum(jnp.sum((pad_end[None, :] <= tile_start[:, None]).astype(jnp.int32), axis=1), e - 1)
    n_used = pad_end[-1] // tm
    row = jnp.arange(n_rows, dtype=jnp.int32)
    row_e = jnp.repeat(tile_expert, tm)
    row_rank = row - pad_start[row_e]
    src = jnp.clip(raw_start[row_e] + row_rank, 0, n_asg - 1)
    row_tok = jnp.where(row_rank < counts[row_e], order_tok[src], 0)

    n_parts = next(p for p in (MOE_PARTS, 2, 1) if n_tiles % p == 0)
    tp = n_tiles // n_parts
    ys = None
    for q in range(n_parts):
        te_q = tile_expert[q * tp:(q + 1) * tp]
        nu_q = jnp.clip(n_used - q * tp, 0, tp).astype(jnp.int32).reshape(1)
        xs = h2[row_tok[q * tp * tm:(q + 1) * tp * tm]]
        hs = _moe_up(te_q, nu_q, xs, wg, wu, bg, bu, MOE_COLS)
        ys = _moe_down(te_q, nu_q, hs, wd, bdn, MOE_COLS, n_rows, q * tp, ys)
    return ys, dest.reshape(n_tok, TOP_K).T, gates


def _combine_kernel(y_ref, g_ref, x1_ref, gate_ref, fg_ref, *rest):
    o_ref = rest[-1]
    g = g_ref[...]
    acc = y_ref[0].astype(F32) * g[:, 0:1]
    for k in range(1, y_ref.shape[0]):
        acc = acc + y_ref[k].astype(F32) * g[:, k:k + 1]
    xo = x1_ref[...] + gate_ref[...] * acc
    o_ref[...] = xo * lax.rsqrt(jnp.mean(xo * xo, axis=-1, keepdims=True) + NORM_EPS) * fg_ref[...]


def _combine(picked, gates, x1, gate, final_g, seq, tm, tile0, out_prev):
    nk, m, d = picked.shape
    total = x1.shape[0]
    in_specs = [pl.BlockSpec((nk, tm, d), lambda i: (0, i, 0)),
                pl.BlockSpec((tm, nk), lambda i: (tile0 + i, 0)),
                pl.BlockSpec((tm, d), lambda i: (tile0 + i, 0)),
                pl.BlockSpec((None, 1, d), lambda i: (((tile0 + i) * tm) // seq, 0, 0)),
                pl.BlockSpec((1, d), lambda i: (0, 0))]
    args = [picked, gates, x1, gate, final_g]
    aliases = {}
    if out_prev is not None:
        in_specs.append(pl.BlockSpec(memory_space=pl.ANY))
        args.append(out_prev)
        aliases = {len(args) - 1: 0}
    return pl.pallas_call(
        _combine_kernel,
        grid=(m // tm,),
        in_specs=in_specs,
        out_specs=pl.BlockSpec((tm, d), lambda i: (tile0 + i, 0)),
        out_shape=jax.ShapeDtypeStruct((total, d), F32),
        input_output_aliases=aliases,
        compiler_params=_cparams(("parallel",)),
    )(*args)


def kernel(x, c, ctx, c_ctx, ada_w, ada_b, norm1_g, norm2_g, in_w, conv_w, conv_b, hy_w1, hy_b1, hy_w2, hy_b2, hy_w3, hy_b3, hy_freq, hy_w4, hy_bias, hy_norm_g, rw_w0, rw_w1, rw_w2, rw_a0, rw_a1, rw_a2, rw_kk, rw_ka, rw_rk, rw_g1, rw_g2, rw_lnx_g, rw_lnx_b, out_w, router_w, router_b, ex_w_gate, ex_b_gate, ex_w_up, ex_b_up, ex_w_down, ex_b_down, final_g):
    assert ada_w.shape[0] == 1, "single-layer block: context outputs never reach a latent token"
    nb, seq, d = x.shape
    hy = hy_bias.shape[1]
    rw = rw_kk.shape[1]
    n_hy = 3 * hy
    proj = in_w.shape[2]
    n_exp = router_w.shape[2]

    cond = jnp.concatenate([jax.nn.silu(c), jax.nn.silu(c_ctx)[None]], axis=0)
    mod = _matmul(cond, ada_w[0], F32, cond.shape[0], MM_TILE) + ada_b[0]
    mod_x = [m[:, None, :] for m in jnp.split(mod[:nb], 6, axis=-1)]
    mod_c = [m[:, None, :] for m in jnp.split(mod[nb:], 6, axis=-1)]

    lora_w = jnp.concatenate([rw_w1[0, 0], rw_w1[0, 1], rw_a1[0, 0], rw_a1[0, 1], rw_g1[0]], axis=1)
    n_lora = lora_w.shape[1]
    nz = -(-(proj + n_lora) // 512) * 512
    pad = nz - proj - n_lora
    w_all = jnp.concatenate([in_w[0], lora_w, jnp.zeros((d, pad), F32)], axis=1).astype(BF16)
    pass_taps = jnp.concatenate([jnp.zeros((1, n_lora + pad), F32), jnp.ones((1, n_lora + pad), F32),
                                 jnp.zeros((1, n_lora + pad), F32)], axis=0)
    cw_all = jnp.concatenate([conv_w[0], pass_taps], axis=1)
    cb_all = jnp.concatenate([conv_b[0], jnp.zeros((n_lora + pad,), F32)])[None]
    g1 = norm1_g[0][None]
    zx = _inproj(x, mod_x[0], mod_x[1], g1, w_all, cw_all, cb_all, *PROJ_TILE)
    zc = _inproj(ctx, mod_c[0], mod_c[1], g1, w_all[:, n_hy:], cw_all[:, n_hy:], cb_all[:, n_hy:], *CTX_PROJ_TILE)

    h_fwd, h_bwd = _hyena_filter(seq, hy_w1[0], hy_b1[0], hy_w2[0], hy_b2[0], hy_w3[0], hy_b3[0],
                                 hy_freq[0], hy_w4[0])
    fwd, inv = _dft_matrices(seq)
    taps = _taps_spectrum(fwd, h_fwd, h_bwd)
    spec = _dft_fwd(fwd, zx, hy, *DFT_TILE)
    y_hy = _dft_inv(inv, spec, taps, zx, hy_bias[0][None], hy_norm_g[0][None], *DFT_TILE)

    scan_args = (rw_w2[0].astype(BF16), rw_a2[0].astype(BF16), rw_w0[0], rw_a0[0],
                 rw_kk[0][None], rw_ka[0][None], rw_rk[0].reshape(1, rw))
    s0 = jnp.zeros((2, nb, rw // PACK_W, PACK_W, PACK_W), F32)
    _, _, s_ctx = _rwkv_scan(zc, 0, proj - n_hy, *scan_args, s0)
    ys, bonus, _ = _rwkv_scan(zx, n_hy, proj, *scan_args, s_ctx)

    ne = -(-n_exp // LANE) * LANE
    rw_pad = jnp.pad(router_w[0], ((0, 0), (0, ne - n_exp)))
    rw_hi = rw_pad.astype(BF16)
    rw_lo = (rw_pad - rw_hi.astype(F32)).astype(BF16)
    rb = jnp.pad(router_b[0], (0, ne - n_exp))[None]
    x1, h2, logits = _outproj(y_hy, ys, bonus, zx, proj + n_lora - rw_g1.shape[-1], rw_g2[0].astype(BF16),
                              rw_lnx_g[0][None], rw_lnx_b[0][None], out_w[0].astype(BF16), x,
                              mod_x[2], norm2_g[0][None], mod_x[3], mod_x[4], rw_hi, rw_lo, rb, OUTPROJ_ROWS)

    ys, dest_km, gates = _moe(h2, logits, n_exp, ex_w_gate[0], ex_b_gate[0], ex_w_up[0], ex_b_up[0],
                              ex_w_down[0], ex_b_down[0])
    n_tok = nb * seq
    tm = _tile(seq, COMBINE_ROWS)
    n_parts = next(p for p in (COMBINE_PARTS, 2, 1) if (n_tok // tm) % p == 0)
    tq = n_tok // n_parts
    out = None
    for q in range(n_parts):
        picked = ys[dest_km[:, q * tq:(q + 1) * tq]]
        out = _combine(picked, gates, x1, mod_x[5], final_g[None], seq, tm, q * (tq // tm), out)
    return out.reshape(nb, seq, d)
```

```python
import functools
import math

import jax
import jax.numpy as jnp
from jax import lax
from jax.experimental import pallas as pl
from jax.experimental.pallas import tpu as pltpu

F32 = jnp.float32
BF16 = jnp.bfloat16

HEAD = 64
CHUNK = 64
PACK = 4
PACK_W = PACK * HEAD
SCAN_BATCH = 4
HALO = 16
INPROJ_ROWS = 256
SUB_ROWS = 256
FILTER_BANDS = 16
FILTER_TARGET = 1e-2
FAST_DECAY_PCT = 0.3
SLOW_DECAY_PCT = 1.5
TOP_K = 4
SWIGLU_LIMIT = 7.0
SWIGLU_ALPHA = 1.702
NORM_EPS = 1e-6
LNX_EPS = 64e-5
MOE_TILE = 512
ROUTE_TILE = 512
MOE_PARTS = 4
COMBINE_PARTS = 4
LANE = 128
PROJ_TILE = (1024, 512)
CTX_PROJ_TILE = (256, 1792)
DFT_TILE = (1024, 512)
OUTPROJ_ROWS = 512
COMBINE_ROWS = 256
FILTER_ROWS = 512
MOE_COLS = 1024
MM_TILE = 1024
VMEM_LIMIT = 56 * 1024 * 1024


def _cparams(sem):
    return pltpu.CompilerParams(dimension_semantics=sem, vmem_limit_bytes=VMEM_LIMIT)


def _tile(n, want):
    if n <= want:
        return n
    t = want
    while n % t:
        t //= 2
    assert t >= 8, (n, want)
    return t


def _dot(a, b):
    return jnp.dot(a.astype(BF16), b.astype(BF16), preferred_element_type=F32)


def _group_ones(width):
    r = lax.broadcasted_iota(jnp.int32, (width, width), 0) // HEAD
    c = lax.broadcasted_iota(jnp.int32, (width, width), 1) // HEAD
    return jnp.where(r == c, 1.0, 0.0).astype(BF16)


def _group_sum(x, ones):
    width = ones.shape[0]
    parts = [_dot(x[:, o:o + width], ones) for o in range(0, x.shape[1], width)]
    return parts[0] if len(parts) == 1 else jnp.concatenate(parts, axis=1)


def _mm_kernel(a_ref, b_ref, o_ref):
    o_ref[...] = _dot(a_ref[...], b_ref[...]).astype(o_ref.dtype)


def _matmul(a, b, out_dtype, tm, tn):
    m, k = a.shape
    n = b.shape[1]
    tm, tn = _tile(m, tm), _tile(n, tn)
    return pl.pallas_call(
        _mm_kernel,
        grid=(n // tn, m // tm),
        in_specs=[pl.BlockSpec((tm, k), lambda j, i: (i, 0)),
                  pl.BlockSpec((k, tn), lambda j, i: (0, j))],
        out_specs=pl.BlockSpec((tm, tn), lambda j, i: (i, j)),
        out_shape=jax.ShapeDtypeStruct((m, n), out_dtype),
        compiler_params=_cparams(("parallel", "parallel")),
    )(a, b)


def _bmm_shared_lhs(a, b, out_dtype, tm, tn):
    m, k = a.shape
    nb, _, n = b.shape
    tm, tn = _tile(m, tm), _tile(n, tn)
    return pl.pallas_call(
        _mm_kernel,
        grid=(m // tm, nb, n // tn),
        in_specs=[pl.BlockSpec((tm, k), lambda i, bb, j: (i, 0)),
                  pl.BlockSpec((None, k, tn), lambda i, bb, j: (bb, 0, j))],
        out_specs=pl.BlockSpec((None, tm, tn), lambda i, bb, j: (bb, i, j)),
        out_shape=jax.ShapeDtypeStruct((nb, m, n), out_dtype),
        compiler_params=_cparams(("parallel", "parallel", "parallel")),
    )(a, b)


def _inproj_kernel(seq, xm_ref, xp_ref, xn_ref, shift_ref, scale_ref, g_ref, w_ref, cw_ref, cb_ref,
                   z_ref, h_ref, zz_ref):
    tm = xm_ref.shape[0]
    i = pl.program_id(0)

    @pl.when(pl.program_id(1) == 0)
    def _():
        def norm(xv):
            y = xv * lax.rsqrt(jnp.mean(xv * xv, axis=-1, keepdims=True) + NORM_EPS) * g_ref[...]
            return y * (1.0 + scale_ref[...]) + shift_ref[...]
        keep_prev = jnp.where((i * tm) % seq == 0, 0.0, 1.0)
        keep_next = jnp.where(((i + 1) * tm) % seq == 0, 0.0, 1.0)
        h_ref[0:HALO] = (norm(xp_ref[...]) * keep_prev).astype(BF16)
        h_ref[HALO:HALO + tm] = norm(xm_ref[...]).astype(BF16)
        h_ref[HALO + tm:] = (norm(xn_ref[...]) * keep_next).astype(BF16)

    cw = cw_ref[...]
    n_chunks = max(1, tm // INPROJ_ROWS)
    q = tm // n_chunks
    d0 = o0 = 0
    for ci in range(n_chunks):
        last = ci + 1 == n_chunks
        d1 = tm + 2 * HALO if last else (ci + 1) * q + HALO
        o1 = tm if last else d1 - 2 * HALO
        zz_ref[d0:d1] = jnp.dot(h_ref[d0:d1], w_ref[...], preferred_element_type=F32)
        z_ref[o0:o1] = (zz_ref[HALO - 1 + o0:HALO - 1 + o1] * cw[0:1] + zz_ref[HALO + o0:HALO + o1] * cw[1:2]
                        + zz_ref[HALO + 1 + o0:HALO + 1 + o1] * cw[2:3] + cb_ref[...]).astype(z_ref.dtype)
        d0, o0 = d1, o1


def _inproj(x, shift, scale, g, w, cw, cb, tm, tn):
    nb, seq, d = x.shape
    n = w.shape[1]
    tm, tn = _tile(seq, tm), _tile(n, tn)
    m = nb * seq
    n_halo = m // HALO
    per_batch = shift.shape[0] > 1
    mod_spec = pl.BlockSpec((None, 1, d), (lambda i, j: ((i * tm) // seq, 0, 0)) if per_batch
                            else (lambda i, j: (0, 0, 0)))
    x2 = x.reshape(m, d)
    out = pl.pallas_call(
        functools.partial(_inproj_kernel, seq),
        grid=(m // tm, n // tn),
        in_specs=[pl.BlockSpec((tm, d), lambda i, j: (i, 0)),
                  pl.BlockSpec((HALO, d), lambda i, j: (jnp.maximum(i * (tm // HALO) - 1, 0), 0)),
                  pl.BlockSpec((HALO, d), lambda i, j: (jnp.minimum((i + 1) * (tm // HALO), n_halo - 1), 0)),
                  mod_spec, mod_spec,
                  pl.BlockSpec((1, d), lambda i, j: (0, 0)),
                  pl.BlockSpec((d, tn), lambda i, j: (0, j)),
                  pl.BlockSpec((3, tn), lambda i, j: (0, j)),
                  pl.BlockSpec((1, tn), lambda i, j: (0, j))],
        out_specs=pl.BlockSpec((tm, tn), lambda i, j: (i, j)),
        out_shape=jax.ShapeDtypeStruct((m, n), BF16),
        scratch_shapes=[pltpu.VMEM((tm + 2 * HALO, d), BF16), pltpu.VMEM((tm + 2 * HALO, tn), F32)],
        compiler_params=_cparams(("parallel", "arbitrary")),
    )(x2, x2, x2, shift, scale, g, w, cw, cb)
    return out.reshape(nb, seq, n)


def _dft_matrices(length):
    n = 2 * length
    f = lax.broadcasted_iota(jnp.int32, (length, length), 0)
    t = lax.broadcasted_iota(jnp.int32, (length, length), 1)
    ang = ((f * t) % n).astype(F32) * (2.0 * math.pi / n)
    cos, sin = jnp.cos(ang), jnp.sin(ang)
    nyq = jnp.where(t % 2 == 0, 1.0, -1.0)
    fwd = jnp.concatenate([cos, jnp.where(f == 0, nyq, -sin)], axis=0)
    f_t = f.T
    inv = jnp.concatenate([jnp.where(f_t == 0, 1.0, 2.0 * cos.T),
                           jnp.where(f_t == 0, nyq.T, -2.0 * sin.T)], axis=1)
    return fwd.astype(BF16), inv.astype(BF16)


def _dft_fwd_kernel(f_ref, x1_ref, v_ref, o_ref, u_ref):
    @pl.when(pl.program_id(2) == 0)
    def _():
        u_ref[...] = (x1_ref[...].astype(F32) * v_ref[...].astype(F32)).astype(BF16)

    o_ref[...] = jnp.dot(f_ref[...], u_ref[...], preferred_element_type=F32).astype(o_ref.dtype)


def _dft_fwd(fwd, z, hy, tm, tn):
    nb, seq, _ = z.shape
    tm, tn = _tile(2 * seq, tm), _tile(hy, tn)
    nj = hy // tn
    return pl.pallas_call(
        _dft_fwd_kernel,
        grid=(nb, nj, 2 * seq // tm),
        in_specs=[pl.BlockSpec((tm, seq), lambda b, j, i: (i, 0)),
                  pl.BlockSpec((None, seq, tn), lambda b, j, i: (b, 0, nj + j)),
                  pl.BlockSpec((None, seq, tn), lambda b, j, i: (b, 0, 2 * nj + j))],
        out_specs=pl.BlockSpec((None, tm, tn), lambda b, j, i: (b, i, j)),
        out_shape=jax.ShapeDtypeStruct((nb, 2 * seq, hy), BF16),
        scratch_shapes=[pltpu.VMEM((seq, tn), BF16)],
        compiler_params=_cparams(("parallel", "parallel", "arbitrary")),
    )(fwd, z, z)


def _dft_inv_kernel(g_ref, s_ref, t_ref, x0_ref, x1_ref, v_ref, bias_ref, ng_ref, o_ref, y_ref):
    half = s_ref.shape[0] // 2

    @pl.when(pl.program_id(2) == 0)
    def _():
        re = s_ref[:half].astype(F32)
        im = s_ref[half:].astype(F32)
        t_re = t_ref[:half].astype(F32)
        t_im = t_ref[half:].astype(F32)
        row0 = lax.broadcasted_iota(jnp.int32, (half, 1), 0) == 0
        y_ref[:half] = (re * t_re - jnp.where(row0, 0.0, im * t_im)).astype(BF16)
        y_ref[half:] = (im * jnp.where(row0, t_im, t_re) + jnp.where(row0, 0.0, re * t_im)).astype(BF16)

    conv = jnp.dot(g_ref[...], y_ref[...], preferred_element_type=F32)
    u = x1_ref[...].astype(F32) * v_ref[...].astype(F32)
    y = x0_ref[...].astype(F32) * (conv + bias_ref[...] * u)
    ms = _group_sum(y * y, _group_ones(min(y.shape[1], PACK_W))) * (1.0 / HEAD)
    o_ref[...] = (y * lax.rsqrt(ms + NORM_EPS) * ng_ref[...]).astype(o_ref.dtype)


def _dft_inv(inv, spec, taps, z, bias, norm_g, tm, tn):
    nb, seq, _ = z.shape
    hy = spec.shape[2]
    tm, tn = _tile(seq, tm), _tile(hy, tn)
    nj = hy // tn
    row = lambda k: pl.BlockSpec((None, tm, tn), lambda b, j, i: (b, i, k * nj + j))
    vec = pl.BlockSpec((1, tn), lambda b, j, i: (0, j))
    return pl.pallas_call(
        _dft_inv_kernel,
        grid=(nb, nj, seq // tm),
        in_specs=[pl.BlockSpec((tm, 2 * seq), lambda b, j, i: (i, 0)),
                  pl.BlockSpec((None, 2 * seq, tn), lambda b, j, i: (b, 0, j)),
                  pl.BlockSpec((2 * seq, tn), lambda b, j, i: (0, j)),
                  row(0), row(1), row(2), vec, vec],
        out_specs=pl.BlockSpec((None, tm, tn), lambda b, j, i: (b, i, j)),
        out_shape=jax.ShapeDtypeStruct((nb, seq, hy), BF16),
        scratch_shapes=[pltpu.VMEM((2 * seq, tn), BF16)],
        compiler_params=_cparams(("parallel", "parallel", "arbitrary")),
    )(inv, spec, taps, z, z, z, bias, norm_g)


def _dot3(a, b):
    a_hi = a.astype(BF16)
    b_hi = b.astype(BF16)
    a_lo = (a - a_hi.astype(F32)).astype(BF16)
    b_lo = (b - b_hi.astype(F32)).astype(BF16)
    dot = lambda p, q: jnp.dot(p, q, preferred_element_type=F32)
    return dot(a_hi, b_hi) + dot(a_hi, b_lo) + dot(a_lo, b_hi)


def _filter_kernel(f_ref, t_ref, w1_ref, b1_ref, w2_ref, b2_ref, w3_ref, b3_ref, fr_ref, w4_ref, dl_ref, o_ref):
    freq = fr_ref[...]
    h = jnp.sin(freq * (_dot3(f_ref[...], w1_ref[...]) + b1_ref[...]))
    h = jnp.sin(freq * (_dot3(h, w2_ref[...]) + b2_ref[...]))
    h = jnp.sin(freq * (_dot3(h, w3_ref[...]) + b3_ref[...]))
    o_ref[...] = _dot3(h, w4_ref[...]) * jnp.exp(-t_ref[...] * dl_ref[...])


def _hyena_filter(length, w1, b1, w2, b2, w3, b3, freq, w4):
    hy = w4.shape[1] // 2
    width = w2.shape[0]
    t = jnp.linspace(0.0, 1.0, length, dtype=F32)[:, None]
    ang = (2.0 * math.pi / length) * jnp.arange(length, dtype=F32)[:, None]
    bands = jnp.linspace(1e-4, FILTER_BANDS - 1, FILTER_BANDS, dtype=F32)[None, :]
    feats = jnp.concatenate([t, jnp.cos(bands * ang), -jnp.sin(bands * ang)], axis=-1)
    emb = feats.shape[1]
    emb_pad = -(-emb // LANE) * LANE
    feats = jnp.pad(feats, ((0, 0), (0, emb_pad - emb)))
    w1p = jnp.pad(w1, ((0, emb_pad - emb), (0, 0)))
    deltas = jnp.abs(jnp.linspace(math.log(FILTER_TARGET) / SLOW_DECAY_PCT,
                                  math.log(FILTER_TARGET) / FAST_DECAY_PCT, hy, dtype=F32))
    tm = _tile(length, FILTER_ROWS)
    full = lambda r, c: pl.BlockSpec((r, c), lambda i: (0, 0))
    h = pl.pallas_call(
        _filter_kernel,
        grid=(length // tm,),
        in_specs=[pl.BlockSpec((tm, emb_pad), lambda i: (i, 0)), pl.BlockSpec((tm, 1), lambda i: (i, 0)),
                  full(emb_pad, width), full(1, width), full(width, width), full(1, width),
                  full(width, width), full(1, width), full(1, width), full(width, 2 * hy), full(1, 2 * hy)],
        out_specs=pl.BlockSpec((tm, 2 * hy), lambda i: (i, 0)),
        out_shape=jax.ShapeDtypeStruct((length, 2 * hy), F32),
        compiler_params=_cparams(("parallel",)),
    )(feats, t, w1p, b1[None], w2, b2[None], w3, b3[None], freq[None], w4, jnp.tile(deltas, 2)[None])
    return h[:, :hy], h[:, hy:]


def _taps_spectrum(fwd, h_fwd, h_bwd):
    length = h_fwd.shape[0]
    sig = jnp.stack([h_fwd, h_bwd.at[0].set(0.0)]).astype(BF16)
    spec = _bmm_shared_lhs(fwd, sig, F32, MM_TILE, MM_TILE)
    re, im = spec[:, :length], spec[:, length:]
    scale = 1.0 / (2 * length)
    row0 = lax.broadcasted_iota(jnp.int32, (length, 1), 0) == 0
    t_re = (re[0] + re[1]) * scale
    t_im = jnp.where(row0, im[0] + im[1], im[0] - im[1]) * scale
    return jnp.concatenate([t_re, t_im], axis=0).astype(BF16)


def _scan_kernel(r_ref, k_ref, v_ref, lw_ref, la_ref, w2_ref, a2_ref, w0_ref, a0_ref, kks_ref, ka_ref, rk_ref,
                 s0_ref, y_ref, bonus_ref, sfin_ref, ht_ref):
    c = CHUNK
    w = PACK_W
    direction = pl.program_id(0)
    step = pl.program_id(2)
    n_batch = r_ref.shape[0]
    n_groups = r_ref.shape[2] // w
    lora = w2_ref.shape[0]

    @pl.when(step == 0)
    def _():
        ht_ref[...] = s0_ref[...]

    sign = 1 - 2 * direction
    d_sq = (lax.broadcasted_iota(jnp.int32, (c, c), 1) - lax.broadcasted_iota(jnp.int32, (c, c), 0)) * sign
    incl_sq = jnp.where(d_sq <= 0, 1.0, 0.0).astype(BF16)
    d_c = (lax.broadcasted_iota(jnp.int32, (c, w), 1) % c - lax.broadcasted_iota(jnp.int32, (c, w), 0)) * sign
    strict_c = d_c < 0
    incl_c = d_c <= 0
    eye_c = jnp.where(d_c == 0, 1.0, 0.0)
    bd_mask = (lax.broadcasted_iota(jnp.int32, (w, w), 0) // c) == (lax.broadcasted_iota(jnp.int32, (w, w), 1) // HEAD)
    ones_bd = jnp.where(bd_mask, 1.0, 0.0).astype(BF16)

    def bd(x):
        xb = x.astype(BF16)
        return jnp.where(bd_mask, jnp.concatenate([xb] * PACK, axis=0), jnp.zeros((), BF16))

    fwd = direction == 0
    incl_2 = jnp.concatenate([incl_sq, incl_sq], axis=1)
    prep = []
    for bi in range(n_batch):
        r_all = r_ref[bi].astype(F32)
        k_all = k_ref[bi].astype(F32)
        v_all = v_ref[bi].astype(F32)
        lw_pre = lw_ref[bi].astype(F32)
        la_pre = la_ref[bi].astype(F32)
        lw_pre = jnp.where(fwd, lw_pre[:, :lora], lw_pre[:, lora:])
        la_pre = jnp.where(fwd, la_pre[:, :lora], la_pre[:, lora:])
        x_w = w0_ref[...] + _dot(jnp.tanh(lw_pre), w2_ref[...])
        lw_all = -math.exp(-0.5) * jax.nn.sigmoid(x_w)
        a_sig = jax.nn.sigmoid(a0_ref[...] + _dot(la_pre, a2_ref[...]))
        kk = k_all * kks_ref[...]
        kk = kk * lax.rsqrt(jnp.maximum(_group_sum(kk * kk, ones_bd), 1e-24))
        kd_all = k_all * (1.0 + (a_sig - 1.0) * ka_ref[...])
        bb_all = kk * a_sig
        bonus_ref[bi] = (_group_sum(r_all * kd_all * rk_ref[...], ones_bd) * v_all).astype(bonus_ref.dtype)

        lw_hi = lw_all.astype(BF16)
        lw_lo = (lw_all - lw_hi.astype(F32)).astype(BF16)
        cum = jnp.dot(incl_2, jnp.concatenate([lw_hi, lw_lo], axis=0), preferred_element_type=F32)
        tot = jnp.sum(lw_all, axis=0, keepdims=True)
        g_inv = jnp.exp(-cum)
        g_rem = jnp.exp(tot - cum)
        prep.append(dict(v=v_all, g_tot=jnp.exp(tot), a_t=-kk * jnp.exp(cum - lw_all), r_t=r_all * jnp.exp(cum),
                         b_t=bb_all * g_inv, k_t=kd_all * g_inv, b_s=bb_all * g_rem, k_s=kd_all * g_rem))

    probs = [(bi, g) for bi in range(n_batch) for g in range(n_groups)]
    idx = range(len(probs))
    cols = [slice(g * w, (g + 1) * w) for _, g in probs]
    take = lambda name: [prep[bi][name][:, cols[q]] for q, (bi, _) in enumerate(probs)]
    v_g, a_t, r_t, b_t, k_t, b_s, k_s = (take(nm) for nm in ("v", "a_t", "r_t", "b_t", "k_t", "b_s", "k_s"))
    hts = [ht_ref[bi, g] for bi, g in probs]
    lhs = [jnp.concatenate([a_t[q], r_t[q]], axis=0) for q in idx]
    scores = [_dot(lhs[q], jnp.concatenate([bd(b_t[q]).T, bd(k_t[q]).T], axis=1)) for q in idx]
    h0 = [_dot(lhs[q], hts[q].T) for q in idx]
    a_ab = [jnp.where(strict_c, s[:c, :w], 0.0) for s in scores]
    a_ak = [jnp.where(strict_c, s[:c, w:], 0.0) for s in scores]
    a_rb = [jnp.where(incl_c, s[c:, :w], 0.0) for s in scores]
    a_rk = [jnp.where(incl_c, s[c:, w:], 0.0) for s in scores]
    bd_v = [bd(vq) for vq in v_g]
    av = [_dot(jnp.concatenate([a_ak[q], a_rk[q]], axis=0), bd_v[q]) for q in idx]
    x = [h0[q][:c] + av[q][:c] for q in idx]

    t_inv = [eye_c + a for a in a_ab]
    p = [_dot(a, bd(a)) for a in a_ab]
    n_sq = int(math.log2(c)) - 1
    for it in range(n_sq):
        bd_p = [bd(pq) for pq in p]
        if it + 1 < n_sq:
            pt = [_dot(jnp.concatenate([p[q], t_inv[q]], axis=0), bd_p[q]) for q in idx]
            p = [m[:c] for m in pt]
            t_inv = [t_inv[q] + pt[q][c:] for q in idx]
        else:
            t_inv = [t_inv[q] + _dot(t_inv[q], bd_p[q]) for q in idx]
    u = [_dot(t_inv[q], bd(x[q])) for q in idx]

    y = [h0[q][c:] + av[q][c:] + _dot(a_rb[q], bd(u[q])) for q in idx]
    for q, (bi, _) in enumerate(probs):
        y_ref[bi, :, cols[q]] = y[q].astype(y_ref.dtype)

    for q, (bi, g) in enumerate(probs):
        uv = jnp.concatenate([u[q], v_g[q]], axis=0).astype(BF16)
        bk = jnp.concatenate([b_s[q], k_s[q]], axis=0).astype(BF16)
        upd = lax.dot_general(uv, bk, (((0,), (0,)), ((), ())), preferred_element_type=F32)
        ht_ref[bi, g] = hts[q] * prep[bi]["g_tot"][:, cols[q]] + jnp.where(bd_mask, upd, 0.0)

    @pl.when(step == pl.num_programs(2) - 1)
    def _():
        sfin_ref[...] = ht_ref[...]


def _rwkv_scan(z, col0, lora_col0, w2, a2, w0, a0, kks, ka, rk, s0):
    nb, seq, _ = z.shape
    lora, ch = w2.shape[1:]
    c = CHUNK
    assert CHUNK == HEAD and ch % PACK_W == 0 and seq % c == 0 and 2 * lora == LANE
    assert col0 % ch == 0 and lora_col0 % LANE == 0
    nc = seq // c
    ng = ch // PACK_W
    bs = SCAN_BATCH if nb % SCAN_BATCH == 0 else 1
    chunk_of = lambda n, i: jnp.where(n == 0, i, nc - 1 - i)
    zcol = lambda k: pl.BlockSpec((bs, c, ch), lambda n, b, i: (b, chunk_of(n, i), col0 // ch + k))
    zlora = lambda k: pl.BlockSpec((bs, c, LANE), lambda n, b, i: (b, chunk_of(n, i), lora_col0 // LANE + k))
    per_dir_w = pl.BlockSpec((None, lora, ch), lambda n, b, i: (n, 0, 0))
    per_dir_v = pl.BlockSpec((None, 1, ch), lambda n, b, i: (n, 0, 0))
    shared_v = pl.BlockSpec((1, ch), lambda n, b, i: (0, 0))
    state = pl.BlockSpec((None, bs, ng, PACK_W, PACK_W), lambda n, b, i: (n, b, 0, 0, 0))
    out = pl.BlockSpec((None, bs, c, ch), lambda n, b, i: (n, b, chunk_of(n, i), 0))
    return pl.pallas_call(
        _scan_kernel,
        grid=(2, nb // bs, nc),
        in_specs=[zcol(0), zcol(1), zcol(2), zlora(0), zlora(1), per_dir_w, per_dir_w, per_dir_v, per_dir_v,
                  shared_v, shared_v, shared_v, state],
        out_specs=[out, out, state],
        out_shape=[jax.ShapeDtypeStruct((2, nb, seq, ch), BF16), jax.ShapeDtypeStruct((2, nb, seq, ch), BF16),
                   jax.ShapeDtypeStruct((2, nb, ng, PACK_W, PACK_W), F32)],
        scratch_shapes=[pltpu.VMEM((bs, ng, PACK_W, PACK_W), F32)],
        compiler_params=_cparams(("parallel", "parallel", "arbitrary")),
    )(z, z, z, z, z, w2, a2, w0.reshape(2, 1, ch), a0.reshape(2, 1, ch), kks, ka, rk, s0)


def _outproj_kernel(yh_ref, ys0_ref, ys1_ref, b0_ref, b1_ref, lg_ref, g2_ref, lng_ref, lnb_ref, ow_ref,
                    x_ref, gate_ref, n2g_ref, shift_ref, scale_ref, rwh_ref, rwl_ref, rb_ref,
                    x1_ref, h2_ref, lg_out_ref, mix_ref):
    hy = yh_ref.shape[1]
    tm = x_ref.shape[0]
    ones = _group_ones(PACK_W)
    rows = _tile(tm, SUB_ROWS)
    subs = [slice(r0, r0 + rows) for r0 in range(0, tm, rows)]

    for rs in subs:
        y = ys0_ref[rs].astype(F32) + ys1_ref[rs].astype(F32)
        mu = _group_sum(y, ones) * (1.0 / HEAD)
        yc = y - mu
        var = _group_sum(yc * yc, ones) * (1.0 / HEAD)
        y = yc * lax.rsqrt(var + LNX_EPS) * lng_ref[...] + lnb_ref[...]
        gate = _dot(jax.nn.sigmoid(lg_ref[rs].astype(F32)), g2_ref[...])
        y = (y + b0_ref[rs].astype(F32) + b1_ref[rs].astype(F32)) * gate
        mix_ref[rs, :hy] = yh_ref[rs]
        mix_ref[rs, hy:] = y.astype(BF16)
    mixes = [jnp.dot(mix_ref[rs], ow_ref[...], preferred_element_type=F32) for rs in subs]
    for rs, mix in zip(subs, mixes):
        x1 = x_ref[rs] + gate_ref[...] * mix
        x1_ref[rs] = x1
        h2 = x1 * lax.rsqrt(jnp.mean(x1 * x1, axis=-1, keepdims=True) + NORM_EPS) * n2g_ref[...]
        h2 = h2 * (1.0 + scale_ref[...]) + shift_ref[...]
        h2_ref[rs] = h2.astype(BF16)
        h_hi = h2.astype(BF16)
        h_lo = (h2 - h_hi.astype(F32)).astype(BF16)
        logits = (jnp.dot(h_hi, rwh_ref[...], preferred_element_type=F32)
                  + jnp.dot(h_hi, rwl_ref[...], preferred_element_type=F32)
                  + jnp.dot(h_lo, rwh_ref[...], preferred_element_type=F32))
        lg_out_ref[rs] = logits + rb_ref[...]


def _outproj(y_hy, ys, bonus, z, lora_g_col, g2, lnx_g, lnx_b, out_w, x, gate, n2g, shift, scale,
             rw_hi, rw_lo, rb, tm):
    nb, seq, d = x.shape
    m = nb * seq
    hy = y_hy.shape[2]
    rw = ys.shape[3]
    tm = _tile(seq, tm)
    lg = g2.shape[0]
    ne = rw_hi.shape[1]
    assert lora_g_col % lg == 0
    ys2 = ys.reshape(2, m, rw)
    bn2 = bonus.reshape(2, m, rw)
    nz = z.shape[2]
    full = lambda r, c: pl.BlockSpec((r, c), lambda i: (0, 0))
    dir_spec = lambda n: pl.BlockSpec((None, tm, rw), lambda i: (n, i, 0))
    mod_spec = pl.BlockSpec((None, 1, d), lambda i: ((i * tm) // seq, 0, 0))
    return pl.pallas_call(
        _outproj_kernel,
        grid=(m // tm,),
        in_specs=[pl.BlockSpec((tm, hy), lambda i: (i, 0)), dir_spec(0), dir_spec(1), dir_spec(0), dir_spec(1),
                  pl.BlockSpec((tm, lg), lambda i: (i, lora_g_col // lg)),
                  full(lg, rw), full(1, rw), full(1, rw), full(hy + rw, d),
                  pl.BlockSpec((tm, d), lambda i: (i, 0)), mod_spec, full(1, d), mod_spec, mod_spec,
                  full(d, ne), full(d, ne), full(1, ne)],
        out_specs=[pl.BlockSpec((tm, d), lambda i: (i, 0)), pl.BlockSpec((tm, d), lambda i: (i, 0)),
                   pl.BlockSpec((tm, ne), lambda i: (i, 0))],
        out_shape=[jax.ShapeDtypeStruct((m, d), F32), jax.ShapeDtypeStruct((m, d), BF16),
                   jax.ShapeDtypeStruct((m, ne), F32)],
        scratch_shapes=[pltpu.VMEM((tm, hy + rw), BF16)],
        compiler_params=_cparams(("parallel",)),
    )(y_hy.reshape(m, hy), ys2, ys2, bn2, bn2, z.reshape(m, nz), g2, lnx_g, lnx_b, out_w,
      x.reshape(m, d), gate, n2g, shift, scale, rw_hi, rw_lo, rb)


def _moe_up_kernel(te_ref, nu_ref, x_ref, wg_ref, wu_ref, bg_ref, bu_ref, h_ref):
    i = pl.program_id(1)

    @pl.when(i < nu_ref[0])
    def _():
        x = x_ref[...]
        g = jnp.dot(x, wg_ref[...].astype(BF16), preferred_element_type=F32) + bg_ref[...]
        u = jnp.dot(x, wu_ref[...].astype(BF16), preferred_element_type=F32) + bu_ref[...]
        g = jnp.minimum(g, SWIGLU_LIMIT)
        u = jnp.clip(u, -SWIGLU_LIMIT, SWIGLU_LIMIT)
        h_ref[...] = ((u + 1.0) * (g * jax.nn.sigmoid(SWIGLU_ALPHA * g))).astype(h_ref.dtype)

    @pl.when(i >= nu_ref[0])
    def _():
        h_ref[...] = jnp.zeros_like(h_ref)


def _moe_down_kernel(te_ref, nu_ref, h_ref, wd_ref, bd_ref, *rest):
    y_ref = rest[-1]
    i = pl.program_id(1)

    @pl.when(i < nu_ref[0])
    def _():
        y = jnp.dot(h_ref[...], wd_ref[...].astype(BF16), preferred_element_type=F32) + bd_ref[...]
        y_ref[...] = y.astype(y_ref.dtype)

    @pl.when(i >= nu_ref[0])
    def _():
        y_ref[...] = jnp.zeros_like(y_ref)


def _moe_up(tile_expert, n_used, xs, wg, wu, bg, bu, tn):
    n_rows, d = xs.shape
    e, _, f = wg.shape
    tm = MOE_TILE
    tn = _tile(f, tn)
    w_spec = pl.BlockSpec((None, d, tn), lambda j, i, te, nu: (te[i], 0, j))
    b_spec = pl.BlockSpec((None, 1, tn), lambda j, i, te, nu: (te[i], 0, j))
    return pl.pallas_call(
        _moe_up_kernel,
        grid_spec=pltpu.PrefetchScalarGridSpec(
            num_scalar_prefetch=2,
            grid=(f // tn, n_rows // tm),
            in_specs=[pl.BlockSpec((tm, d), lambda j, i, te, nu: (i, 0)), w_spec, w_spec, b_spec, b_spec],
            out_specs=pl.BlockSpec((tm, tn), lambda j, i, te, nu: (i, j))),
        out_shape=jax.ShapeDtypeStruct((n_rows, f), BF16),
        compiler_params=_cparams(("arbitrary", "arbitrary")),
    )(tile_expert, n_used, xs, wg, wu, bg.reshape(e, 1, f), bu.reshape(e, 1, f))


def _moe_down(tile_expert, n_used, hs, wd, bdn, tn, total_rows, tile0, ys_prev):
    n_rows, f = hs.shape
    e, _, d = wd.shape
    tm = MOE_TILE
    tn = _tile(d, tn)
    in_specs = [pl.BlockSpec((tm, f), lambda j, i, te, nu: (i, 0)),
                pl.BlockSpec((None, f, tn), lambda j, i, te, nu: (te[i], 0, j)),
                pl.BlockSpec((None, 1, tn), lambda j, i, te, nu: (te[i], 0, j))]
    args = [tile_expert, n_used, hs, wd, bdn.reshape(e, 1, d)]
    aliases = {}
    if ys_prev is not None:
        in_specs.append(pl.BlockSpec(memory_space=pl.ANY))
        args.append(ys_prev)
        aliases = {len(args) - 1: 0}
    return pl.pallas_call(
        _moe_down_kernel,
        grid_spec=pltpu.PrefetchScalarGridSpec(
            num_scalar_prefetch=2,
            grid=(d // tn, n_rows // tm),
            in_specs=in_specs,
            out_specs=pl.BlockSpec((tm, tn), lambda j, i, te, nu: (tile0 + i, j))),
        out_shape=jax.ShapeDtypeStruct((total_rows, d), BF16),
        input_output_aliases=aliases,
        compiler_params=_cparams(("arbitrary", "arbitrary")),
    )(*args)


def _route_kernel(n_exp, lg_ref, idx_ref, gate_ref, rank_ref, cnt_ref, carry_ref):
    tm, lanes = lg_ref.shape

    @pl.when(pl.program_id(0) == 0)
    def _():
        carry_ref[...] = jnp.zeros_like(carry_ref)

    lane = lax.broadcasted_iota(jnp.int32, (tm, lanes), 1).astype(F32)
    lg = jnp.where(lane < n_exp, lg_ref[...], -jnp.inf)
    tops, hots, idxs = [], [], []
    for _ in range(TOP_K):
        m = jnp.max(lg, axis=-1, keepdims=True)
        idx = jnp.min(jnp.where(lg == m, lane, float(lanes)), axis=-1, keepdims=True)
        hot = lane == idx
        lg = jnp.where(hot, -jnp.inf, lg)
        tops.append(m)
        hots.append(hot)
        idxs.append(idx)
    exps = [jnp.exp(t - tops[0]) for t in tops]
    inv = 1.0 / sum(exps)
    occ = sum(jnp.where(h, 1.0, 0.0) for h in hots)
    earlier = jnp.where(lax.broadcasted_iota(jnp.int32, (tm, tm), 0) > lax.broadcasted_iota(jnp.int32, (tm, tm), 1),
                        1.0, 0.0).astype(BF16)
    before = jnp.dot(earlier, occ.astype(BF16), preferred_element_type=F32) + carry_ref[...]
    ranks = [jnp.sum(jnp.where(h, before, 0.0), axis=-1, keepdims=True) for h in hots]
    carry_ref[...] += jnp.sum(occ, axis=0, keepdims=True)
    cnt_ref[...] = carry_ref[...]
    spread = lambda cols: sum(jnp.where(lane == float(k), c, 0.0) for k, c in enumerate(cols))
    idx_ref[...] = spread(idxs)
    gate_ref[...] = spread([e * inv for e in exps])
    rank_ref[...] = spread(ranks)


def _route(logits, n_exp, tm):
    n_tok, lanes = logits.shape
    tm = _tile(n_tok, tm)
    row = pl.BlockSpec((tm, lanes), lambda i: (i, 0))
    out = jax.ShapeDtypeStruct((n_tok, lanes), F32)
    idx, gates, rank, cnt = pl.pallas_call(
        functools.partial(_route_kernel, n_exp),
        grid=(n_tok // tm,),
        in_specs=[row],
        out_specs=[row, row, row, pl.BlockSpec((1, lanes), lambda i: (0, 0))],
        out_shape=[out, out, out, jax.ShapeDtypeStruct((1, lanes), F32)],
        scratch_shapes=[pltpu.VMEM((1, lanes), F32)],
        compiler_params=_cparams(("arbitrary",)),
    )(logits)
    return (idx[:, :TOP_K].astype(jnp.int32), gates[:, :TOP_K], rank[:, :TOP_K].astype(jnp.int32),
            cnt[0, :n_exp].astype(jnp.int32))


def _moe(h2, logits, e, wg, bg, wu, bu, wd, bdn):
    n_tok, d = h2.shape
    tm = MOE_TILE
    top_idx, gates, rank, counts = _route(logits, e, ROUTE_TILE)
    n_asg = n_tok * TOP_K
    raw_start = jnp.cumsum(counts) - counts
    padded = (counts + tm - 1) // tm * tm
    pad_end = jnp.cumsum(padded)
    pad_start = pad_end - padded
    dest = (pad_start[top_idx] + rank).reshape(n_asg)
    tok = jnp.arange(n_asg, dtype=jnp.int32) // TOP_K
    _, order_tok = lax.sort((dest, tok), num_keys=1)
    n_tiles = -(-(n_asg + e * (tm - 1)) // tm)
    n_rows = n_tiles * tm
    tile_start = jnp.arange(n_tiles, dtype=jnp.int32) * tm
    tile_expert = jnp.minimum(jnp.sum((pad_end[None, :] <= tile_start[:, None]).astype(jnp.int32), axis=1), e - 1)
    n_used = pad_end[-1] // tm
    row = jnp.arange(n_rows, dtype=jnp.int32)
    row_e = jnp.repeat(tile_expert, tm)
    row_rank = row - pad_start[row_e]
    src = jnp.clip(raw_start[row_e] + row_rank, 0, n_asg - 1)
    row_tok = jnp.where(row_rank < counts[row_e], order_tok[src], 0)

    n_parts = next(p for p in (MOE_PARTS, 2, 1) if n_tiles % p == 0)
    tp = n_tiles // n_parts
    ys = None
    for q in range(n_parts):
        te_q = tile_expert[q * tp:(q + 1) * tp]
        nu_q = jnp.clip(n_used - q * tp, 0, tp).astype(jnp.int32).reshape(1)
        xs = h2[row_tok[q * tp * tm:(q + 1) * tp * tm]]
        hs = _moe_up(te_q, nu_q, xs, wg, wu, bg, bu, MOE_COLS)
        ys = _moe_down(te_q, nu_q, hs, wd, bdn, MOE_COLS, n_rows, q * tp, ys)
    return ys, dest.reshape(n_tok, TOP_K).T, gates


def _combine_kernel(y_ref, g_ref, x1_ref, gate_ref, fg_ref, *rest):
    o_ref = rest[-1]
    g = g_ref[...]
    acc = y_ref[0].astype(F32) * g[:, 0:1]
    for k in range(1, y_ref.shape[0]):
        acc = acc + y_ref[k].astype(F32) * g[:, k:k + 1]
    xo = x1_ref[...] + gate_ref[...] * acc
    o_ref[...] = xo * lax.rsqrt(jnp.mean(xo * xo, axis=-1, keepdims=True) + NORM_EPS) * fg_ref[...]


def _combine(picked, gates, x1, gate, final_g, seq, tm, tile0, out_prev):
    nk, m, d = picked.shape
    total = x1.shape[0]
    in_specs = [pl.BlockSpec((nk, tm, d), lambda i: (0, i, 0)),
                pl.BlockSpec((tm, nk), lambda i: (tile0 + i, 0)),
                pl.BlockSpec((tm, d), lambda i: (tile0 + i, 0)),
                pl.BlockSpec((None, 1, d), lambda i: (((tile0 + i) * tm) // seq, 0, 0)),
                pl.BlockSpec((1, d), lambda i: (0, 0))]
    args = [picked, gates, x1, gate, final_g]
    aliases = {}
    if out_prev is not None:
        in_specs.append(pl.BlockSpec(memory_space=pl.ANY))
        args.append(out_prev)
        aliases = {len(args) - 1: 0}
    return pl.pallas_call(
        _combine_kernel,
        grid=(m // tm,),
        in_specs=in_specs,
        out_specs=pl.BlockSpec((tm, d), lambda i: (tile0 + i, 0)),
        out_shape=jax.ShapeDtypeStruct((total, d), F32),
        input_output_aliases=aliases,
        compiler_params=_cparams(("parallel",)),
    )(*args)


def kernel(x, c, ctx, c_ctx, ada_w, ada_b, norm1_g, norm2_g, in_w, conv_w, conv_b, hy_w1, hy_b1, hy_w2, hy_b2, hy_w3, hy_b3, hy_freq, hy_w4, hy_bias, hy_norm_g, rw_w0, rw_w1, rw_w2, rw_a0, rw_a1, rw_a2, rw_kk, rw_ka, rw_rk, rw_g1, rw_g2, rw_lnx_g, rw_lnx_b, out_w, router_w, router_b, ex_w_gate, ex_b_gate, ex_w_up, ex_b_up, ex_w_down, ex_b_down, final_g):
    assert ada_w.shape[0] == 1, "single-layer block: context outputs never reach a latent token"
    nb, seq, d = x.shape
    hy = hy_bias.shape[1]
    rw = rw_kk.shape[1]
    n_hy = 3 * hy
    proj = in_w.shape[2]
    n_exp = router_w.shape[2]

    cond = jnp.concatenate([jax.nn.silu(c), jax.nn.silu(c_ctx)[None]], axis=0)
    mod = _matmul(cond, ada_w[0], F32, cond.shape[0], MM_TILE) + ada_b[0]
    mod_x = [m[:, None, :] for m in jnp.split(mod[:nb], 6, axis=-1)]
    mod_c = [m[:, None, :] for m in jnp.split(mod[nb:], 6, axis=-1)]

    lora_w = jnp.concatenate([rw_w1[0, 0], rw_w1[0, 1], rw_a1[0, 0], rw_a1[0, 1], rw_g1[0]], axis=1)
    n_lora = lora_w.shape[1]
    nz = -(-(proj + n_lora) // 512) * 512
    pad = nz - proj - n_lora
    w_all = jnp.concatenate([in_w[0], lora_w, jnp.zeros((d, pad), F32)], axis=1).astype(BF16)
    pass_taps = jnp.concatenate([jnp.zeros((1, n_lora + pad), F32), jnp.ones((1, n_lora + pad), F32),
                                 jnp.zeros((1, n_lora + pad), F32)], axis=0)
    cw_all = jnp.concatenate([conv_w[0], pass_taps], axis=1)
    cb_all = jnp.concatenate([conv_b[0], jnp.zeros((n_lora + pad,), F32)])[None]
    g1 = norm1_g[0][None]
    zx = _inproj(x, mod_x[0], mod_x[1], g1, w_all, cw_all, cb_all, *PROJ_TILE)
    zc = _inproj(ctx, mod_c[0], mod_c[1], g1, w_all[:, n_hy:], cw_all[:, n_hy:], cb_all[:, n_hy:], *CTX_PROJ_TILE)

    h_fwd, h_bwd = _hyena_filter(seq, hy_w1[0], hy_b1[0], hy_w2[0], hy_b2[0], hy_w3[0], hy_b3[0],
                                 hy_freq[0], hy_w4[0])
    fwd, inv = _dft_matrices(seq)
    taps = _taps_spectrum(fwd, h_fwd, h_bwd)
    spec = _dft_fwd(fwd, zx, hy, *DFT_TILE)
    y_hy = _dft_inv(inv, spec, taps, zx, hy_bias[0][None], hy_norm_g[0][None], *DFT_TILE)

    scan_args = (rw_w2[0].astype(BF16), rw_a2[0].astype(BF16), rw_w0[0], rw_a0[0],
                 rw_kk[0][None], rw_ka[0][None], rw_rk[0].reshape(1, rw))
    s0 = jnp.zeros((2, nb, rw // PACK_W, PACK_W, PACK_W), F32)
    _, _, s_ctx = _rwkv_scan(zc, 0, proj - n_hy, *scan_args, s0)
    ys, bonus, _ = _rwkv_scan(zx, n_hy, proj, *scan_args, s_ctx)

    ne = -(-n_exp // LANE) * LANE
    rw_pad = jnp.pad(router_w[0], ((0, 0), (0, ne - n_exp)))
    rw_hi = rw_pad.astype(BF16)
    rw_lo = (rw_pad - rw_hi.astype(F32)).astype(BF16)
    rb = jnp.pad(router_b[0], (0, ne - n_exp))[None]
    x1, h2, logits = _outproj(y_hy, ys, bonus, zx, proj + n_lora - rw_g1.shape[-1], rw_g2[0].astype(BF16),
                              rw_lnx_g[0][None], rw_lnx_b[0][None], out_w[0].astype(BF16), x,
                              mod_x[2], norm2_g[0][None], mod_x[3], mod_x[4], rw_hi, rw_lo, rb, OUTPROJ_ROWS)

    ys, dest_km, gates = _moe(h2, logits, n_exp, ex_w_gate[0], ex_b_gate[0], ex_w_up[0], ex_b_up[0],
                              ex_w_down[0], ex_b_down[0])
    n_tok = nb * seq
    tm = _tile(seq, COMBINE_ROWS)
    n_parts = next(p for p in (COMBINE_PARTS, 2, 1) if (n_tok // tm) % p == 0)
    tq = n_tok // n_parts
    out = None
    for q in range(n_parts):
        picked = ys[dest_km[:, q * tq:(q + 1) * tq]]
        out = _combine(picked, gates, x1, mod_x[5], final_g[None], seq, tm, q * (tq // tm), out)
    return out.reshape(nb, seq, d)
```

```python
import functools
import math

import jax
import jax.numpy as jnp
from jax import lax
from jax.experimental import pallas as pl
from jax.experimental.pallas import tpu as pltpu

F32 = jnp.float32
BF16 = jnp.bfloat16

HEAD = 64
CHUNK = 64
PACK = 4
PACK_W = PACK * HEAD
SCAN_BATCH = 4
HALO = 16
INPROJ_ROWS = 256
SUB_ROWS = 256
FILTER_BANDS = 16
FILTER_TARGET = 1e-2
FAST_DECAY_PCT = 0.3
SLOW_DECAY_PCT = 1.5
TOP_K = 4
SWIGLU_LIMIT = 7.0
SWIGLU_ALPHA = 1.702
NORM_EPS = 1e-6
LNX_EPS = 64e-5
MOE_TILE = 512
ROUTE_TILE = 1024
MOE_PARTS = 4
COMBINE_PARTS = 4
LANE = 128
PROJ_TILE = (1024, 512)
CTX_PROJ_TILE = (256, 1792)
DFT_FWD_TILE = (2048, 512)
DFT_TILE = (1024, 512)
OUTPROJ_ROWS = 512
COMBINE_ROWS = 256
FILTER_ROWS = 512
MOE_COLS = 1024
MM_TILE = 1024
VMEM_LIMIT = 56 * 1024 * 1024


def _cparams(sem):
    return pltpu.CompilerParams(dimension_semantics=sem, vmem_limit_bytes=VMEM_LIMIT)


def _tile(n, want):
    if n <= want:
        return n
    t = want
    while n % t:
        t //= 2
    assert t >= 8, (n, want)
    return t


def _dot(a, b):
    return jnp.dot(a.astype(BF16), b.astype(BF16), preferred_element_type=F32)


def _group_ones(width):
    r = lax.broadcasted_iota(jnp.int32, (width, width), 0) // HEAD
    c = lax.broadcasted_iota(jnp.int32, (width, width), 1) // HEAD
    return jnp.where(r == c, 1.0, 0.0).astype(BF16)


def _group_sum(x, ones):
    width = ones.shape[0]
    parts = [_dot(x[:, o:o + width], ones) for o in range(0, x.shape[1], width)]
    return parts[0] if len(parts) == 1 else jnp.concatenate(parts, axis=1)


def _mm_kernel(a_ref, b_ref, o_ref):
    o_ref[...] = _dot(a_ref[...], b_ref[...]).astype(o_ref.dtype)


def _matmul(a, b, out_dtype, tm, tn):
    m, k = a.shape
    n = b.shape[1]
    tm, tn = _tile(m, tm), _tile(n, tn)
    return pl.pallas_call(
        _mm_kernel,
        grid=(n // tn, m // tm),
        in_specs=[pl.BlockSpec((tm, k), lambda j, i: (i, 0)),
                  pl.BlockSpec((k, tn), lambda j, i: (0, j))],
        out_specs=pl.BlockSpec((tm, tn), lambda j, i: (i, j)),
        out_shape=jax.ShapeDtypeStruct((m, n), out_dtype),
        compiler_params=_cparams(("parallel", "parallel")),
    )(a, b)


def _bmm_shared_lhs(a, b, out_dtype, tm, tn):
    m, k = a.shape
    nb, _, n = b.shape
    tm, tn = _tile(m, tm), _tile(n, tn)
    return pl.pallas_call(
        _mm_kernel,
        grid=(m // tm, nb, n // tn),
        in_specs=[pl.BlockSpec((tm, k), lambda i, bb, j: (i, 0)),
                  pl.BlockSpec((None, k, tn), lambda i, bb, j: (bb, 0, j))],
        out_specs=pl.BlockSpec((None, tm, tn), lambda i, bb, j: (bb, i, j)),
        out_shape=jax.ShapeDtypeStruct((nb, m, n), out_dtype),
        compiler_params=_cparams(("parallel", "parallel", "parallel")),
    )(a, b)


def _inproj_kernel(seq, xm_ref, xp_ref, xn_ref, shift_ref, scale_ref, g_ref, w_ref, cw_ref, cb_ref,
                   z_ref, h_ref, zz_ref):
    tm = xm_ref.shape[0]
    i = pl.program_id(0)

    @pl.when(pl.program_id(1) == 0)
    def _():
        def norm(xv):
            y = xv * lax.rsqrt(jnp.mean(xv * xv, axis=-1, keepdims=True) + NORM_EPS) * g_ref[...]
            return y * (1.0 + scale_ref[...]) + shift_ref[...]
        keep_prev = jnp.where((i * tm) % seq == 0, 0.0, 1.0)
        keep_next = jnp.where(((i + 1) * tm) % seq == 0, 0.0, 1.0)
        h_ref[0:HALO] = (norm(xp_ref[...]) * keep_prev).astype(BF16)
        h_ref[HALO:HALO + tm] = norm(xm_ref[...]).astype(BF16)
        h_ref[HALO + tm:] = (norm(xn_ref[...]) * keep_next).astype(BF16)

    cw = cw_ref[...]
    n_chunks = max(1, tm // INPROJ_ROWS)
    q = tm // n_chunks
    d0 = o0 = 0
    for ci in range(n_chunks):
        last = ci + 1 == n_chunks
        d1 = tm + 2 * HALO if last else (ci + 1) * q + HALO
        o1 = tm if last else d1 - 2 * HALO
        zz_ref[d0:d1] = jnp.dot(h_ref[d0:d1], w_ref[...], preferred_element_type=F32)
        z_ref[o0:o1] = (zz_ref[HALO - 1 + o0:HALO - 1 + o1] * cw[0:1] + zz_ref[HALO + o0:HALO + o1] * cw[1:2]
                        + zz_ref[HALO + 1 + o0:HALO + 1 + o1] * cw[2:3] + cb_ref[...]).astype(z_ref.dtype)
        d0, o0 = d1, o1


def _inproj(x, shift, scale, g, w, cw, cb, tm, tn):
    nb, seq, d = x.shape
    n = w.shape[1]
    tm, tn = _tile(seq, tm), _tile(n, tn)
    m = nb * seq
    n_halo = m // HALO
    per_batch = shift.shape[0] > 1
    mod_spec = pl.BlockSpec((None, 1, d), (lambda i, j: ((i * tm) // seq, 0, 0)) if per_batch
                            else (lambda i, j: (0, 0, 0)))
    x2 = x.reshape(m, d)
    out = pl.pallas_call(
        functools.partial(_inproj_kernel, seq),
        grid=(m // tm, n // tn),
        in_specs=[pl.BlockSpec((tm, d), lambda i, j: (i, 0)),
                  pl.BlockSpec((HALO, d), lambda i, j: (jnp.maximum(i * (tm // HALO) - 1, 0), 0)),
                  pl.BlockSpec((HALO, d), lambda i, j: (jnp.minimum((i + 1) * (tm // HALO), n_halo - 1), 0)),
                  mod_spec, mod_spec,
                  pl.BlockSpec((1, d), lambda i, j: (0, 0)),
                  pl.BlockSpec((d, tn), lambda i, j: (0, j)),
                  pl.BlockSpec((3, tn), lambda i, j: (0, j)),
                  pl.BlockSpec((1, tn), lambda i, j: (0, j))],
        out_specs=pl.BlockSpec((tm, tn), lambda i, j: (i, j)),
        out_shape=jax.ShapeDtypeStruct((m, n), BF16),
        scratch_shapes=[pltpu.VMEM((tm + 2 * HALO, d), BF16), pltpu.VMEM((tm + 2 * HALO, tn), F32)],
        compiler_params=_cparams(("parallel", "arbitrary")),
    )(x2, x2, x2, shift, scale, g, w, cw, cb)
    return out.reshape(nb, seq, n)


def _dft_matrices(length):
    n = 2 * length
    f = lax.broadcasted_iota(jnp.int32, (length, length), 0)
    t = lax.broadcasted_iota(jnp.int32, (length, length), 1)
    ang = ((f * t) % n).astype(F32) * (2.0 * math.pi / n)
    cos, sin = jnp.cos(ang), jnp.sin(ang)
    nyq = jnp.where(t % 2 == 0, 1.0, -1.0)
    fwd = jnp.concatenate([cos, jnp.where(f == 0, nyq, -sin)], axis=0)
    f_t = f.T
    inv = jnp.concatenate([jnp.where(f_t == 0, 1.0, 2.0 * cos.T),
                           jnp.where(f_t == 0, nyq.T, -2.0 * sin.T)], axis=1)
    return fwd.astype(BF16), inv.astype(BF16)


def _dft_fwd_kernel(f_ref, x1_ref, v_ref, o_ref, u_ref):
    @pl.when(pl.program_id(2) == 0)
    def _():
        u_ref[...] = (x1_ref[...].astype(F32) * v_ref[...].astype(F32)).astype(BF16)

    o_ref[...] = jnp.dot(f_ref[...], u_ref[...], preferred_element_type=F32).astype(o_ref.dtype)


def _dft_fwd(fwd, z, hy, tm, tn):
    nb, seq, _ = z.shape
    tm, tn = _tile(2 * seq, tm), _tile(hy, tn)
    nj = hy // tn
    return pl.pallas_call(
        _dft_fwd_kernel,
        grid=(nb, nj, 2 * seq // tm),
        in_specs=[pl.BlockSpec((tm, seq), lambda b, j, i: (i, 0)),
                  pl.BlockSpec((None, seq, tn), lambda b, j, i: (b, 0, nj + j)),
                  pl.BlockSpec((None, seq, tn), lambda b, j, i: (b, 0, 2 * nj + j))],
        out_specs=pl.BlockSpec((None, tm, tn), lambda b, j, i: (b, i, j)),
        out_shape=jax.ShapeDtypeStruct((nb, 2 * seq, hy), BF16),
        scratch_shapes=[pltpu.VMEM((seq, tn), BF16)],
        compiler_params=_cparams(("parallel", "parallel", "arbitrary")),
    )(fwd, z, z)


def _dft_inv_kernel(g_ref, s_ref, t_ref, x0_ref, x1_ref, v_ref, bias_ref, ng_ref, o_ref, y_ref):
    half = s_ref.shape[0] // 2

    @pl.when(pl.program_id(2) == 0)
    def _():
        re = s_ref[:half].astype(F32)
        im = s_ref[half:].astype(F32)
        t_re = t_ref[:half].astype(F32)
        t_im = t_ref[half:].astype(F32)
        row0 = lax.broadcasted_iota(jnp.int32, (half, 1), 0) == 0
        y_ref[:half] = (re * t_re - jnp.where(row0, 0.0, im * t_im)).astype(BF16)
        y_ref[half:] = (im * jnp.where(row0, t_im, t_re) + jnp.where(row0, 0.0, re * t_im)).astype(BF16)

    conv = jnp.dot(g_ref[...], y_ref[...], preferred_element_type=F32)
    u = x1_ref[...].astype(F32) * v_ref[...].astype(F32)
    y = x0_ref[...].astype(F32) * (conv + bias_ref[...] * u)
    ms = _group_sum(y * y, _group_ones(min(y.shape[1], PACK_W))) * (1.0 / HEAD)
    o_ref[...] = (y * lax.rsqrt(ms + NORM_EPS) * ng_ref[...]).astype(o_ref.dtype)


def _dft_inv(inv, spec, taps, z, bias, norm_g, tm, tn):
    nb, seq, _ = z.shape
    hy = spec.shape[2]
    tm, tn = _tile(seq, tm), _tile(hy, tn)
    nj = hy // tn
    row = lambda k: pl.BlockSpec((None, tm, tn), lambda b, j, i: (b, i, k * nj + j))
    vec = pl.BlockSpec((1, tn), lambda b, j, i: (0, j))
    return pl.pallas_call(
        _dft_inv_kernel,
        grid=(nb, nj, seq // tm),
        in_specs=[pl.BlockSpec((tm, 2 * seq), lambda b, j, i: (i, 0)),
                  pl.BlockSpec((None, 2 * seq, tn), lambda b, j, i: (b, 0, j)),
                  pl.BlockSpec((2 * seq, tn), lambda b, j, i: (0, j)),
                  row(0), row(1), row(2), vec, vec],
        out_specs=pl.BlockSpec((None, tm, tn), lambda b, j, i: (b, i, j)),
        out_shape=jax.ShapeDtypeStruct((nb, seq, hy), BF16),
        scratch_shapes=[pltpu.VMEM((2 * seq, tn), BF16)],
        compiler_params=_cparams(("parallel", "parallel", "arbitrary")),
    )(inv, spec, taps, z, z, z, bias, norm_g)


def _dot3(a, b):
    a_hi = a.astype(BF16)
    b_hi = b.astype(BF16)
    a_lo = (a - a_hi.astype(F32)).astype(BF16)
    b_lo = (b - b_hi.astype(F32)).astype(BF16)
    dot = lambda p, q: jnp.dot(p, q, preferred_element_type=F32)
    return dot(a_hi, b_hi) + dot(a_hi, b_lo) + dot(a_lo, b_hi)


def _filter_kernel(f_ref, t_ref, w1_ref, b1_ref, w2_ref, b2_ref, w3_ref, b3_ref, fr_ref, w4_ref, dl_ref, o_ref):
    freq = fr_ref[...]
    h = jnp.sin(freq * (_dot3(f_ref[...], w1_ref[...]) + b1_ref[...]))
    h = jnp.sin(freq * (_dot3(h, w2_ref[...]) + b2_ref[...]))
    h = jnp.sin(freq * (_dot3(h, w3_ref[...]) + b3_ref[...]))
    o_ref[...] = _dot3(h, w4_ref[...]) * jnp.exp(-t_ref[...] * dl_ref[...])


def _hyena_filter(length, w1, b1, w2, b2, w3, b3, freq, w4):
    hy = w4.shape[1] // 2
    width = w2.shape[0]
    t = jnp.linspace(0.0, 1.0, length, dtype=F32)[:, None]
    ang = (2.0 * math.pi / length) * jnp.arange(length, dtype=F32)[:, None]
    bands = jnp.linspace(1e-4, FILTER_BANDS - 1, FILTER_BANDS, dtype=F32)[None, :]
    feats = jnp.concatenate([t, jnp.cos(bands * ang), -jnp.sin(bands * ang)], axis=-1)
    emb = feats.shape[1]
    emb_pad = -(-emb // LANE) * LANE
    feats = jnp.pad(feats, ((0, 0), (0, emb_pad - emb)))
    w1p = jnp.pad(w1, ((0, emb_pad - emb), (0, 0)))
    deltas = jnp.abs(jnp.linspace(math.log(FILTER_TARGET) / SLOW_DECAY_PCT,
                                  math.log(FILTER_TARGET) / FAST_DECAY_PCT, hy, dtype=F32))
    tm = _tile(length, FILTER_ROWS)
    full = lambda r, c: pl.BlockSpec((r, c), lambda i: (0, 0))
    h = pl.pallas_call(
        _filter_kernel,
        grid=(length // tm,),
        in_specs=[pl.BlockSpec((tm, emb_pad), lambda i: (i, 0)), pl.BlockSpec((tm, 1), lambda i: (i, 0)),
                  full(emb_pad, width), full(1, width), full(width, width), full(1, width),
                  full(width, width), full(1, width), full(1, width), full(width, 2 * hy), full(1, 2 * hy)],
        out_specs=pl.BlockSpec((tm, 2 * hy), lambda i: (i, 0)),
        out_shape=jax.ShapeDtypeStruct((length, 2 * hy), F32),
        compiler_params=_cparams(("parallel",)),
    )(feats, t, w1p, b1[None], w2, b2[None], w3, b3[None], freq[None], w4, jnp.tile(deltas, 2)[None])
    return h[:, :hy], h[:, hy:]


def _taps_spectrum(fwd, h_fwd, h_bwd):
    length = h_fwd.shape[0]
    sig = jnp.stack([h_fwd, h_bwd.at[0].set(0.0)]).astype(BF16)
    spec = _bmm_shared_lhs(fwd, sig, F32, MM_TILE, MM_TILE)
    re, im = spec[:, :length], spec[:, length:]
    scale = 1.0 / (2 * length)
    row0 = lax.broadcasted_iota(jnp.int32, (length, 1), 0) == 0
    t_re = (re[0] + re[1]) * scale
    t_im = jnp.where(row0, im[0] + im[1], im[0] - im[1]) * scale
    return jnp.concatenate([t_re, t_im], axis=0).astype(BF16)


def _scan_kernel(r_ref, k_ref, v_ref, lw_ref, la_ref, w2_ref, a2_ref, w0_ref, a0_ref, kks_ref, ka_ref, rk_ref,
                 s0_ref, y_ref, bonus_ref, sfin_ref, ht_ref):
    c = CHUNK
    w = PACK_W
    direction = pl.program_id(0)
    step = pl.program_id(2)
    n_batch = r_ref.shape[0]
    n_groups = r_ref.shape[2] // w
    lora = w2_ref.shape[0]

    @pl.when(step == 0)
    def _():
        ht_ref[...] = s0_ref[...]

    sign = 1 - 2 * direction
    d_sq = (lax.broadcasted_iota(jnp.int32, (c, c), 1) - lax.broadcasted_iota(jnp.int32, (c, c), 0)) * sign
    incl_sq = jnp.where(d_sq <= 0, 1.0, 0.0).astype(BF16)
    d_c = (lax.broadcasted_iota(jnp.int32, (c, w), 1) % c - lax.broadcasted_iota(jnp.int32, (c, w), 0)) * sign
    strict_c = d_c < 0
    incl_c = d_c <= 0
    eye_c = jnp.where(d_c == 0, 1.0, 0.0)
    bd_mask = (lax.broadcasted_iota(jnp.int32, (w, w), 0) // c) == (lax.broadcasted_iota(jnp.int32, (w, w), 1) // HEAD)
    ones_bd = jnp.where(bd_mask, 1.0, 0.0).astype(BF16)

    def bd(x):
        xb = x.astype(BF16)
        return jnp.where(bd_mask, jnp.concatenate([xb] * PACK, axis=0), jnp.zeros((), BF16))

    fwd = direction == 0
    incl_2 = jnp.concatenate([incl_sq, incl_sq], axis=1)
    prep = []
    for bi in range(n_batch):
        r_all = r_ref[bi].astype(F32)
        k_all = k_ref[bi].astype(F32)
        v_all = v_ref[bi].astype(F32)
        lw_pre = lw_ref[bi].astype(F32)
        la_pre = la_ref[bi].astype(F32)
        lw_pre = jnp.where(fwd, lw_pre[:, :lora], lw_pre[:, lora:])
        la_pre = jnp.where(fwd, la_pre[:, :lora], la_pre[:, lora:])
        x_w = w0_ref[...] + _dot(jnp.tanh(lw_pre), w2_ref[...])
        lw_all = -math.exp(-0.5) * jax.nn.sigmoid(x_w)
        a_sig = jax.nn.sigmoid(a0_ref[...] + _dot(la_pre, a2_ref[...]))
        kk = k_all * kks_ref[...]
        kk = kk * lax.rsqrt(jnp.maximum(_group_sum(kk * kk, ones_bd), 1e-24))
        kd_all = k_all * (1.0 + (a_sig - 1.0) * ka_ref[...])
        bb_all = kk * a_sig
        bonus_ref[bi] = (_group_sum(r_all * kd_all * rk_ref[...], ones_bd) * v_all).astype(bonus_ref.dtype)

        lw_hi = lw_all.astype(BF16)
        lw_lo = (lw_all - lw_hi.astype(F32)).astype(BF16)
        cum = jnp.dot(incl_2, jnp.concatenate([lw_hi, lw_lo], axis=0), preferred_element_type=F32)
        tot = jnp.sum(lw_all, axis=0, keepdims=True)
        g_inv = jnp.exp(-cum)
        g_rem = jnp.exp(tot - cum)
        prep.append(dict(v=v_all, g_tot=jnp.exp(tot), a_t=-kk * jnp.exp(cum - lw_all), r_t=r_all * jnp.exp(cum),
                         b_t=bb_all * g_inv, k_t=kd_all * g_inv, b_s=bb_all * g_rem, k_s=kd_all * g_rem))

    probs = [(bi, g) for bi in range(n_batch) for g in range(n_groups)]
    idx = range(len(probs))
    cols = [slice(g * w, (g + 1) * w) for _, g in probs]
    take = lambda name: [prep[bi][name][:, cols[q]] for q, (bi, _) in enumerate(probs)]
    v_g, a_t, r_t, b_t, k_t, b_s, k_s = (take(nm) for nm in ("v", "a_t", "r_t", "b_t", "k_t", "b_s", "k_s"))
    hts = [ht_ref[bi, g] for bi, g in probs]
    lhs = [jnp.concatenate([a_t[q], r_t[q]], axis=0) for q in idx]
    scores = [_dot(lhs[q], jnp.concatenate([bd(b_t[q]).T, bd(k_t[q]).T], axis=1)) for q in idx]
    h0 = [_dot(lhs[q], hts[q].T) for q in idx]
    a_ab = [jnp.where(strict_c, s[:c, :w], 0.0) for s in scores]
    a_ak = [jnp.where(strict_c, s[:c, w:], 0.0) for s in scores]
    a_rb = [jnp.where(incl_c, s[c:, :w], 0.0) for s in scores]
    a_rk = [jnp.where(incl_c, s[c:, w:], 0.0) for s in scores]
    bd_v = [bd(vq) for vq in v_g]
    av = [_dot(jnp.concatenate([a_ak[q], a_rk[q]], axis=0), bd_v[q]) for q in idx]
    x = [h0[q][:c] + av[q][:c] for q in idx]

    t_inv = [eye_c + a for a in a_ab]
    p = [_dot(a, bd(a)) for a in a_ab]
    n_sq = int(math.log2(c)) - 1
    for it in range(n_sq):
        bd_p = [bd(pq) for pq in p]
        if it + 1 < n_sq:
            pt = [_dot(jnp.concatenate([p[q], t_inv[q]], axis=0), bd_p[q]) for q in idx]
            p = [m[:c] for m in pt]
            t_inv = [t_inv[q] + pt[q][c:] for q in idx]
        else:
            t_inv = [t_inv[q] + _dot(t_inv[q], bd_p[q]) for q in idx]
    u = [_dot(t_inv[q], bd(x[q])) for q in idx]

    y = [h0[q][c:] + av[q][c:] + _dot(a_rb[q], bd(u[q])) for q in idx]
    for q, (bi, _) in enumerate(probs):
        y_ref[bi, :, cols[q]] = y[q].astype(y_ref.dtype)

    for q, (bi, g) in enumerate(probs):
        uv = jnp.concatenate([u[q], v_g[q]], axis=0).astype(BF16)
        bk = jnp.concatenate([b_s[q], k_s[q]], axis=0).astype(BF16)
        upd = lax.dot_general(uv, bk, (((0,), (0,)), ((), ())), preferred_element_type=F32)
        ht_ref[bi, g] = hts[q] * prep[bi]["g_tot"][:, cols[q]] + jnp.where(bd_mask, upd, 0.0)

    @pl.when(step == pl.num_programs(2) - 1)
    def _():
        sfin_ref[...] = ht_ref[...]


def _rwkv_scan(z, col0, lora_col0, w2, a2, w0, a0, kks, ka, rk, s0):
    nb, seq, _ = z.shape
    lora, ch = w2.shape[1:]
    c = CHUNK
    assert CHUNK == HEAD and ch % PACK_W == 0 and seq % c == 0 and 2 * lora == LANE
    assert col0 % ch == 0 and lora_col0 % LANE == 0
    nc = seq // c
    ng = ch // PACK_W
    bs = SCAN_BATCH if nb % SCAN_BATCH == 0 else 1
    chunk_of = lambda n, i: jnp.where(n == 0, i, nc - 1 - i)
    zcol = lambda k: pl.BlockSpec((bs, c, ch), lambda n, b, i: (b, chunk_of(n, i), col0 // ch + k))
    zlora = lambda k: pl.BlockSpec((bs, c, LANE), lambda n, b, i: (b, chunk_of(n, i), lora_col0 // LANE + k))
    per_dir_w = pl.BlockSpec((None, lora, ch), lambda n, b, i: (n, 0, 0))
    per_dir_v = pl.BlockSpec((None, 1, ch), lambda n, b, i: (n, 0, 0))
    shared_v = pl.BlockSpec((1, ch), lambda n, b, i: (0, 0))
    state = pl.BlockSpec((None, bs, ng, PACK_W, PACK_W), lambda n, b, i: (n, b, 0, 0, 0))
    out = pl.BlockSpec((None, bs, c, ch), lambda n, b, i: (n, b, chunk_of(n, i), 0))
    return pl.pallas_call(
        _scan_kernel,
        grid=(2, nb // bs, nc),
        in_specs=[zcol(0), zcol(1), zcol(2), zlora(0), zlora(1), per_dir_w, per_dir_w, per_dir_v, per_dir_v,
                  shared_v, shared_v, shared_v, state],
        out_specs=[out, out, state],
        out_shape=[jax.ShapeDtypeStruct((2, nb, seq, ch), BF16), jax.ShapeDtypeStruct((2, nb, seq, ch), BF16),
                   jax.ShapeDtypeStruct((2, nb, ng, PACK_W, PACK_W), F32)],
        scratch_shapes=[pltpu.VMEM((bs, ng, PACK_W, PACK_W), F32)],
        compiler_params=_cparams(("parallel", "parallel", "arbitrary")),
    )(z, z, z, z, z, w2, a2, w0.reshape(2, 1, ch), a0.reshape(2, 1, ch), kks, ka, rk, s0)


def _outproj_kernel(yh_ref, ys0_ref, ys1_ref, b0_ref, b1_ref, lg_ref, g2_ref, lng_ref, lnb_ref, ow_ref,
                    x_ref, gate_ref, n2g_ref, shift_ref, scale_ref, rwh_ref, rwl_ref, rb_ref,
                    x1_ref, h2_ref, lg_out_ref, mix_ref):
    hy = yh_ref.shape[1]
    tm = x_ref.shape[0]
    ones = _group_ones(PACK_W)
    rows = _tile(tm, SUB_ROWS)
    subs = [slice(r0, r0 + rows) for r0 in range(0, tm, rows)]

    for rs in subs:
        y = ys0_ref[rs].astype(F32) + ys1_ref[rs].astype(F32)
        mu = _group_sum(y, ones) * (1.0 / HEAD)
        yc = y - mu
        var = _group_sum(yc * yc, ones) * (1.0 / HEAD)
        y = yc * lax.rsqrt(var + LNX_EPS) * lng_ref[...] + lnb_ref[...]
        gate = _dot(jax.nn.sigmoid(lg_ref[rs].astype(F32)), g2_ref[...])
        y = (y + b0_ref[rs].astype(F32) + b1_ref[rs].astype(F32)) * gate
        mix_ref[rs, :hy] = yh_ref[rs]
        mix_ref[rs, hy:] = y.astype(BF16)
    mixes = [jnp.dot(mix_ref[rs], ow_ref[...], preferred_element_type=F32) for rs in subs]
    for rs, mix in zip(subs, mixes):
        x1 = x_ref[rs] + gate_ref[...] * mix
        x1_ref[rs] = x1
        h2 = x1 * lax.rsqrt(jnp.mean(x1 * x1, axis=-1, keepdims=True) + NORM_EPS) * n2g_ref[...]
        h2 = h2 * (1.0 + scale_ref[...]) + shift_ref[...]
        h2_ref[rs] = h2.astype(BF16)
        h_hi = h2.astype(BF16)
        h_lo = (h2 - h_hi.astype(F32)).astype(BF16)
        logits = (jnp.dot(h_hi, rwh_ref[...], preferred_element_type=F32)
                  + jnp.dot(h_hi, rwl_ref[...], preferred_element_type=F32)
                  + jnp.dot(h_lo, rwh_ref[...], preferred_element_type=F32))
        lg_out_ref[rs] = logits + rb_ref[...]


def _outproj(y_hy, ys, bonus, z, lora_g_col, g2, lnx_g, lnx_b, out_w, x, gate, n2g, shift, scale,
             rw_hi, rw_lo, rb, tm):
    nb, seq, d = x.shape
    m = nb * seq
    hy = y_hy.shape[2]
    rw = ys.shape[3]
    tm = _tile(seq, tm)
    lg = g2.shape[0]
    ne = rw_hi.shape[1]
    assert lora_g_col % lg == 0
    ys2 = ys.reshape(2, m, rw)
    bn2 = bonus.reshape(2, m, rw)
    nz = z.shape[2]
    full = lambda r, c: pl.BlockSpec((r, c), lambda i: (0, 0))
    dir_spec = lambda n: pl.BlockSpec((None, tm, rw), lambda i: (n, i, 0))
    mod_spec = pl.BlockSpec((None, 1, d), lambda i: ((i * tm) // seq, 0, 0))
    return pl.pallas_call(
        _outproj_kernel,
        grid=(m // tm,),
        in_specs=[pl.BlockSpec((tm, hy), lambda i: (i, 0)), dir_spec(0), dir_spec(1), dir_spec(0), dir_spec(1),
                  pl.BlockSpec((tm, lg), lambda i: (i, lora_g_col // lg)),
                  full(lg, rw), full(1, rw), full(1, rw), full(hy + rw, d),
                  pl.BlockSpec((tm, d), lambda i: (i, 0)), mod_spec, full(1, d), mod_spec, mod_spec,
                  full(d, ne), full(d, ne), full(1, ne)],
        out_specs=[pl.BlockSpec((tm, d), lambda i: (i, 0)), pl.BlockSpec((tm, d), lambda i: (i, 0)),
                   pl.BlockSpec((tm, ne), lambda i: (i, 0))],
        out_shape=[jax.ShapeDtypeStruct((m, d), F32), jax.ShapeDtypeStruct((m, d), BF16),
                   jax.ShapeDtypeStruct((m, ne), F32)],
        scratch_shapes=[pltpu.VMEM((tm, hy + rw), BF16)],
        compiler_params=_cparams(("parallel",)),
    )(y_hy.reshape(m, hy), ys2, ys2, bn2, bn2, z.reshape(m, nz), g2, lnx_g, lnx_b, out_w,
      x.reshape(m, d), gate, n2g, shift, scale, rw_hi, rw_lo, rb)


def _moe_up_kernel(te_ref, nu_ref, x_ref, wg_ref, wu_ref, bg_ref, bu_ref, h_ref):
    i = pl.program_id(1)

    @pl.when(i < nu_ref[0])
    def _():
        x = x_ref[...]
        g = jnp.dot(x, wg_ref[...].astype(BF16), preferred_element_type=F32) + bg_ref[...]
        u = jnp.dot(x, wu_ref[...].astype(BF16), preferred_element_type=F32) + bu_ref[...]
        g = jnp.minimum(g, SWIGLU_LIMIT)
        u = jnp.clip(u, -SWIGLU_LIMIT, SWIGLU_LIMIT)
        h_ref[...] = ((u + 1.0) * (g * jax.nn.sigmoid(SWIGLU_ALPHA * g))).astype(h_ref.dtype)

    @pl.when(i >= nu_ref[0])
    def _():
        h_ref[...] = jnp.zeros_like(h_ref)


def _moe_down_kernel(te_ref, nu_ref, h_ref, wd_ref, bd_ref, *rest):
    y_ref = rest[-1]
    i = pl.program_id(1)

    @pl.when(i < nu_ref[0])
    def _():
        y = jnp.dot(h_ref[...], wd_ref[...].astype(BF16), preferred_element_type=F32) + bd_ref[...]
        y_ref[...] = y.astype(y_ref.dtype)

    @pl.when(i >= nu_ref[0])
    def _():
        y_ref[...] = jnp.zeros_like(y_ref)


def _moe_up(tile_expert, n_used, xs, wg, wu, bg, bu, tn):
    n_rows, d = xs.shape
    e, _, f = wg.shape
    tm = MOE_TILE
    tn = _tile(f, tn)
    w_spec = pl.BlockSpec((None, d, tn), lambda j, i, te, nu: (te[i], 0, j))
    b_spec = pl.BlockSpec((None, 1, tn), lambda j, i, te, nu: (te[i], 0, j))
    return pl.pallas_call(
        _moe_up_kernel,
        grid_spec=pltpu.PrefetchScalarGridSpec(
            num_scalar_prefetch=2,
            grid=(f // tn, n_rows // tm),
            in_specs=[pl.BlockSpec((tm, d), lambda j, i, te, nu: (i, 0)), w_spec, w_spec, b_spec, b_spec],
            out_specs=pl.BlockSpec((tm, tn), lambda j, i, te, nu: (i, j))),
        out_shape=jax.ShapeDtypeStruct((n_rows, f), BF16),
        compiler_params=_cparams(("arbitrary", "arbitrary")),
    )(tile_expert, n_used, xs, wg, wu, bg.reshape(e, 1, f), bu.reshape(e, 1, f))


def _moe_down(tile_expert, n_used, hs, wd, bdn, tn, total_rows, tile0, ys_prev):
    n_rows, f = hs.shape
    e, _, d = wd.shape
    tm = MOE_TILE
    tn = _tile(d, tn)
    in_specs = [pl.BlockSpec((tm, f), lambda j, i, te, nu: (i, 0)),
                pl.BlockSpec((None, f, tn), lambda j, i, te, nu: (te[i], 0, j)),
                pl.BlockSpec((None, 1, tn), lambda j, i, te, nu: (te[i], 0, j))]
    args = [tile_expert, n_used, hs, wd, bdn.reshape(e, 1, d)]
    aliases = {}
    if ys_prev is not None:
        in_specs.append(pl.BlockSpec(memory_space=pl.ANY))
        args.append(ys_prev)
        aliases = {len(args) - 1: 0}
    return pl.pallas_call(
        _moe_down_kernel,
        grid_spec=pltpu.PrefetchScalarGridSpec(
            num_scalar_prefetch=2,
            grid=(d // tn, n_rows // tm),
            in_specs=in_specs,
            out_specs=pl.BlockSpec((tm, tn), lambda j, i, te, nu: (tile0 + i, j))),
        out_shape=jax.ShapeDtypeStruct((total_rows, d), BF16),
        input_output_aliases=aliases,
        compiler_params=_cparams(("arbitrary", "arbitrary")),
    )(*args)


def _route_kernel(n_exp, lg_ref, idx_ref, gate_ref, rank_ref, cnt_ref, carry_ref):
    tm, lanes = lg_ref.shape

    @pl.when(pl.program_id(0) == 0)
    def _():
        carry_ref[...] = jnp.zeros_like(carry_ref)

    lane = lax.broadcasted_iota(jnp.int32, (tm, lanes), 1).astype(F32)
    lg = jnp.where(lane < n_exp, lg_ref[...], -jnp.inf)
    tops, hots, idxs = [], [], []
    for _ in range(TOP_K):
        m = jnp.max(lg, axis=-1, keepdims=True)
        idx = jnp.min(jnp.where(lg == m, lane, float(lanes)), axis=-1, keepdims=True)
        hot = lane == idx
        lg = jnp.where(hot, -jnp.inf, lg)
        tops.append(m)
        hots.append(hot)
        idxs.append(idx)
    exps = [jnp.exp(t - tops[0]) for t in tops]
    inv = 1.0 / sum(exps)
    occ = sum(jnp.where(h, 1.0, 0.0) for h in hots)
    earlier = jnp.where(lax.broadcasted_iota(jnp.int32, (tm, tm), 0) > lax.broadcasted_iota(jnp.int32, (tm, tm), 1),
                        1.0, 0.0).astype(BF16)
    before = jnp.dot(earlier, occ.astype(BF16), preferred_element_type=F32) + carry_ref[...]
    ranks = [jnp.sum(jnp.where(h, before, 0.0), axis=-1, keepdims=True) for h in hots]
    carry_ref[...] += jnp.sum(occ, axis=0, keepdims=True)
    cnt_ref[...] = carry_ref[...]
    spread = lambda cols: sum(jnp.where(lane == float(k), c, 0.0) for k, c in enumerate(cols))
    idx_ref[...] = spread(idxs)
    gate_ref[...] = spread([e * inv for e in exps])
    rank_ref[...] = spread(ranks)


def _route(logits, n_exp, tm):
    n_tok, lanes = logits.shape
    tm = _tile(n_tok, tm)
    row = pl.BlockSpec((tm, lanes), lambda i: (i, 0))
    out = jax.ShapeDtypeStruct((n_tok, lanes), F32)
    idx, gates, rank, cnt = pl.pallas_call(
        functools.partial(_route_kernel, n_exp),
        grid=(n_tok // tm,),
        in_specs=[row],
        out_specs=[row, row, row, pl.BlockSpec((1, lanes), lambda i: (0, 0))],
        out_shape=[out, out, out, jax.ShapeDtypeStruct((1, lanes), F32)],
        scratch_shapes=[pltpu.VMEM((1, lanes), F32)],
        compiler_params=_cparams(("arbitrary",)),
    )(logits)
    return (idx[:, :TOP_K].astype(jnp.int32), gates[:, :TOP_K], rank[:, :TOP_K].astype(jnp.int32),
            cnt[0, :n_exp].astype(jnp.int32))


def _moe(h2, logits, e, wg, bg, wu, bu, wd, bdn):
    n_tok, d = h2.shape
    tm = MOE_TILE
    top_idx, gates, rank, counts = _route(logits, e, ROUTE_TILE)
    n_asg = n_tok * TOP_K
    raw_start = jnp.cumsum(counts) - counts
    padded = (counts + tm - 1) // tm * tm
    pad_end = jnp.cumsum(padded)
    pad_start = pad_end - padded
    dest = (pad_start[top_idx] + rank).reshape(n_asg)
    tok = jnp.arange(n_asg, dtype=jnp.int32) // TOP_K
    _, order_tok = lax.sort((dest, tok), num_keys=1)
    n_tiles = -(-(n_asg + e * (tm - 1)) // tm)
    n_rows = n_tiles * tm
    tile_start = jnp.arange(n_tiles, dtype=jnp.int32) * tm
    tile_expert = jnp.minimum(jnp.sum((pad_end[None, :] <= tile_start[:, None]).astype(jnp.int32), axis=1), e - 1)
    n_used = pad_end[-1] // tm
    row = jnp.arange(n_rows, dtype=jnp.int32)
    row_e = jnp.repeat(tile_expert, tm)
    row_rank = row - pad_start[row_e]
    src = jnp.clip(raw_start[row_e] + row_rank, 0, n_asg - 1)
    row_tok = jnp.where(row_rank < counts[row_e], order_tok[src], 0)

    n_parts = next(p for p in (MOE_PARTS, 2, 1) if n_tiles % p == 0)
    tp = n_tiles // n_parts
    ys = None
    for q in range(n_parts):
        te_q = tile_expert[q * tp:(q + 1) * tp]
        nu_q = jnp.clip(n_used - q * tp, 0, tp).astype(jnp.int32).reshape(1)
        xs = h2[row_tok[q * tp * tm:(q + 1) * tp * tm]]
        hs = _moe_up(te_q, nu_q, xs, wg, wu, bg, bu, MOE_COLS)
        ys = _moe_down(te_q, nu_q, hs, wd, bdn, MOE_COLS, n_rows, q * tp, ys)
    return ys, dest.reshape(n_tok, TOP_K).T, gates


def _combine_kernel(y_ref, g_ref, x1_ref, gate_ref, fg_ref, *rest):
    o_ref = rest[-1]
    g = g_ref[...]
    acc = y_ref[0].astype(F32) * g[:, 0:1]
    for k in range(1, y_ref.shape[0]):
        acc = acc + y_ref[k].astype(F32) * g[:, k:k + 1]
    xo = x1_ref[...] + gate_ref[...] * acc
    o_ref[...] = xo * lax.rsqrt(jnp.mean(xo * xo, axis=-1, keepdims=True) + NORM_EPS) * fg_ref[...]


def _combine(picked, gates, x1, gate, final_g, seq, tm, tile0, out_prev):
    nk, m, d = picked.shape
    total = x1.shape[0]
    in_specs = [pl.BlockSpec((nk, tm, d), lambda i: (0, i, 0)),
                pl.BlockSpec((tm, nk), lambda i: (tile0 + i, 0)),
                pl.BlockSpec((tm, d), lambda i: (tile0 + i, 0)),
                pl.BlockSpec((None, 1, d), lambda i: (((tile0 + i) * tm) // seq, 0, 0)),
                pl.BlockSpec((1, d), lambda i: (0, 0))]
    args = [picked, gates, x1, gate, final_g]
    aliases = {}
    if out_prev is not None:
        in_specs.append(pl.BlockSpec(memory_space=pl.ANY))
        args.append(out_prev)
        aliases = {len(args) - 1: 0}
    return pl.pallas_call(
        _combine_kernel,
        grid=(m // tm,),
        in_specs=in_specs,
        out_specs=pl.BlockSpec((tm, d), lambda i: (tile0 + i, 0)),
        out_shape=jax.ShapeDtypeStruct((total, d), F32),
        input_output_aliases=aliases,
        compiler_params=_cparams(("parallel",)),
    )(*args)


def kernel(x, c, ctx, c_ctx, ada_w, ada_b, norm1_g, norm2_g, in_w, conv_w, conv_b, hy_w1, hy_b1, hy_w2, hy_b2, hy_w3, hy_b3, hy_freq, hy_w4, hy_bias, hy_norm_g, rw_w0, rw_w1, rw_w2, rw_a0, rw_a1, rw_a2, rw_kk, rw_ka, rw_rk, rw_g1, rw_g2, rw_lnx_g, rw_lnx_b, out_w, router_w, router_b, ex_w_gate, ex_b_gate, ex_w_up, ex_b_up, ex_w_down, ex_b_down, final_g):
    assert ada_w.shape[0] == 1, "single-layer block: context outputs never reach a latent token"
    nb, seq, d = x.shape
    hy = hy_bias.shape[1]
    rw = rw_kk.shape[1]
    n_hy = 3 * hy
    proj = in_w.shape[2]
    n_exp = router_w.shape[2]

    cond = jnp.concatenate([jax.nn.silu(c), jax.nn.silu(c_ctx)[None]], axis=0)
    mod = _matmul(cond, ada_w[0], F32, cond.shape[0], MM_TILE) + ada_b[0]
    mod_x = [m[:, None, :] for m in jnp.split(mod[:nb], 6, axis=-1)]
    mod_c = [m[:, None, :] for m in jnp.split(mod[nb:], 6, axis=-1)]

    lora_w = jnp.concatenate([rw_w1[0, 0], rw_w1[0, 1], rw_a1[0, 0], rw_a1[0, 1], rw_g1[0]], axis=1)
    n_lora = lora_w.shape[1]
    nz = -(-(proj + n_lora) // 512) * 512
    pad = nz - proj - n_lora
    w_all = jnp.concatenate([in_w[0], lora_w, jnp.zeros((d, pad), F32)], axis=1).astype(BF16)
    pass_taps = jnp.concatenate([jnp.zeros((1, n_lora + pad), F32), jnp.ones((1, n_lora + pad), F32),
                                 jnp.zeros((1, n_lora + pad), F32)], axis=0)
    cw_all = jnp.concatenate([conv_w[0], pass_taps], axis=1)
    cb_all = jnp.concatenate([conv_b[0], jnp.zeros((n_lora + pad,), F32)])[None]
    g1 = norm1_g[0][None]
    zx = _inproj(x, mod_x[0], mod_x[1], g1, w_all, cw_all, cb_all, *PROJ_TILE)
    zc = _inproj(ctx, mod_c[0], mod_c[1], g1, w_all[:, n_hy:], cw_all[:, n_hy:], cb_all[:, n_hy:], *CTX_PROJ_TILE)

    h_fwd, h_bwd = _hyena_filter(seq, hy_w1[0], hy_b1[0], hy_w2[0], hy_b2[0], hy_w3[0], hy_b3[0],
                                 hy_freq[0], hy_w4[0])
    fwd, inv = _dft_matrices(seq)
    taps = _taps_spectrum(fwd, h_fwd, h_bwd)
    spec = _dft_fwd(fwd, zx, hy, *DFT_FWD_TILE)
    y_hy = _dft_inv(inv, spec, taps, zx, hy_bias[0][None], hy_norm_g[0][None], *DFT_TILE)

    scan_args = (rw_w2[0].astype(BF16), rw_a2[0].astype(BF16), rw_w0[0], rw_a0[0],
                 rw_kk[0][None], rw_ka[0][None], rw_rk[0].reshape(1, rw))
    s0 = jnp.zeros((2, nb, rw // PACK_W, PACK_W, PACK_W), F32)
    _, _, s_ctx = _rwkv_scan(zc, 0, proj - n_hy, *scan_args, s0)
    ys, bonus, _ = _rwkv_scan(zx, n_hy, proj, *scan_args, s_ctx)

    ne = -(-n_exp // LANE) * LANE
    rw_pad = jnp.pad(router_w[0], ((0, 0), (0, ne - n_exp)))
    rw_hi = rw_pad.astype(BF16)
    rw_lo = (rw_pad - rw_hi.astype(F32)).astype(BF16)
    rb = jnp.pad(router_b[0], (0, ne - n_exp))[None]
    x1, h2, logits = _outproj(y_hy, ys, bonus, zx, proj + n_lora - rw_g1.shape[-1], rw_g2[0].astype(BF16),
                              rw_lnx_g[0][None], rw_lnx_b[0][None], out_w[0].astype(BF16), x,
                              mod_x[2], norm2_g[0][None], mod_x[3], mod_x[4], rw_hi, rw_lo, rb, OUTPROJ_ROWS)

    ys, dest_km, gates = _moe(h2, logits, n_exp, ex_w_gate[0], ex_b_gate[0], ex_w_up[0], ex_b_up[0],
                              ex_w_down[0], ex_b_down[0])
    n_tok = nb * seq
    tm = _tile(seq, COMBINE_ROWS)
    n_parts = next(p for p in (COMBINE_PARTS, 2, 1) if (n_tok // tm) % p == 0)
    tq = n_tok // n_parts
    out = None
    for q in range(n_parts):
        picked = ys[dest_km[:, q * tq:(q + 1) * tq]]
        out = _combine(picked, gates, x1, mod_x[5], final_g[None], seq, tm, q * (tq // tm), out)
    return out.reshape(nb, seq, d)
```
